```python
import jax, jax.numpy as jnp
from jax import lax
import numpy as np

D_MODEL = 2048
BATCH = 2
SEQ = 16384
DEPTH = 2

CHUNK = 64
Q_BLOCK = 128
N_BRANCH = 3
D_RNN = D_MODEL // 2
N_RNN_BLOCKS = 8
RNN_BLOCK = D_RNN // N_RNN_BLOCKS
CONV_WIDTH = 4
LRU_C = 8.0
N_HEADS = 8
HEAD_DIM = 128
D_ATTN = N_HEADS * HEAD_DIM
D_POOL = D_MODEL // 2
POOL_WINDOWS = (2, 4, 8, 16)
N_POOL_GROUPS = 4
POOL_GROUP = D_POOL // N_POOL_GROUPS
D_FF = ((-(-8 * D_MODEL // 3) + 255) // 256) * 256
NORM_EPS = 1e-6
IN_SIZES = (D_RNN, D_RNN, D_ATTN, D_ATTN, D_ATTN, N_HEADS, D_POOL, N_BRANCH * D_MODEL)
D_IN = D_RNN * 2 + D_ATTN * 3 + N_HEADS + D_POOL + N_BRANCH * D_MODEL

kernel_name = "hybrid_rglru_fox_pool_block"


def rms_norm(x, g):
    xf = x.astype(jnp.float32)
    y = xf * lax.rsqrt(jnp.mean(xf * xf, axis=-1, keepdims=True) + NORM_EPS)
    return (y * g.astype(jnp.float32)).astype(x.dtype)


def split_columns(p):
    offsets = []
    acc = 0
    for s in IN_SIZES[:-1]:
        acc += s
        offsets.append(acc)
    return jnp.split(p, offsets, axis=-1)


def causal_depthwise_conv(u, w, b):
    S = u.shape[1]
    up = jnp.pad(u, ((0, 0), (CONV_WIDTH - 1, 0), (0, 0)))
    out = b + up[:, 0:S] * w[0]
    for k in range(1, CONV_WIDTH):
        out = out + up[:, k:k + S] * w[k]
    return out


def rg_lru(u, w_r, b_r, w_i, b_i, lam):
    B, S, _ = u.shape
    ub = u.reshape(B, S, N_RNN_BLOCKS, RNN_BLOCK)
    r = jax.nn.sigmoid(jnp.einsum('bsnc,ncd->bsnd', ub, w_r).reshape(B, S, D_RNN) + b_r)
    i = jax.nn.sigmoid(jnp.einsum('bsnc,ncd->bsnd', ub, w_i).reshape(B, S, D_RNN) + b_i)
    log_a = (LRU_C * r.astype(jnp.float32)) * jax.nn.log_sigmoid(lam.astype(jnp.float32))
    a = jnp.exp(log_a)
    b_in = jnp.sqrt(-jnp.expm1(2.0 * log_a)) * (i * u).astype(jnp.float32)

    def combine(left, right):
        a1, b1 = left
        a2, b2 = right
        return a1 * a2, a2 * b1 + b2

    _, h = lax.associative_scan(combine, (a, b_in), axis=1)
    return h.astype(u.dtype)


def forgetting_attention(q, k, v, f_logit):
    B, S, H, Dh = q.shape
    nb = S // Q_BLOCK
    scale = Dh ** -0.5
    F = jnp.cumsum(jax.nn.log_sigmoid(f_logit.astype(jnp.float32)), axis=1).transpose(0, 2, 1)
    k_h = k.transpose(0, 2, 1, 3)
    v_h = v.transpose(0, 2, 1, 3)
    q_blocks = q.reshape(B, nb, Q_BLOCK, H, Dh).transpose(1, 0, 3, 2, 4)
    F_blocks = F.reshape(B, H, nb, Q_BLOCK).transpose(2, 0, 1, 3)
    k_pos = jnp.arange(S)

    def one_block(args):
        qb, Fq, blk = args
        s = jnp.einsum('bhqd,bhkd->bhqk', qb, k_h, preferred_element_type=jnp.float32) * scale
        s = s + Fq[..., None] - F[:, :, None, :]
        q_pos = blk * Q_BLOCK + jnp.arange(Q_BLOCK)
        s = jnp.where(q_pos[:, None] >= k_pos[None, :], s, -jnp.inf)
        p = jax.nn.softmax(s, axis=-1)
        return jnp.einsum('bhqk,bhkd->bhqd', p.astype(v_h.dtype), v_h)

    o = lax.map(one_block, (q_blocks, F_blocks, jnp.arange(nb)))
    return o.transpose(1, 0, 3, 2, 4).reshape(B, S, H * Dh)


def multiscale_pool(u, w_pool, scale):
    B, S, _ = u.shape
    uf = u.astype(jnp.float32).reshape(B, S, N_POOL_GROUPS, POOL_GROUP)
    cs = jnp.cumsum(uf, axis=1)
    t = jnp.arange(1, S + 1, dtype=jnp.float32)
    outs = []
    for g, w in enumerate(POOL_WINDOWS):
        c = cs[:, :, g]
        lagged = jnp.pad(c, ((0, 0), (w, 0), (0, 0)))[:, :S]
        cnt = jnp.minimum(t, w)[None, :, None]
        outs.append((c - lagged) / cnt - uf[:, :, g])
    pooled = jnp.stack(outs, axis=2).astype(u.dtype)
    y = jnp.einsum('bsgc,gcd->bsgd', pooled, w_pool).reshape(B, S, D_POOL)
    return y * scale


def hybrid_layer(x, g_mix, w_in, b_forget, conv_w, conv_b, w_rg, b_rg, w_ig, b_ig, lru_lambda,
                 w_pool, pool_scale, w_branch_rnn, w_branch_attn, w_branch_pool, w_out,
                 g_ffn, w_ffn_in, w_ffn_out):
    B, S, _ = x.shape
    h = rms_norm(x, g_mix)
    rnn_x, rnn_y, q, k, v, f_logit, pool_in, gate_logits = split_columns(h @ w_in)
    u = causal_depthwise_conv(rnn_x, conv_w, conv_b)
    y_a = jax.nn.gelu(rnn_y) * rg_lru(u, w_rg, b_rg, w_ig, b_ig, lru_lambda)
    y_b = forgetting_attention(q.reshape(B, S, N_HEADS, HEAD_DIM),
                               k.reshape(B, S, N_HEADS, HEAD_DIM),
                               v.reshape(B, S, N_HEADS, HEAD_DIM),
                               f_logit + b_forget)
    y_c = multiscale_pool(pool_in, w_pool, pool_scale)
    gates = jax.nn.sigmoid(gate_logits.astype(jnp.float32)).astype(x.dtype).reshape(B, S, N_BRANCH, D_MODEL)
    merged = (gates[:, :, 0] * (y_a @ w_branch_rnn)
              + gates[:, :, 1] * (y_b @ w_branch_attn)
              + gates[:, :, 2] * (y_c @ w_branch_pool))
    x = x + merged @ w_out
    h2 = rms_norm(x, g_ffn)
    gate, up = jnp.split(h2 @ w_ffn_in, 2, axis=-1)
    return x + (jax.nn.silu(gate) * up) @ w_ffn_out


def setup_inputs(seed: int = 0) -> dict:
    key = jax.random.key(seed)
    ks = jax.random.split(key, 24)
    f32 = jnp.float32
    nrm = lambda k, shape, s: jax.random.normal(k, shape, f32) * s
    a8 = jax.random.uniform(ks[10], (DEPTH, D_RNN), f32, 0.9, 0.999)
    a = a8 ** (1.0 / LRU_C)
    lru_lambda = jnp.log(a) - jnp.log1p(-a)
    return {
        "x": jax.random.normal(ks[0], (BATCH, SEQ, D_MODEL), f32),
        "g_mix": 1.0 + nrm(ks[1], (DEPTH, D_MODEL), 0.05),
        "w_in": nrm(ks[2], (DEPTH, D_MODEL, D_IN), D_MODEL ** -0.5),
        "b_forget": jax.random.uniform(ks[3], (DEPTH, N_HEADS), f32, 1.0, 4.0),
        "conv_w": nrm(ks[4], (DEPTH, CONV_WIDTH, D_RNN), CONV_WIDTH ** -0.5),
        "conv_b": nrm(ks[5], (DEPTH, D_RNN), 0.02),
        "w_rg": nrm(ks[6], (DEPTH, N_RNN_BLOCKS, RNN_BLOCK, RNN_BLOCK), RNN_BLOCK ** -0.5),
        "b_rg": nrm(ks[7], (DEPTH, D_RNN), 0.02),
        "w_ig": nrm(ks[8], (DEPTH, N_RNN_BLOCKS, RNN_BLOCK, RNN_BLOCK), RNN_BLOCK ** -0.5),
        "b_ig": nrm(ks[9], (DEPTH, D_RNN), 0.02),
        "lru_lambda": lru_lambda,
        "w_pool": nrm(ks[11], (DEPTH, N_POOL_GROUPS, POOL_GROUP, POOL_GROUP), POOL_GROUP ** -0.5),
        "pool_scale": 1.0 + nrm(ks[12], (DEPTH, D_POOL), 0.1),
        "w_branch_rnn": nrm(ks[13], (DEPTH, D_RNN, D_MODEL), D_RNN ** -0.5),
        "w_branch_attn": nrm(ks[14], (DEPTH, D_ATTN, D_MODEL), D_ATTN ** -0.5),
        "w_branch_pool": nrm(ks[15], (DEPTH, D_POOL, D_MODEL), D_POOL ** -0.5),
        "w_out": nrm(ks[16], (DEPTH, D_MODEL, D_MODEL), D_MODEL ** -0.5),
        "g_ffn": 1.0 + nrm(ks[17], (DEPTH, D_MODEL), 0.05),
        "w_ffn_in": nrm(ks[18], (DEPTH, D_MODEL, 2 * D_FF), D_MODEL ** -0.5),
        "w_ffn_out": nrm(ks[19], (DEPTH, D_FF, D_MODEL), D_FF ** -0.5),
        "g_final": 1.0 + nrm(ks[20], (D_MODEL,), 0.05),
    }


def reference(x, g_mix, w_in, b_forget, conv_w, conv_b, w_rg, b_rg, w_ig, b_ig, lru_lambda,
              w_pool, pool_scale, w_branch_rnn, w_branch_attn, w_branch_pool, w_out,
              g_ffn, w_ffn_in, w_ffn_out, g_final):
    for l in range(DEPTH):
        x = hybrid_layer(x, g_mix[l], w_in[l], b_forget[l], conv_w[l], conv_b[l], w_rg[l], b_rg[l],
                         w_ig[l], b_ig[l], lru_lambda[l], w_pool[l], pool_scale[l],
                         w_branch_rnn[l], w_branch_attn[l], w_branch_pool[l], w_out[l],
                         g_ffn[l], w_ffn_in[l], w_ffn_out[l])
    return rms_norm(x, g_final)
```

```python
import functools

import jax
import jax.numpy as jnp
from jax import lax
from jax.experimental import pallas as pl
from jax.experimental.pallas import tpu as pltpu

F32 = jnp.float32
BF16 = jnp.bfloat16

N_RNN_BLOCKS = 8
CONV_WIDTH = 4
LRU_C = 8.0
N_HEADS = 8
HEAD_DIM = 128
POOL_WINDOWS = (2, 4, 8, 16)
POOL_HALO = 16
CONV_HALO = 8
NORM_EPS = 1e-6
LANES = 128

VMEM_LIMIT = 56 * 1024 * 1024


def _params(*sem):
    return pltpu.CompilerParams(dimension_semantics=sem, vmem_limit_bytes=VMEM_LIMIT)


def _tile(n, want):
    t = min(n, want)
    while n % t:
        t //= 2
    return t


def _sigmoid(x):
    return 1.0 / (1.0 + jnp.exp(-x))


def _log_sigmoid(x):
    return jnp.minimum(x, 0.0) - jnp.log1p(jnp.exp(-jnp.abs(x)))


def _normalize_rows(x, g):
    ms = jnp.mean(x * x, axis=-1, keepdims=True)
    return x * lax.rsqrt(ms + NORM_EPS) * g


def _norm_matmul_kernel(x_ref, g_ref, w_ref, o_ref, h_ref):
    @pl.when(pl.program_id(1) == 0)
    def _():
        h_ref[...] = _normalize_rows(x_ref[...], g_ref[...]).astype(BF16)

    o_ref[...] = jnp.dot(h_ref[...], w_ref[...], preferred_element_type=F32).astype(o_ref.dtype)


def _norm_matmul(x, g, w, out_dtype, tm=1024, tn=512):
    M, D = x.shape
    N = w.shape[1]
    tm, tn = _tile(M, tm), _tile(N, tn)
    return pl.pallas_call(
        _norm_matmul_kernel,
        grid=(M // tm, N // tn),
        in_specs=[pl.BlockSpec((tm, D), lambda i, j: (i, 0)),
                  pl.BlockSpec((1, D), lambda i, j: (0, 0)),
                  pl.BlockSpec((D, tn), lambda i, j: (0, j))],
        out_specs=pl.BlockSpec((tm, tn), lambda i, j: (i, j)),
        out_shape=jax.ShapeDtypeStruct((M, N), out_dtype),
        scratch_shapes=[pltpu.VMEM((tm, D), BF16)],
        compiler_params=_params("parallel", "arbitrary"),
        name="norm_matmul",
    )(x, g, w)


def _norm_swiglu_kernel(x_ref, g_ref, wg_ref, wu_ref, o_ref, h_ref):
    @pl.when(pl.program_id(1) == 0)
    def _():
        h_ref[...] = _normalize_rows(x_ref[...], g_ref[...]).astype(BF16)

    h = h_ref[...]
    gate = jnp.dot(h, wg_ref[...], preferred_element_type=F32)
    up = jnp.dot(h, wu_ref[...], preferred_element_type=F32)
    o_ref[...] = (gate * _sigmoid(gate) * up).astype(o_ref.dtype)


def _norm_swiglu(x, g, w, tm=1024, tn=512):
    M, D = x.shape
    F = w.shape[1] // 2
    tm, tn = _tile(M, tm), _tile(F, tn)
    nf = F // tn
    return pl.pallas_call(
        _norm_swiglu_kernel,
        grid=(M // tm, nf),
        in_specs=[pl.BlockSpec((tm, D), lambda i, j: (i, 0)),
                  pl.BlockSpec((1, D), lambda i, j: (0, 0)),
                  pl.BlockSpec((D, tn), lambda i, j: (0, j)),
                  pl.BlockSpec((D, tn), lambda i, j: (0, j + nf))],
        out_specs=pl.BlockSpec((tm, tn), lambda i, j: (i, j)),
        out_shape=jax.ShapeDtypeStruct((M, F), BF16),
        scratch_shapes=[pltpu.VMEM((tm, D), BF16)],
        compiler_params=_params("parallel", "arbitrary"),
        name="norm_swiglu",
    )(x, g, w, w)


def _matmul_res_kernel(a_ref, w_ref, r_ref, o_ref):
    o_ref[...] = r_ref[...] + jnp.dot(a_ref[...], w_ref[...], preferred_element_type=F32)


def _matmul_res(a, w, res, tm=512, tn=512):
    M, K = a.shape
    N = w.shape[1]
    tm, tn = _tile(M, tm), _tile(N, tn)
    return pl.pallas_call(
        _matmul_res_kernel,
        grid=(M // tm, N // tn),
        in_specs=[pl.BlockSpec((tm, K), lambda i, j: (i, 0)),
                  pl.BlockSpec((K, tn), lambda i, j: (0, j)),
                  pl.BlockSpec((tm, tn), lambda i, j: (i, j))],
        out_specs=pl.BlockSpec((tm, tn), lambda i, j: (i, j)),
        out_shape=jax.ShapeDtypeStruct((M, N), F32),
        compiler_params=_params("parallel", "arbitrary"),
        name="matmul_res",
    )(a, w, res)


def _rmsnorm_kernel(x_ref, g_ref, o_ref):
    o_ref[...] = _normalize_rows(x_ref[...], g_ref[...])


def _rmsnorm(x, g, tm=512):
    M, D = x.shape
    tm = _tile(M, tm)
    return pl.pallas_call(
        _rmsnorm_kernel,
        grid=(M // tm,),
        in_specs=[pl.BlockSpec((tm, D), lambda i: (i, 0)),
                  pl.BlockSpec((1, D), lambda i: (0, 0))],
        out_specs=pl.BlockSpec((tm, D), lambda i: (i, 0)),
        out_shape=jax.ShapeDtypeStruct((M, D), F32),
        compiler_params=_params("parallel"),
        name="rmsnorm",
    )(x, g)


def _forget_cumsum_kernel(f_ref, b_ref, o_ref):
    x = _log_sigmoid(f_ref[0] + b_ref[0])
    R = x.shape[0]
    li = lax.broadcasted_iota(jnp.int32, (LANES, LANES), 0)
    lj = lax.broadcasted_iota(jnp.int32, (LANES, LANES), 1)
    upper = (li <= lj).astype(F32)
    within = jnp.dot(x, upper, preferred_element_type=F32, precision=lax.Precision.HIGHEST)
    ri = lax.broadcasted_iota(jnp.int32, (R, R), 0)
    rj = lax.broadcasted_iota(jnp.int32, (R, R), 1)
    strictly_lower = (rj < ri).astype(F32)
    row_total = jnp.broadcast_to(within[:, LANES - 1:LANES], (R, LANES))
    before = jnp.dot(strictly_lower, row_total, preferred_element_type=F32,
                     precision=lax.Precision.HIGHEST)
    o_ref[0] = within + before


def _forget_cumsum(f, b):
    BH, R, _ = f.shape
    return pl.pallas_call(
        _forget_cumsum_kernel,
        grid=(BH,),
        in_specs=[pl.BlockSpec((1, R, LANES), lambda i: (i, 0, 0)),
                  pl.BlockSpec((1, 1, LANES), lambda i: (i, 0, 0))],
        out_specs=pl.BlockSpec((1, R, LANES), lambda i: (i, 0, 0)),
        out_shape=jax.ShapeDtypeStruct((BH, R, LANES), F32),
        compiler_params=_params("parallel"),
        name="forget_cumsum",
    )(f, b)


def _attn_kernel(q_ref, k_ref, v_ref, f_ref, o_ref, *, blk, scale):
    i = pl.program_id(2)
    q = q_ref[...]

    def step(j, carry, masked):
        m, l, acc = carry
        start = pl.multiple_of(j * blk, blk)
        k = k_ref[pl.ds(start, blk), :]
        v = v_ref[pl.ds(start, blk), :]
        s = lax.dot_general(q, k, (((1,), (1,)), ((), ())), preferred_element_type=F32)
        s = s * scale - f_ref[0, j]
        if masked:
            row = lax.broadcasted_iota(jnp.int32, (blk, blk), 0)
            col = lax.broadcasted_iota(jnp.int32, (blk, blk), 1)
            s = jnp.where(row >= col, s, -jnp.inf)
        m_new = jnp.maximum(m, jnp.max(s, axis=-1, keepdims=True))
        alpha = jnp.exp(m - m_new)
        p = jnp.exp(s - m_new)
        l = alpha * l + jnp.sum(p, axis=-1, keepdims=True)
        acc = alpha * acc + jnp.dot(p.astype(BF16), v, preferred_element_type=F32)
        return m_new, l, acc

    init = (jnp.full((blk, 1), -jnp.inf, F32), jnp.zeros((blk, 1), F32),
            jnp.zeros((blk, HEAD_DIM), F32))
    carry = lax.fori_loop(0, i, functools.partial(step, masked=False), init)
    _, l, acc = step(i, carry, masked=True)
    o_ref[...] = (acc / l).astype(o_ref.dtype)


def _attention(qkv, fcum, B, S, blk=512):
    H, Dh = N_HEADS, HEAD_DIM
    blk = _tile(S, blk)
    nq = S // blk
    f = fcum.reshape(B * H, nq, 1, blk)
    kernel = functools.partial(_attn_kernel, blk=blk, scale=Dh ** -0.5)
    return pl.pallas_call(
        kernel,
        grid=(B, H, nq),
        in_specs=[pl.BlockSpec((blk, Dh), lambda b, h, i: (b * nq + i, h)),
                  pl.BlockSpec((S, Dh), lambda b, h, i: (b, H + h)),
                  pl.BlockSpec((S, Dh), lambda b, h, i: (b, 2 * H + h)),
                  pl.BlockSpec((1, nq, 1, blk), lambda b, h, i: (b * H + h, 0, 0, 0))],
        out_specs=pl.BlockSpec((blk, Dh), lambda b, h, i: (b * nq + i, h)),
        out_shape=jax.ShapeDtypeStruct((B * S, H * Dh), BF16),
        compiler_params=_params("parallel", "parallel", "arbitrary"),
        name="fox_attention",
    )(qkv, qkv, qkv, f)


def _gelu_tanh(x):
    c = 0.7978845608028654
    return 0.5 * x * (1.0 + jnp.tanh(c * (x + 0.044715 * (x * x * x))))


def _rglru_kernel(x_ref, y_ref, cw_ref, cb_ref, wr_ref, br_ref, wi_ref, bi_ref, lam_ref,
                  o_ref, ext_ref, a_ref, b_ref, h_ref, state_ref, *, T):
    C = x_ref.shape[1]
    cb = C // N_RNN_BLOCKS

    @pl.when(pl.program_id(1) == 0)
    def _():
        ext_ref[0:CONV_HALO, :] = jnp.zeros((CONV_HALO, C), F32)
        state_ref[...] = jnp.zeros_like(state_ref)

    x = x_ref[...]
    ext_ref[CONV_HALO:CONV_HALO + T, :] = x
    u = cb_ref[...] + x * cw_ref[CONV_WIDTH - 1:CONV_WIDTH, :]
    for k in range(CONV_WIDTH - 1):
        off = CONV_HALO + k - (CONV_WIDTH - 1)
        u = u + ext_ref[off:off + T, :] * cw_ref[k:k + 1, :]
    ext_ref[0:CONV_HALO, :] = x[T - CONV_HALO:T, :]

    ub = u.astype(BF16)
    r_parts, i_parts = [], []
    for n in range(N_RNN_BLOCKS):
        un = ub[:, n * cb:(n + 1) * cb]
        r_parts.append(jnp.dot(un, wr_ref[n], preferred_element_type=F32))
        i_parts.append(jnp.dot(un, wi_ref[n], preferred_element_type=F32))
    r = _sigmoid(jnp.concatenate(r_parts, axis=1) + br_ref[...])
    ig = _sigmoid(jnp.concatenate(i_parts, axis=1) + bi_ref[...])
    log_a = (LRU_C * r) * _log_sigmoid(lam_ref[...])
    a = jnp.exp(log_a)
    a_ref[...] = a
    b_ref[...] = jnp.sqrt(-jnp.tanh(log_a) * (a * a + 1.0)) * (ig * u)

    def group(g, h):
        base = pl.multiple_of(g * 8, 8)
        a8 = a_ref[pl.ds(base, 8), :]
        b8 = b_ref[pl.ds(base, 8), :]
        rows = []
        for s in range(8):
            h = a8[s:s + 1, :] * h + b8[s:s + 1, :]
            rows.append(h)
        h_ref[pl.ds(base, 8), :] = jnp.concatenate(rows, axis=0)
        return h

    h_last = lax.fori_loop(0, T // 8, group, state_ref[0:1, :])
    state_ref[0:1, :] = h_last
    o_ref[...] = (_gelu_tanh(y_ref[...]) * h_ref[...]).astype(o_ref.dtype)


def _rglru(xy, conv_w, conv_b, w_rg, b_rg, w_ig, b_ig, lam, B, S, T=512):
    C = xy.shape[1] // 2
    T = _tile(S, T)
    nt = S // T
    vec = pl.BlockSpec((1, C), lambda b, t: (0, 0))
    blockdiag = pl.BlockSpec(w_rg.shape, lambda b, t: (0, 0, 0))
    return pl.pallas_call(
        functools.partial(_rglru_kernel, T=T),
        grid=(B, nt),
        in_specs=[pl.BlockSpec((T, C), lambda b, t: (b * nt + t, 0)),
                  pl.BlockSpec((T, C), lambda b, t: (b * nt + t, 1)),
                  pl.BlockSpec((CONV_WIDTH, C), lambda b, t: (0, 0)),
                  vec, blockdiag, vec, blockdiag, vec, vec],
        out_specs=pl.BlockSpec((T, C), lambda b, t: (b * nt + t, 0)),
        out_shape=jax.ShapeDtypeStruct((B * S, C), BF16),
        scratch_shapes=[pltpu.VMEM((T + CONV_HALO, C), F32),
                        pltpu.VMEM((T, C), F32), pltpu.VMEM((T, C), F32), pltpu.VMEM((T, C), F32),
                        pltpu.VMEM((8, C), F32)],
        compiler_params=_params("arbitrary", "arbitrary"),
        name="rglru",
    )(xy, xy, conv_w, conv_b, w_rg, b_rg, w_ig, b_ig, lam)


def _pool_kernel(u_ref, w_ref, s_ref, o_ref, ext_ref, *, T):
    C = u_ref.shape[1]
    G = len(POOL_WINDOWS)
    cg = C // G
    t = pl.program_id(1)

    @pl.when(t == 0)
    def _():
        ext_ref[0:POOL_HALO, :] = jnp.zeros((POOL_HALO, C), F32)

    x = u_ref[...]
    ext_ref[POOL_HALO:POOL_HALO + T, :] = x
    pos = (t * T + 1 + lax.broadcasted_iota(jnp.int32, (T, 1), 0)).astype(F32)
    for g, w in enumerate(POOL_WINDOWS):
        lo = g * cg
        xs = x[:, lo:lo + cg]
        acc = xs
        for k in range(1, w):
            acc = acc + ext_ref[POOL_HALO - k:POOL_HALO - k + T, lo:lo + cg]
        pooled = acc / jnp.minimum(pos, float(w)) - xs
        y = jnp.dot(pooled.astype(BF16), w_ref[g], preferred_element_type=F32)
        o_ref[:, lo:lo + cg] = (y * s_ref[:, lo:lo + cg]).astype(o_ref.dtype)
    ext_ref[0:POOL_HALO, :] = x[T - POOL_HALO:T, :]


def _pool(u, w_pool, scale, B, S, T=512):
    C = u.shape[1]
    T = _tile(S, T)
    nt = S // T
    return pl.pallas_call(
        functools.partial(_pool_kernel, T=T),
        grid=(B, nt),
        in_specs=[pl.BlockSpec((T, C), lambda b, t: (b * nt + t, 0)),
                  pl.BlockSpec(w_pool.shape, lambda b, t: (0, 0, 0)),
                  pl.BlockSpec((1, C), lambda b, t: (0, 0))],
        out_specs=pl.BlockSpec((T, C), lambda b, t: (b * nt + t, 0)),
        out_shape=jax.ShapeDtypeStruct((B * S, C), BF16),
        scratch_shapes=[pltpu.VMEM((T + POOL_HALO, C), F32)],
        compiler_params=_params("arbitrary", "arbitrary"),
        name="multiscale_pool",
    )(u, w_pool, scale)


def _merge_kernel(ya_ref, yb_ref, yc_ref, wa_ref, wb_ref, wc_ref, ga_ref, gb_ref, gc_ref, o_ref):
    out = _sigmoid(ga_ref[...]) * jnp.dot(ya_ref[...], wa_ref[...], preferred_element_type=F32)
    out = out + _sigmoid(gb_ref[...]) * jnp.dot(yb_ref[...], wb_ref[...], preferred_element_type=F32)
    out = out + _sigmoid(gc_ref[...]) * jnp.dot(yc_ref[...], wc_ref[...], preferred_element_type=F32)
    o_ref[...] = out.astype(o_ref.dtype)


def _merge(ya, yb, yc, wa, wb, wc, gates, tm=1024, tn=512):
    M, C = ya.shape
    D = wa.shape[1]
    tm, tn = _tile(M, tm), _tile(D, tn)
    nd = D // tn
    y_spec = pl.BlockSpec((tm, C), lambda i, j: (i, 0))
    w_spec = pl.BlockSpec((C, tn), lambda i, j: (0, j))
    return pl.pallas_call(
        _merge_kernel,
        grid=(M // tm, nd),
        in_specs=[y_spec, y_spec, y_spec, w_spec, w_spec, w_spec,
                  pl.BlockSpec((tm, tn), lambda i, j: (i, j)),
                  pl.BlockSpec((tm, tn), lambda i, j: (i, j + nd)),
                  pl.BlockSpec((tm, tn), lambda i, j: (i, j + 2 * nd))],
        out_specs=pl.BlockSpec((tm, tn), lambda i, j: (i, j)),
        out_shape=jax.ShapeDtypeStruct((M, D), BF16),
        compiler_params=_params("parallel", "arbitrary"),
        name="gated_merge",
    )(ya, yb, yc, wa, wb, wc, gates, gates, gates)


def _layer(x, B, S, g_mix, w_in, b_forget, conv_w, conv_b, w_rg, b_rg, w_ig, b_ig, lam,
           w_pool, pool_scale, w_a, w_b, w_c, w_out, g_ffn, w_ffn_in, w_ffn_out):
    D = x.shape[1]
    H = N_HEADS
    d_rnn = conv_w.shape[1]
    d_attn = H * HEAD_DIM
    d_pool = pool_scale.shape[0]
    row = lambda v: v.reshape(1, -1)

    o_rnn = 2 * d_rnn
    o_qkv = o_rnn + 3 * d_attn
    o_f = o_qkv + H
    o_pool = o_f + d_pool
    w_in = w_in.astype(BF16)
    w_forget = jnp.pad(w_in[:, o_qkv:o_f], ((0, 0), (0, LANES - H)))
    g = row(g_mix)

    xy = _norm_matmul(x, g, w_in[:, :o_rnn], F32)
    qkv = _norm_matmul(x, g, w_in[:, o_rnn:o_qkv], BF16)
    f_logit = _norm_matmul(x, g, w_forget, F32)[:, :H]
    pool_in = _norm_matmul(x, g, w_in[:, o_f:o_pool], F32)
    gates = _norm_matmul(x, g, w_in[:, o_pool:], F32)

    y_a = _rglru(xy, conv_w, row(conv_b), w_rg.astype(BF16), row(b_rg), w_ig.astype(BF16),
                 row(b_ig), row(lam), B, S)

    f_seq = f_logit.reshape(B, S, H).transpose(0, 2, 1).reshape(B * H, S // LANES, LANES)
    bias = jnp.broadcast_to(b_forget[None, :, None, None], (B, H, 1, LANES)).reshape(B * H, 1, LANES)
    fcum = _forget_cumsum(f_seq, bias).reshape(B * H, S)
    y_b = _attention(qkv, fcum, B, S)

    y_c = _pool(pool_in, w_pool.astype(BF16), row(pool_scale), B, S)

    merged = _merge(y_a, y_b, y_c, w_a.astype(BF16), w_b.astype(BF16), w_c.astype(BF16), gates)
    x = _matmul_res(merged, w_out.astype(BF16), x)
    act = _norm_swiglu(x, row(g_ffn), w_ffn_in.astype(BF16))
    return _matmul_res(act, w_ffn_out.astype(BF16), x)


def kernel(x, g_mix, w_in, b_forget, conv_w, conv_b, w_rg, b_rg, w_ig, b_ig, lru_lambda, w_pool,
           pool_scale, w_branch_rnn, w_branch_attn, w_branch_pool, w_out, g_ffn, w_ffn_in,
           w_ffn_out, g_final):
    B, S, D = x.shape
    h = x.reshape(B * S, D)
    for l in range(g_mix.shape[0]):
        h = _layer(h, B, S, g_mix[l], w_in[l], b_forget[l], conv_w[l], conv_b[l], w_rg[l], b_rg[l],
                   w_ig[l], b_ig[l], lru_lambda[l], w_pool[l], pool_scale[l], w_branch_rnn[l],
                   w_branch_attn[l], w_branch_pool[l], w_out[l], g_ffn[l], w_ffn_in[l], w_ffn_out[l])
    return _rmsnorm(h, g_final.reshape(1, -1)).reshape(B, S, D)
```

```python
import functools

import jax
import jax.numpy as jnp
from jax import lax
from jax.experimental import pallas as pl
from jax.experimental.pallas import tpu as pltpu

F32 = jnp.float32
BF16 = jnp.bfloat16

N_RNN_BLOCKS = 8
CONV_WIDTH = 4
LRU_C = 8.0
N_HEADS = 8
HEAD_DIM = 128
POOL_WINDOWS = (2, 4, 8, 16)
POOL_HALO = 16
CONV_HALO = 8
NORM_EPS = 1e-6
LANES = 128

VMEM_LIMIT = 56 * 1024 * 1024


def _params(*sem):
    return pltpu.CompilerParams(dimension_semantics=sem, vmem_limit_bytes=VMEM_LIMIT)


def _tile(n, want):
    t = min(n, want)
    while n % t:
        t //= 2
    return t


def _sigmoid(x):
    return 1.0 / (1.0 + jnp.exp(-x))


def _log_sigmoid(x):
    return jnp.minimum(x, 0.0) - jnp.log1p(jnp.exp(-jnp.abs(x)))


def _normalize_rows(x, g):
    ms = jnp.mean(x * x, axis=-1, keepdims=True)
    return x * lax.rsqrt(ms + NORM_EPS) * g


def _in_proj_kernel(x_ref, g_ref, w_ref, wf_ref, c_ref, *rest, bounds):
    outs, f_ref, h_ref = rest[:-2], rest[-2], rest[-1]
    j = pl.program_id(1)

    @pl.when(j == 0)
    def _():
        h = _normalize_rows(x_ref[...], g_ref[...]).astype(BF16)
        h_ref[...] = h
        f_ref[...] = jnp.dot(h, wf_ref[...], preferred_element_type=F32)

    acc = jnp.dot(h_ref[...], w_ref[...], preferred_element_type=F32)
    for k, o_ref in enumerate(outs):
        @pl.when((j >= bounds[k]) & (j < bounds[k + 1]))
        def _(o_ref=o_ref):
            o_ref[...] = (acc * c_ref[...]).astype(o_ref.dtype)


def _in_proj(x, g, w, w_forget, colscale, widths, dtypes, tm=1024, tn=512):
    M, D = x.shape
    N = w.shape[1]
    tm = _tile(M, tm)
    assert all(wd % tn == 0 for wd in widths) and sum(widths) == N
    bounds = [0]
    for wd in widths:
        bounds.append(bounds[-1] + wd // tn)

    def out_map(k):
        lo, n = bounds[k], bounds[k + 1] - bounds[k]
        return lambda i, j: (i, jnp.clip(j - lo, 0, n - 1))

    out_specs = [pl.BlockSpec((tm, tn), out_map(k)) for k in range(len(widths))]
    out_shape = [jax.ShapeDtypeStruct((M, wd), dt) for wd, dt in zip(widths, dtypes)]
    out_specs.append(pl.BlockSpec((tm, LANES), lambda i, j: (i, 0)))
    out_shape.append(jax.ShapeDtypeStruct((M, LANES), F32))
    return pl.pallas_call(
        functools.partial(_in_proj_kernel, bounds=tuple(bounds)),
        grid=(M // tm, N // tn),
        in_specs=[pl.BlockSpec((tm, D), lambda i, j: (i, 0)),
                  pl.BlockSpec((1, D), lambda i, j: (0, 0)),
                  pl.BlockSpec((D, tn), lambda i, j: (0, j)),
                  pl.BlockSpec((D, LANES), lambda i, j: (0, 0)),
                  pl.BlockSpec((1, tn), lambda i, j: (0, j))],
        out_specs=out_specs,
        out_shape=out_shape,
        scratch_shapes=[pltpu.VMEM((tm, D), BF16)],
        compiler_params=_params("arbitrary", "arbitrary"),
        name="in_proj",
    )(x, g, w, w_forget, colscale)


def _norm_swiglu_kernel(x_ref, g_ref, wg_ref, wu_ref, o_ref, h_ref):
    @pl.when(pl.program_id(1) == 0)
    def _():
        h_ref[...] = _normalize_rows(x_ref[...], g_ref[...]).astype(BF16)

    h = h_ref[...]
    gate = jnp.dot(h, wg_ref[...], preferred_element_type=F32)
    up = jnp.dot(h, wu_ref[...], preferred_element_type=F32)
    o_ref[...] = (gate * _sigmoid(gate) * up).astype(o_ref.dtype)


def _norm_swiglu(x, g, w, tm=1024, tn=512):
    M, D = x.shape
    F = w.shape[1] // 2
    tm, tn = _tile(M, tm), _tile(F, tn)
    nf = F // tn
    return pl.pallas_call(
        _norm_swiglu_kernel,
        grid=(M // tm, nf),
        in_specs=[pl.BlockSpec((tm, D), lambda i, j: (i, 0)),
                  pl.BlockSpec((1, D), lambda i, j: (0, 0)),
                  pl.BlockSpec((D, tn), lambda i, j: (0, j)),
                  pl.BlockSpec((D, tn), lambda i, j: (0, j + nf))],
        out_specs=pl.BlockSpec((tm, tn), lambda i, j: (i, j)),
        out_shape=jax.ShapeDtypeStruct((M, F), BF16),
        scratch_shapes=[pltpu.VMEM((tm, D), BF16)],
        compiler_params=_params("parallel", "arbitrary"),
        name="norm_swiglu",
    )(x, g, w, w)


def _matmul_res_kernel(a_ref, w_ref, r_ref, o_ref):
    o_ref[...] = r_ref[...] + jnp.dot(a_ref[...], w_ref[...], preferred_element_type=F32)


def _matmul_res(a, w, res, tm=512, tn=512):
    M, K = a.shape
    N = w.shape[1]
    tm, tn = _tile(M, tm), _tile(N, tn)
    return pl.pallas_call(
        _matmul_res_kernel,
        grid=(M // tm, N // tn),
        in_specs=[pl.BlockSpec((tm, K), lambda i, j: (i, 0)),
                  pl.BlockSpec((K, tn), lambda i, j: (0, j)),
                  pl.BlockSpec((tm, tn), lambda i, j: (i, j))],
        out_specs=pl.BlockSpec((tm, tn), lambda i, j: (i, j)),
        out_shape=jax.ShapeDtypeStruct((M, N), F32),
        compiler_params=_params("parallel", "arbitrary"),
        name="matmul_res",
    )(a, w, res)


def _rmsnorm_kernel(x_ref, g_ref, o_ref):
    o_ref[...] = _normalize_rows(x_ref[...], g_ref[...])


def _rmsnorm(x, g, tm=512):
    M, D = x.shape
    tm = _tile(M, tm)
    return pl.pallas_call(
        _rmsnorm_kernel,
        grid=(M // tm,),
        in_specs=[pl.BlockSpec((tm, D), lambda i: (i, 0)),
                  pl.BlockSpec((1, D), lambda i: (0, 0))],
        out_specs=pl.BlockSpec((tm, D), lambda i: (i, 0)),
        out_shape=jax.ShapeDtypeStruct((M, D), F32),
        compiler_params=_params("parallel"),
        name="rmsnorm",
    )(x, g)


N_BIAS_COLS = 3
LOG2E = 1.4426950408889634


def _forget_bias_kernel(f_ref, b_ref, tri_ref, sel_ref, o_ref, carry_ref):
    @pl.when(pl.program_id(1) == 0)
    def _():
        carry_ref[...] = jnp.zeros_like(carry_ref)

    x = _log_sigmoid(f_ref[...] + b_ref[...])
    cum = jnp.dot(tri_ref[...], x, preferred_element_type=F32,
                  precision=lax.Precision.HIGHEST) + carry_ref[0:1, :]
    T = x.shape[0]
    carry_ref[0:1, :] = cum[T - 1:T, :]
    f2 = cum * (-LOG2E)
    hi = f2.astype(BF16)
    rest = f2 - hi.astype(F32)
    mid = rest.astype(BF16)
    lo = (rest - mid.astype(F32)).astype(BF16)
    parts = jnp.concatenate([hi, mid, lo], axis=1)
    o_ref[...] = jnp.dot(parts, sel_ref[...], preferred_element_type=F32).astype(o_ref.dtype)


def _forget_bias(f, b, B, S, T=512):
    H = N_HEADS
    T = _tile(S, T)
    nt = S // T
    tri = jnp.tril(jnp.ones((T, T), F32))
    r = jnp.arange(N_BIAS_COLS * LANES)[:, None]
    c = jnp.arange(H * LANES)[None, :]
    sel = ((r // LANES == c % LANES) & (r % LANES == c // LANES)).astype(BF16)
    return pl.pallas_call(
        _forget_bias_kernel,
        grid=(B, nt),
        in_specs=[pl.BlockSpec((T, LANES), lambda b_, t: (b_ * nt + t, 0)),
                  pl.BlockSpec((1, LANES), lambda b_, t: (0, 0)),
                  pl.BlockSpec((T, T), lambda b_, t: (0, 0)),
                  pl.BlockSpec(sel.shape, lambda b_, t: (0, 0))],
        out_specs=pl.BlockSpec((T, H * LANES), lambda b_, t: (b_ * nt + t, 0)),
        out_shape=jax.ShapeDtypeStruct((B * S, H * LANES), BF16),
        scratch_shapes=[pltpu.VMEM((8, LANES), F32)],
        compiler_params=_params("arbitrary", "arbitrary"),
        name="forget_bias",
    )(f, b, tri, sel)


def _attn_kernel(q_ref, k_ref, v_ref, kb_ref, o_ref, vt_ref, acc_ref, m_ref, l_ref, *,
                 tq, tk, nh, unroll, ahead):
    i = pl.program_id(2)
    S = k_ref.shape[0]
    per_q = tq // tk
    tile = nh * tq

    @pl.when(i == 0)
    def _():
        for c in range(S // tile):
            vt_ref[:, c * tile:(c + 1) * tile] = v_ref[c * tile:(c + 1) * tile, :].T

    lane = lax.broadcasted_iota(jnp.int32, (tq, HEAD_DIM), 1)
    ones_cols = jnp.where(lane < N_BIAS_COLS, 1.0, 0.0).astype(BF16)
    q_aug = [jnp.concatenate([q_ref[h * tq:(h + 1) * tq, :], ones_cols], axis=1) for h in range(nh)]

    m_ref[...] = jnp.full_like(m_ref, -jnp.inf)
    l_ref[...] = jnp.zeros_like(l_ref)
    acc_ref[...] = jnp.zeros_like(acc_ref)

    def scores(j, h):
        start = pl.multiple_of(j * tk, tk)
        k_aug = jnp.concatenate([k_ref[pl.ds(start, tk), :], kb_ref[pl.ds(start, tk), :]], axis=1)
        return lax.dot_general(k_aug, q_aug[h], (((1,), (1,)), ((), ())),
                               preferred_element_type=F32)

    def step(j, h, diag=None, s=None):
        cols = slice(h * tq, (h + 1) * tq)
        start = pl.multiple_of(j * tk, tk)
        if s is None:
            s = scores(j, h)
        if diag is not None:
            key = lax.broadcasted_iota(jnp.int32, (tk, tq), 0) + diag * tk
            qry = lax.broadcasted_iota(jnp.int32, (tk, tq), 1)
            s = jnp.where(key <= qry, s, -jnp.inf)
        m_old = m_ref[0:1, cols]
        m_new = jnp.maximum(m_old, jnp.max(s, axis=0, keepdims=True))
        alpha = jnp.exp2(m_old - m_new)
        p = jnp.exp2(s - m_new)
        l_ref[0:1, cols] = alpha * l_ref[0:1, cols] + jnp.sum(p, axis=0, keepdims=True)
        m_ref[0:1, cols] = m_new
        pv = jnp.dot(vt_ref[:, pl.ds(start, tk)], p.astype(BF16), preferred_element_type=F32)
        acc_ref[:, cols] = alpha * acc_ref[:, cols] + pv

    n_full = i * nh * per_q

    def run(subs):
        pending = {t: scores(j, h) for t, (j, h, _) in enumerate(subs[:ahead])}
        for t, (j, h, diag) in enumerate(subs):
            if t + ahead < len(subs):
                jn, hn, _ = subs[t + ahead]
                pending[t + ahead] = scores(jn, hn)
            step(j, h, diag=diag, s=pending.pop(t))

    def blocks(base, count):
        return [(base + u, h, None) for u in range(count) for h in range(nh)]

    n_groups = n_full // unroll

    def group(g, carry):
        run(blocks(g * unroll, unroll))
        return carry

    lax.fori_loop(0, n_groups, group, 0)

    rem = n_full - n_groups * unroll
    p = unroll // 2
    while p >= nh * per_q:
        @pl.when((rem & p) != 0)
        def _(p=p):
            run(blocks(n_groups * unroll + (rem // (2 * p)) * (2 * p), p))
        p //= 2

    diag_subs = []
    for d in range(nh * per_q):
        for h in range(nh):
            if d < h * per_q:
                diag_subs.append((n_full + d, h, None))
            elif d < (h + 1) * per_q:
                diag_subs.append((n_full + d, h, d - h * per_q))
    run(diag_subs)
    o_ref[...] = (acc_ref[...] / l_ref[0:1, :]).T.astype(o_ref.dtype)


def _attention(qkv, kbias, B, S, tq=512, tk=256, nh=1, unroll=8, ahead=2):
    H, Dh = N_HEADS, HEAD_DIM
    tq = _tile(S, tq)
    tk = _tile(tq, tk)
    nh = _tile(S // tq, nh)
    tile = nh * tq
    nq = S // tile
    return pl.pallas_call(
        functools.partial(_attn_kernel, tq=tq, tk=tk, nh=nh, unroll=unroll, ahead=ahead),
        grid=(B, H, nq),
        in_specs=[pl.BlockSpec((tile, Dh), lambda b, h, i: (b * nq + i, h)),
                  pl.BlockSpec((S, Dh), lambda b, h, i: (b, H + h)),
                  pl.BlockSpec((S, Dh), lambda b, h, i: (b, 2 * H + h)),
                  pl.BlockSpec((S, LANES), lambda b, h, i: (b, h))],
        out_specs=pl.BlockSpec((tile, Dh), lambda b, h, i: (b * nq + i, h)),
        out_shape=jax.ShapeDtypeStruct((B * S, H * Dh), BF16),
        scratch_shapes=[pltpu.VMEM((Dh, S), BF16), pltpu.VMEM((Dh, tile), F32),
                        pltpu.VMEM((8, tile), F32), pltpu.VMEM((8, tile), F32)],
        compiler_params=_params("arbitrary", "arbitrary", "arbitrary"),
        name="fox_attention",
    )(qkv, qkv, qkv, kbias)


def _gelu_tanh(x):
    c = 0.7978845608028654
    return 0.5 * x * (1.0 + jnp.tanh(c * (x + 0.044715 * (x * x * x))))


def _rglru_kernel(x_ref, y_ref, cw_ref, cb_ref, wr_ref, br_ref, wi_ref, bi_ref, lam_ref,
                  o_ref, ext_ref, a_ref, b_ref, h_ref, state_ref, *, T):
    C = x_ref.shape[1]
    cb = C // N_RNN_BLOCKS

    @pl.when(pl.program_id(1) == 0)
    def _():
        ext_ref[0:CONV_HALO, :] = jnp.zeros((CONV_HALO, C), F32)
        state_ref[...] = jnp.zeros_like(state_ref)

    x = x_ref[...]
    ext_ref[CONV_HALO:CONV_HALO + T, :] = x
    u = cb_ref[...] + x * cw_ref[CONV_WIDTH - 1:CONV_WIDTH, :]
    for k in range(CONV_WIDTH - 1):
        off = CONV_HALO + k - (CONV_WIDTH - 1)
        u = u + ext_ref[off:off + T, :] * cw_ref[k:k + 1, :]
    ext_ref[0:CONV_HALO, :] = x[T - CONV_HALO:T, :]

    ub = u.astype(BF16)
    r_parts, i_parts = [], []
    for n in range(N_RNN_BLOCKS):
        un = ub[:, n * cb:(n + 1) * cb]
        r_parts.append(jnp.dot(un, wr_ref[n], preferred_element_type=F32))
        i_parts.append(jnp.dot(un, wi_ref[n], preferred_element_type=F32))
    r = _sigmoid(jnp.concatenate(r_parts, axis=1) + br_ref[...])
    ig = _sigmoid(jnp.concatenate(i_parts, axis=1) + bi_ref[...])
    log_a = (LRU_C * r) * _log_sigmoid(lam_ref[...])
    a = jnp.exp(log_a)
    a_ref[...] = a
    b_ref[...] = jnp.sqrt(-jnp.tanh(log_a) * (a * a + 1.0)) * (ig * u)

    def group(g, h):
        base = pl.multiple_of(g * 8, 8)
        a8 = a_ref[pl.ds(base, 8), :]
        b8 = b_ref[pl.ds(base, 8), :]
        rows = []
        for s in range(8):
            h = a8[s:s + 1, :] * h + b8[s:s + 1, :]
            rows.append(h)
        h_ref[pl.ds(base, 8), :] = jnp.concatenate(rows, axis=0)
        return h

    h_last = lax.fori_loop(0, T // 8, group, state_ref[0:1, :])
    state_ref[0:1, :] = h_last
    o_ref[...] = (_gelu_tanh(y_ref[...]) * h_ref[...]).astype(o_ref.dtype)


def _rglru(xy, conv_w, conv_b, w_rg, b_rg, w_ig, b_ig, lam, B, S, T=512):
    C = xy.shape[1] // 2
    T = _tile(S, T)
    nt = S // T
    vec = pl.BlockSpec((1, C), lambda b, t: (0, 0))
    blockdiag = pl.BlockSpec(w_rg.shape, lambda b, t: (0, 0, 0))
    return pl.pallas_call(
        functools.partial(_rglru_kernel, T=T),
        grid=(B, nt),
        in_specs=[pl.BlockSpec((T, C), lambda b, t: (b * nt + t, 0)),
                  pl.BlockSpec((T, C), lambda b, t: (b * nt + t, 1)),
                  pl.BlockSpec((CONV_WIDTH, C), lambda b, t: (0, 0)),
                  vec, blockdiag, vec, blockdiag, vec, vec],
        out_specs=pl.BlockSpec((T, C), lambda b, t: (b * nt + t, 0)),
        out_shape=jax.ShapeDtypeStruct((B * S, C), BF16),
        scratch_shapes=[pltpu.VMEM((T + CONV_HALO, C), F32),
                        pltpu.VMEM((T, C), F32), pltpu.VMEM((T, C), F32), pltpu.VMEM((T, C), F32),
                        pltpu.VMEM((8, C), F32)],
        compiler_params=_params("arbitrary", "arbitrary"),
        name="rglru",
    )(xy, xy, conv_w, conv_b, w_rg, b_rg, w_ig, b_ig, lam)


def _pool_kernel(u_ref, w_ref, s_ref, o_ref, ext_ref, *, T):
    C = u_ref.shape[1]
    G = len(POOL_WINDOWS)
    cg = C // G
    t = pl.program_id(1)

    @pl.when(t == 0)
    def _():
        ext_ref[0:POOL_HALO, :] = jnp.zeros((POOL_HALO, C), F32)

    x = u_ref[...]
    ext_ref[POOL_HALO:POOL_HALO + T, :] = x
    pos = (t * T + 1 + lax.broadcasted_iota(jnp.int32, (T, 1), 0)).astype(F32)
    for g, w in enumerate(POOL_WINDOWS):
        lo = g * cg
        xs = x[:, lo:lo + cg]
        acc = xs
        for k in range(1, w):
            acc = acc + ext_ref[POOL_HALO - k:POOL_HALO - k + T, lo:lo + cg]
        pooled = acc / jnp.minimum(pos, float(w)) - xs
        y = jnp.dot(pooled.astype(BF16), w_ref[g], preferred_element_type=F32)
        o_ref[:, lo:lo + cg] = (y * s_ref[:, lo:lo + cg]).astype(o_ref.dtype)
    ext_ref[0:POOL_HALO, :] = x[T - POOL_HALO:T, :]


def _pool(u, w_pool, scale, B, S, T=512):
    C = u.shape[1]
    T = _tile(S, T)
    nt = S // T
    return pl.pallas_call(
        functools.partial(_pool_kernel, T=T),
        grid=(B, nt),
        in_specs=[pl.BlockSpec((T, C), lambda b, t: (b * nt + t, 0)),
                  pl.BlockSpec(w_pool.shape, lambda b, t: (0, 0, 0)),
                  pl.BlockSpec((1, C), lambda b, t: (0, 0))],
        out_specs=pl.BlockSpec((T, C), lambda b, t: (b * nt + t, 0)),
        out_shape=jax.ShapeDtypeStruct((B * S, C), BF16),
        scratch_shapes=[pltpu.VMEM((T + POOL_HALO, C), F32)],
        compiler_params=_params("arbitrary", "arbitrary"),
        name="multiscale_pool",
    )(u, w_pool, scale)


def _merge_kernel(ya_ref, yb_ref, yc_ref, wa_ref, wb_ref, wc_ref, ga_ref, gb_ref, gc_ref, o_ref):
    out = _sigmoid(ga_ref[...]) * jnp.dot(ya_ref[...], wa_ref[...], preferred_element_type=F32)
    out = out + _sigmoid(gb_ref[...]) * jnp.dot(yb_ref[...], wb_ref[...], preferred_element_type=F32)
    out = out + _sigmoid(gc_ref[...]) * jnp.dot(yc_ref[...], wc_ref[...], preferred_element_type=F32)
    o_ref[...] = out.astype(o_ref.dtype)


def _merge(ya, yb, yc, wa, wb, wc, gates, tm=1024, tn=512):
    M, C = ya.shape
    D = wa.shape[1]
    tm, tn = _tile(M, tm), _tile(D, tn)
    nd = D // tn
    y_spec = pl.BlockSpec((tm, C), lambda i, j: (i, 0))
    w_spec = pl.BlockSpec((C, tn), lambda i, j: (0, j))
    return pl.pallas_call(
        _merge_kernel,
        grid=(M // tm, nd),
        in_specs=[y_spec, y_spec, y_spec, w_spec, w_spec, w_spec,
                  pl.BlockSpec((tm, tn), lambda i, j: (i, j)),
                  pl.BlockSpec((tm, tn), lambda i, j: (i, j + nd)),
                  pl.BlockSpec((tm, tn), lambda i, j: (i, j + 2 * nd))],
        out_specs=pl.BlockSpec((tm, tn), lambda i, j: (i, j)),
        out_shape=jax.ShapeDtypeStruct((M, D), BF16),
        compiler_params=_params("parallel", "arbitrary"),
        name="gated_merge",
    )(ya, yb, yc, wa, wb, wc, gates, gates, gates)


def _layer(x, B, S, g_mix, w_in, b_forget, conv_w, conv_b, w_rg, b_rg, w_ig, b_ig, lam,
           w_pool, pool_scale, w_a, w_b, w_c, w_out, g_ffn, w_ffn_in, w_ffn_out):
    D = x.shape[1]
    H = N_HEADS
    d_rnn = conv_w.shape[1]
    d_attn = H * HEAD_DIM
    d_pool = pool_scale.shape[0]
    row = lambda v: v.reshape(1, -1)

    o_rnn = 2 * d_rnn
    o_qkv = o_rnn + 3 * d_attn
    o_f = o_qkv + H
    o_pool = o_f + d_pool
    w_main = jnp.concatenate([w_in[:, :o_qkv], w_in[:, o_f:]], axis=1).astype(BF16)
    w_forget = jnp.pad(w_in[:, o_qkv:o_f], ((0, 0), (0, LANES - H))).astype(BF16)
    widths = (o_rnn, 3 * d_attn, d_pool, 3 * D)
    colscale = jnp.ones((1, sum(widths)), F32).at[:, o_rnn:o_rnn + d_attn].set(HEAD_DIM ** -0.5 * LOG2E)
    xy, qkv, pool_in, gates, f_logit = _in_proj(x, row(g_mix), w_main, w_forget, colscale, widths,
                                                (F32, BF16, F32, F32))

    y_a = _rglru(xy, conv_w, row(conv_b), w_rg.astype(BF16), row(b_rg), w_ig.astype(BF16),
                 row(b_ig), row(lam), B, S)

    kbias = _forget_bias(f_logit, jnp.pad(b_forget, (0, LANES - H)).reshape(1, LANES), B, S)
    y_b = _attention(qkv, kbias, B, S)

    y_c = _pool(pool_in, w_pool.astype(BF16), row(pool_scale), B, S)

    merged = _merge(y_a, y_b, y_c, w_a.astype(BF16), w_b.astype(BF16), w_c.astype(BF16), gates)
    x = _matmul_res(merged, w_out.astype(BF16), x, tm=512, tn=D)
    act = _norm_swiglu(x, row(g_ffn), w_ffn_in.astype(BF16))
    return _matmul_res(act, w_ffn_out.astype(BF16), x, tm=1024, tn=512)


def kernel(x, g_mix, w_in, b_forget, conv_w, conv_b, w_rg, b_rg, w_ig, b_ig, lru_lambda, w_pool,
           pool_scale, w_branch_rnn, w_branch_attn, w_branch_pool, w_out, g_ffn, w_ffn_in,
           w_ffn_out, g_final):
    B, S, D = x.shape
    h = x.reshape(B * S, D)
    for l in range(g_mix.shape[0]):
        h = _layer(h, B, S, g_mix[l], w_in[l], b_forget[l], conv_w[l], conv_b[l], w_rg[l], b_rg[l],
                   w_ig[l], b_ig[l], lru_lambda[l], w_pool[l], pool_scale[l], w_branch_rnn[l],
                   w_branch_attn[l], w_branch_pool[l], w_out[l], g_ffn[l], w_ffn_in[l], w_ffn_out[l])
    return _rmsnorm(h, g_final.reshape(1, -1)).reshape(B, S, D)
```

```python
import functools

import jax
import jax.numpy as jnp
from jax import lax
from jax.experimental import pallas as pl
from jax.experimental.pallas import tpu as pltpu

F32 = jnp.float32
BF16 = jnp.bfloat16

N_RNN_BLOCKS = 8
CONV_WIDTH = 4
LRU_C = 8.0
N_HEADS = 8
HEAD_DIM = 128
POOL_WINDOWS = (2, 4, 8, 16)
POOL_HALO = 16
CONV_HALO = 8
NORM_EPS = 1e-6
LANES = 128

VMEM_LIMIT = 56 * 1024 * 1024


def _params(*sem):
    return pltpu.CompilerParams(dimension_semantics=sem, vmem_limit_bytes=VMEM_LIMIT)


def _tile(n, want):
    t = min(n, want)
    while n % t:
        t //= 2
    return t


def _sigmoid(x):
    return 0.5 * jnp.tanh(0.5 * x) + 0.5


def _log_sigmoid(x):
    return jnp.minimum(x, 0.0) - jnp.log1p(jnp.exp(-jnp.abs(x)))


def _normalize_rows(x, g):
    ms = jnp.mean(x * x, axis=-1, keepdims=True)
    return x * lax.rsqrt(ms + NORM_EPS) * g


def _norm_matmul_kernel(x_ref, g_ref, w_ref, c_ref, o_ref, h_ref):
    @pl.when(pl.program_id(1) == 0)
    def _():
        h_ref[...] = _normalize_rows(x_ref[...], g_ref[...]).astype(BF16)

    acc = jnp.dot(h_ref[...], w_ref[...], preferred_element_type=F32)
    o_ref[...] = (acc * c_ref[...]).astype(o_ref.dtype)


def _norm_matmul(x, g, w, colscale, out_dtype, tm=1024, tn=512):
    M, D = x.shape
    N = w.shape[1]
    tm, tn = _tile(M, tm), _tile(N, tn)
    return pl.pallas_call(
        _norm_matmul_kernel,
        grid=(M // tm, N // tn),
        in_specs=[pl.BlockSpec((tm, D), lambda i, j: (i, 0)),
                  pl.BlockSpec((1, D), lambda i, j: (0, 0)),
                  pl.BlockSpec((D, tn), lambda i, j: (0, j)),
                  pl.BlockSpec((1, tn), lambda i, j: (0, j))],
        out_specs=pl.BlockSpec((tm, tn), lambda i, j: (i, j)),
        out_shape=jax.ShapeDtypeStruct((M, N), out_dtype),
        scratch_shapes=[pltpu.VMEM((tm, D), BF16)],
        compiler_params=_params("arbitrary", "arbitrary"),
        name="norm_matmul",
    )(x, g, w, colscale)


def _in_proj_kernel(x_ref, g_ref, w_ref, wf_ref, *rest, bounds):
    outs, f_ref, h_ref = rest[:-2], rest[-2], rest[-1]
    j = pl.program_id(1)

    @pl.when(j == 0)
    def _():
        h = _normalize_rows(x_ref[...], g_ref[...]).astype(BF16)
        h_ref[...] = h
        f_ref[...] = jnp.dot(h, wf_ref[...], preferred_element_type=F32)

    acc = jnp.dot(h_ref[...], w_ref[...], preferred_element_type=F32)
    tm, tn = acc.shape
    rows = min(tm, tn)
    for k, o_ref in enumerate(outs):
        owns = jnp.broadcast_to((j >= bounds[k]) & (j < bounds[k + 1]), (rows, tn))
        for r in range(0, tm, rows):
            pltpu.store(o_ref.at[r:r + rows, :], acc[r:r + rows, :], mask=owns)


def _in_proj(x, g, w, w_forget, widths, tm=1024, tn=512):
    M, D = x.shape
    N = w.shape[1]
    tm = _tile(M, tm)
    assert all(wd % tn == 0 for wd in widths) and sum(widths) == N
    bounds = [0]
    for wd in widths:
        bounds.append(bounds[-1] + wd // tn)

    def out_map(k):
        lo, n = bounds[k], bounds[k + 1] - bounds[k]
        return lambda i, j: (i, jnp.clip(j - lo, 0, n - 1))

    out_specs = [pl.BlockSpec((tm, tn), out_map(k)) for k in range(len(widths))]
    out_shape = [jax.ShapeDtypeStruct((M, wd), F32) for wd in widths]
    out_specs.append(pl.BlockSpec((tm, LANES), lambda i, j: (i, 0)))
    out_shape.append(jax.ShapeDtypeStruct((M, LANES), F32))
    return pl.pallas_call(
        functools.partial(_in_proj_kernel, bounds=tuple(bounds)),
        grid=(M // tm, N // tn),
        in_specs=[pl.BlockSpec((tm, D), lambda i, j: (i, 0)),
                  pl.BlockSpec((1, D), lambda i, j: (0, 0)),
                  pl.BlockSpec((D, tn), lambda i, j: (0, j)),
                  pl.BlockSpec((D, LANES), lambda i, j: (0, 0))],
        out_specs=out_specs,
        out_shape=out_shape,
        scratch_shapes=[pltpu.VMEM((tm, D), BF16)],
        compiler_params=_params("arbitrary", "arbitrary"),
        name="in_proj",
    )(x, g, w, w_forget)


def _norm_swiglu_kernel(x_ref, g_ref, wg_ref, wu_ref, o_ref, h_ref):
    @pl.when(pl.program_id(1) == 0)
    def _():
        h_ref[...] = _normalize_rows(x_ref[...], g_ref[...]).astype(BF16)

    h = h_ref[...]
    gate = jnp.dot(h, wg_ref[...], preferred_element_type=F32)
    up = jnp.dot(h, wu_ref[...], preferred_element_type=F32)
    o_ref[...] = (gate * _sigmoid(gate) * up).astype(o_ref.dtype)


def _norm_swiglu(x, g, w, tm=1024, tn=512):
    M, D = x.shape
    F = w.shape[1] // 2
    tm, tn = _tile(M, tm), _tile(F, tn)
    nf = F // tn
    return pl.pallas_call(
        _norm_swiglu_kernel,
        grid=(M // tm, nf),
        in_specs=[pl.BlockSpec((tm, D), lambda i, j: (i, 0)),
                  pl.BlockSpec((1, D), lambda i, j: (0, 0)),
                  pl.BlockSpec((D, tn), lambda i, j: (0, j)),
                  pl.BlockSpec((D, tn), lambda i, j: (0, j + nf))],
        out_specs=pl.BlockSpec((tm, tn), lambda i, j: (i, j)),
        out_shape=jax.ShapeDtypeStruct((M, F), BF16),
        scratch_shapes=[pltpu.VMEM((tm, D), BF16)],
        compiler_params=_params("parallel", "arbitrary"),
        name="norm_swiglu",
    )(x, g, w, w)


def _matmul_res_kernel(a_ref, w_ref, r_ref, o_ref):
    o_ref[...] = r_ref[...] + jnp.dot(a_ref[...], w_ref[...], preferred_element_type=F32)


def _matmul_res(a, w, res, tm=512, tn=512):
    M, K = a.shape
    N = w.shape[1]
    tm, tn = _tile(M, tm), _tile(N, tn)
    return pl.pallas_call(
        _matmul_res_kernel,
        grid=(M // tm, N // tn),
        in_specs=[pl.BlockSpec((tm, K), lambda i, j: (i, 0)),
                  pl.BlockSpec((K, tn), lambda i, j: (0, j)),
                  pl.BlockSpec((tm, tn), lambda i, j: (i, j))],
        out_specs=pl.BlockSpec((tm, tn), lambda i, j: (i, j)),
        out_shape=jax.ShapeDtypeStruct((M, N), F32),
        compiler_params=_params("parallel", "arbitrary"),
        name="matmul_res",
    )(a, w, res)


def _rmsnorm_kernel(x_ref, g_ref, o_ref):
    o_ref[...] = _normalize_rows(x_ref[...], g_ref[...])


def _rmsnorm(x, g, tm=512):
    M, D = x.shape
    tm = _tile(M, tm)
    return pl.pallas_call(
        _rmsnorm_kernel,
        grid=(M // tm,),
        in_specs=[pl.BlockSpec((tm, D), lambda i: (i, 0)),
                  pl.BlockSpec((1, D), lambda i: (0, 0))],
        out_specs=pl.BlockSpec((tm, D), lambda i: (i, 0)),
        out_shape=jax.ShapeDtypeStruct((M, D), F32),
        compiler_params=_params("parallel"),
        name="rmsnorm",
    )(x, g)


N_BIAS_COLS = 3
LOG2E = 1.4426950408889634


def _forget_bias_kernel(f_ref, b_ref, tri_ref, sel_ref, o_ref, carry_ref):
    @pl.when(pl.program_id(1) == 0)
    def _():
        carry_ref[...] = jnp.zeros_like(carry_ref)

    x = _log_sigmoid(f_ref[...] + b_ref[...])
    cum = jnp.dot(tri_ref[...], x, preferred_element_type=F32,
                  precision=lax.Precision.HIGHEST) + carry_ref[0:1, :]
    T = x.shape[0]
    carry_ref[0:1, :] = cum[T - 1:T, :]
    f2 = cum * (-LOG2E)
    hi = f2.astype(BF16)
    rest = f2 - hi.astype(F32)
    mid = rest.astype(BF16)
    lo = (rest - mid.astype(F32)).astype(BF16)
    parts = jnp.concatenate([hi, mid, lo], axis=1)
    o_ref[...] = jnp.dot(parts, sel_ref[...], preferred_element_type=F32).astype(o_ref.dtype)


def _forget_bias(f, b, B, S, T=512):
    H = N_HEADS
    T = _tile(S, T)
    nt = S // T
    tri = jnp.tril(jnp.ones((T, T), F32))
    r = jnp.arange(N_BIAS_COLS * LANES)[:, None]
    c = jnp.arange(H * LANES)[None, :]
    sel = ((r // LANES == c % LANES) & (r % LANES == c // LANES)).astype(BF16)
    return pl.pallas_call(
        _forget_bias_kernel,
        grid=(B, nt),
        in_specs=[pl.BlockSpec((T, LANES), lambda b_, t: (b_ * nt + t, 0)),
                  pl.BlockSpec((1, LANES), lambda b_, t: (0, 0)),
                  pl.BlockSpec((T, T), lambda b_, t: (0, 0)),
                  pl.BlockSpec(sel.shape, lambda b_, t: (0, 0))],
        out_specs=pl.BlockSpec((T, H * LANES), lambda b_, t: (b_ * nt + t, 0)),
        out_shape=jax.ShapeDtypeStruct((B * S, H * LANES), BF16),
        scratch_shapes=[pltpu.VMEM((8, LANES), F32)],
        compiler_params=_params("arbitrary", "arbitrary"),
        name="forget_bias",
    )(f, b, tri, sel)


SUM_ROWS = 16


def _attn_kernel(q_ref, k_ref, v_ref, kb_ref, o_ref, vt_ref, qt_ref, acc_ref, m_ref, *,
                 tq, tk, unroll, ahead):
    i = pl.program_id(2)
    S = k_ref.shape[0]
    Dh = HEAD_DIM

    @pl.when(i == 0)
    def _():
        for c in range(S // tq):
            vt_ref[0:Dh, c * tq:(c + 1) * tq] = v_ref[c * tq:(c + 1) * tq, :].T
        vt_ref[Dh:Dh + SUM_ROWS, :] = jnp.ones((SUM_ROWS, S), BF16)

    qt_ref[0:Dh, :] = q_ref[...].T
    row = lax.broadcasted_iota(jnp.int32, (Dh, tq), 0)
    qt_ref[Dh:2 * Dh, :] = jnp.where(row < N_BIAS_COLS, 1.0, 0.0).astype(BF16)

    m_ref[...] = jnp.full_like(m_ref, -jnp.inf)
    acc_ref[...] = jnp.zeros_like(acc_ref)

    def scores(j):
        start = pl.multiple_of(j * tk, tk)
        k_aug = jnp.concatenate([k_ref[pl.ds(start, tk), :], kb_ref[pl.ds(start, tk), :]], axis=1)
        return jnp.dot(k_aug, qt_ref[...], preferred_element_type=F32)

    def run(base, count, diag=False):
        assert ahead >= 2
        pending = {c: scores(base + c) for c in range(min(ahead, count))}

        def max_pass(c, m_in):
            s = pending.pop(c)
            if diag:
                key = lax.broadcasted_iota(jnp.int32, (tk, tq), 0) + c * tk
                qry = lax.broadcasted_iota(jnp.int32, (tk, tq), 1)
                s = jnp.where(key <= qry, s, -jnp.inf)
            return s, m_in, jnp.maximum(m_in, jnp.max(s, axis=0, keepdims=True))

        cur = max_pass(0, m_ref[0:1, :])
        for c in range(count):
            if c + ahead < count:
                pending[c + ahead] = scores(base + c + ahead)
            nxt = max_pass(c + 1, cur[2]) if c + 1 < count else None
            s, m_old, m_new = cur
            p = jnp.exp2(s - m_new).astype(BF16)
            start = pl.multiple_of((base + c) * tk, tk)
            pv = jnp.dot(vt_ref[:, pl.ds(start, tk)], p, preferred_element_type=F32)
            acc_ref[...] = jnp.exp2(m_old - m_new) * acc_ref[...] + pv
            cur = nxt
        m_ref[0:1, :] = m_new

    per_q = tq // tk
    n_full = i * per_q
    n_groups = n_full // unroll

    def group(g, carry):
        run(g * unroll, unroll)
        return carry

    lax.fori_loop(0, n_groups, group, 0)

    rem = n_full - n_groups * unroll
    size = unroll // 2
    while size >= per_q:
        @pl.when((rem & size) != 0)
        def _(size=size):
            run(n_groups * unroll + (rem // (2 * size)) * (2 * size), size)
        size //= 2

    run(n_full, per_q, diag=True)
    acc = acc_ref[...]
    o_ref[...] = (acc[0:Dh, :] / acc[Dh:Dh + 1, :]).T.astype(o_ref.dtype)


def _attention(qkv, kbias, B, S, tq=512, tk=256, unroll=16, ahead=2):
    H, Dh = N_HEADS, HEAD_DIM
    tq = _tile(S, tq)
    tk = _tile(tq, tk)
    unroll = max(unroll, tq // tk)
    nq = S // tq
    return pl.pallas_call(
        functools.partial(_attn_kernel, tq=tq, tk=tk, unroll=unroll, ahead=ahead),
        grid=(B, H, nq),
        in_specs=[pl.BlockSpec((tq, Dh), lambda b, h, i: (b * nq + i, h)),
                  pl.BlockSpec((S, Dh), lambda b, h, i: (b, H + h)),
                  pl.BlockSpec((S, Dh), lambda b, h, i: (b, 2 * H + h)),
                  pl.BlockSpec((S, LANES), lambda b, h, i: (b, h))],
        out_specs=pl.BlockSpec((tq, Dh), lambda b, h, i: (b * nq + i, h)),
        out_shape=jax.ShapeDtypeStruct((B * S, H * Dh), BF16),
        scratch_shapes=[pltpu.VMEM((Dh + SUM_ROWS, S), BF16), pltpu.VMEM((2 * Dh, tq), BF16),
                        pltpu.VMEM((Dh + SUM_ROWS, tq), F32), pltpu.VMEM((8, tq), F32)],
        compiler_params=_params("arbitrary", "arbitrary", "arbitrary"),
        name="fox_attention",
    )(qkv, qkv, qkv, kbias)


def _gelu_tanh(x):
    c = 0.7978845608028654
    return 0.5 * x * (1.0 + jnp.tanh(c * (x + 0.044715 * (x * x * x))))


def _rglru_kernel(x_ref, y_ref, cw_ref, cb_ref, wr_ref, br_ref, wi_ref, bi_ref, lam_ref,
                  o_ref, ext_ref, a_ref, b_ref, h_ref, state_ref, *, T):
    C = x_ref.shape[1]
    cb = C // N_RNN_BLOCKS

    @pl.when(pl.program_id(1) == 0)
    def _():
        ext_ref[0:CONV_HALO, :] = jnp.zeros((CONV_HALO, C), F32)
        state_ref[...] = jnp.zeros_like(state_ref)

    x = x_ref[...]
    ext_ref[CONV_HALO:CONV_HALO + T, :] = x
    u = cb_ref[...] + x * cw_ref[CONV_WIDTH - 1:CONV_WIDTH, :]
    for k in range(CONV_WIDTH - 1):
        off = CONV_HALO + k - (CONV_WIDTH - 1)
        u = u + ext_ref[off:off + T, :] * cw_ref[k:k + 1, :]
    ext_ref[0:CONV_HALO, :] = x[T - CONV_HALO:T, :]

    ub = u.astype(BF16)
    r_parts, i_parts = [], []
    for n in range(N_RNN_BLOCKS):
        un = ub[:, n * cb:(n + 1) * cb]
        r_parts.append(jnp.dot(un, wr_ref[n], preferred_element_type=F32))
        i_parts.append(jnp.dot(un, wi_ref[n], preferred_element_type=F32))
    r = _sigmoid(jnp.concatenate(r_parts, axis=1) + br_ref[...])
    ig = _sigmoid(jnp.concatenate(i_parts, axis=1) + bi_ref[...])
    log_a = (LRU_C * r) * _log_sigmoid(lam_ref[...])
    a = jnp.exp(log_a)
    a_ref[...] = a
    b_ref[...] = jnp.sqrt(-jnp.tanh(log_a) * (a * a + 1.0)) * (ig * u)

    def group(g, h):
        base = pl.multiple_of(g * 8, 8)
        a8 = a_ref[pl.ds(base, 8), :]
        b8 = b_ref[pl.ds(base, 8), :]
        rows = []
        for s in range(8):
            h = a8[s:s + 1, :] * h + b8[s:s + 1, :]
            rows.append(h)
        h_ref[pl.ds(base, 8), :] = jnp.concatenate(rows, axis=0)
        return h

    h_last = lax.fori_loop(0, T // 8, group, state_ref[0:1, :])
    state_ref[0:1, :] = h_last
    o_ref[...] = (_gelu_tanh(y_ref[...]) * h_ref[...]).astype(o_ref.dtype)


def _rglru(xy, conv_w, conv_b, w_rg, b_rg, w_ig, b_ig, lam, B, S, T=512):
    C = xy.shape[1] // 2
    T = _tile(S, T)
    nt = S // T
    vec = pl.BlockSpec((1, C), lambda b, t: (0, 0))
    blockdiag = pl.BlockSpec(w_rg.shape, lambda b, t: (0, 0, 0))
    return pl.pallas_call(
        functools.partial(_rglru_kernel, T=T),
        grid=(B, nt),
        in_specs=[pl.BlockSpec((T, C), lambda b, t: (b * nt + t, 0)),
                  pl.BlockSpec((T, C), lambda b, t: (b * nt + t, 1)),
                  pl.BlockSpec((CONV_WIDTH, C), lambda b, t: (0, 0)),
                  vec, blockdiag, vec, blockdiag, vec, vec],
        out_specs=pl.BlockSpec((T, C), lambda b, t: (b * nt + t, 0)),
        out_shape=jax.ShapeDtypeStruct((B * S, C), BF16),
        scratch_shapes=[pltpu.VMEM((T + CONV_HALO, C), F32),
                        pltpu.VMEM((T, C), F32), pltpu.VMEM((T, C), F32), pltpu.VMEM((T, C), F32),
                        pltpu.VMEM((8, C), F32)],
        compiler_params=_params("arbitrary", "arbitrary"),
        name="rglru",
    )(xy, xy, conv_w, conv_b, w_rg, b_rg, w_ig, b_ig, lam)


def _pool_kernel(u_ref, w_ref, s_ref, o_ref, ext_ref, *, T):
    C = u_ref.shape[1]
    G = len(POOL_WINDOWS)
    cg = C // G
    t = pl.program_id(1)

    @pl.when(t == 0)
    def _():
        ext_ref[0:POOL_HALO, :] = jnp.zeros((POOL_HALO, C), F32)

    x = u_ref[...]
    ext_ref[POOL_HALO:POOL_HALO + T, :] = x
    pos = (t * T + 1 + lax.broadcasted_iota(jnp.int32, (T, 1), 0)).astype(F32)
    for g, w in enumerate(POOL_WINDOWS):
        lo = g * cg
        xs = x[:, lo:lo + cg]
        acc = xs
        for k in range(1, w):
            acc = acc + ext_ref[POOL_HALO - k:POOL_HALO - k + T, lo:lo + cg]
        pooled = acc / jnp.minimum(pos, float(w)) - xs
        y = jnp.dot(pooled.astype(BF16), w_ref[g], preferred_element_type=F32)
        o_ref[:, lo:lo + cg] = (y * s_ref[:, lo:lo + cg]).astype(o_ref.dtype)
    ext_ref[0:POOL_HALO, :] = x[T - POOL_HALO:T, :]


def _pool(u, w_pool, scale, B, S, T=512):
    C = u.shape[1]
    T = _tile(S, T)
    nt = S // T
    return pl.pallas_call(
        functools.partial(_pool_kernel, T=T),
        grid=(B, nt),
        in_specs=[pl.BlockSpec((T, C), lambda b, t: (b * nt + t, 0)),
                  pl.BlockSpec(w_pool.shape, lambda b, t: (0, 0, 0)),
                  pl.BlockSpec((1, C), lambda b, t: (0, 0))],
        out_specs=pl.BlockSpec((T, C), lambda b, t: (b * nt + t, 0)),
        out_shape=jax.ShapeDtypeStruct((B * S, C), BF16),
        scratch_shapes=[pltpu.VMEM((T + POOL_HALO, C), F32)],
        compiler_params=_params("arbitrary", "arbitrary"),
        name="multiscale_pool",
    )(u, w_pool, scale)


def _merge_kernel(ya_ref, yb_ref, yc_ref, wa_ref, wb_ref, wc_ref, ga_ref, gb_ref, gc_ref, o_ref):
    out = _sigmoid(ga_ref[...]) * jnp.dot(ya_ref[...], wa_ref[...], preferred_element_type=F32)
    out = out + _sigmoid(gb_ref[...]) * jnp.dot(yb_ref[...], wb_ref[...], preferred_element_type=F32)
    out = out + _sigmoid(gc_ref[...]) * jnp.dot(yc_ref[...], wc_ref[...], preferred_element_type=F32)
    o_ref[...] = out.astype(o_ref.dtype)


def _merge(ya, yb, yc, wa, wb, wc, gates, tm=1024, tn=512):
    M, C = ya.shape
    D = wa.shape[1]
    tm, tn = _tile(M, tm), _tile(D, tn)
    nd = D // tn
    y_spec = pl.BlockSpec((tm, C), lambda i, j: (i, 0))
    w_spec = pl.BlockSpec((C, tn), lambda i, j: (0, j))
    return pl.pallas_call(
        _merge_kernel,
        grid=(M // tm, nd),
        in_specs=[y_spec, y_spec, y_spec, w_spec, w_spec, w_spec,
                  pl.BlockSpec((tm, tn), lambda i, j: (i, j)),
                  pl.BlockSpec((tm, tn), lambda i, j: (i, j + nd)),
                  pl.BlockSpec((tm, tn), lambda i, j: (i, j + 2 * nd))],
        out_specs=pl.BlockSpec((tm, tn), lambda i, j: (i, j)),
        out_shape=jax.ShapeDtypeStruct((M, D), BF16),
        compiler_params=_params("parallel", "arbitrary"),
        name="gated_merge",
    )(ya, yb, yc, wa, wb, wc, gates, gates, gates)


def _layer(x, B, S, g_mix, w_in, b_forget, conv_w, conv_b, w_rg, b_rg, w_ig, b_ig, lam,
           w_pool, pool_scale, w_a, w_b, w_c, w_out, g_ffn, w_ffn_in, w_ffn_out):
    D = x.shape[1]
    H = N_HEADS
    d_rnn = conv_w.shape[1]
    d_attn = H * HEAD_DIM
    d_pool = pool_scale.shape[0]
    row = lambda v: v.reshape(1, -1)

    o_rnn = 2 * d_rnn
    o_qkv = o_rnn + 3 * d_attn
    o_f = o_qkv + H
    o_pool = o_f + d_pool
    w_f32 = jnp.concatenate([w_in[:, :o_rnn], w_in[:, o_f:]], axis=1).astype(BF16)
    w_forget = jnp.pad(w_in[:, o_qkv:o_f], ((0, 0), (0, LANES - H))).astype(BF16)
    xy, pool_in, gates, f_logit = _in_proj(x, row(g_mix), w_f32, w_forget, (o_rnn, d_pool, 3 * D))
    qkv_scale = jnp.ones((1, 3 * d_attn), F32).at[:, :d_attn].set(HEAD_DIM ** -0.5 * LOG2E)
    qkv = _norm_matmul(x, row(g_mix), w_in[:, o_rnn:o_qkv].astype(BF16), qkv_scale, BF16)

    y_a = _rglru(xy, conv_w, row(conv_b), w_rg.astype(BF16), row(b_rg), w_ig.astype(BF16),
                 row(b_ig), row(lam), B, S)

    kbias = _forget_bias(f_logit, jnp.pad(b_forget, (0, LANES - H)).reshape(1, LANES), B, S)
    y_b = _attention(qkv, kbias, B, S)

    y_c = _pool(pool_in, w_pool.astype(BF16), row(pool_scale), B, S)

    merged = _merge(y_a, y_b, y_c, w_a.astype(BF16), w_b.astype(BF16), w_c.astype(BF16), gates,
                    tm=256, tn=D)
    x = _matmul_res(merged, w_out.astype(BF16), x, tm=512, tn=D)
    act = _norm_swiglu(x, row(g_ffn), w_ffn_in.astype(BF16))
    return _matmul_res(act, w_ffn_out.astype(BF16), x, tm=1024, tn=512)


def kernel(x, g_mix, w_in, b_forget, conv_w, conv_b, w_rg, b_rg, w_ig, b_ig, lru_lambda, w_pool,
           pool_scale, w_branch_rnn, w_branch_attn, w_branch_pool, w_out, g_ffn, w_ffn_in,
           w_ffn_out, g_final):
    B, S, D = x.shape
    h = x.reshape(B * S, D)
    for l in range(g_mix.shape[0]):
        h = _layer(h, B, S, g_mix[l], w_in[l], b_forget[l], conv_w[l], conv_b[l], w_rg[l], b_rg[l],
                   w_ig[l], b_ig[l], lru_lambda[l], w_pool[l], pool_scale[l], w_branch_rnn[l],
                   w_branch_attn[l], w_branch_pool[l], w_out[l], g_ffn[l], w_ffn_in[l], w_ffn_out[l])
    return _rmsnorm(h, g_final.reshape(1, -1)).reshape(B, S, D)
```

```python
import functools

import jax
import jax.numpy as jnp
from jax import lax
from jax.experimental import pallas as pl
from jax.experimental.pallas import tpu as pltpu

F32 = jnp.float32
BF16 = jnp.bfloat16

N_RNN_BLOCKS = 8
CONV_WIDTH = 4
LRU_C = 8.0
N_HEADS = 8
HEAD_DIM = 128
POOL_WINDOWS = (2, 4, 8, 16)
POOL_HALO = 16
CONV_HALO = 8
NORM_EPS = 1e-6
LANES = 128

VMEM_LIMIT = 56 * 1024 * 1024


def _params(*sem):
    return pltpu.CompilerParams(dimension_semantics=sem, vmem_limit_bytes=VMEM_LIMIT)


def _tile(n, want):
    t = min(n, want)
    while n % t:
        t //= 2
    return t


def _sigmoid(x):
    return 0.5 * jnp.tanh(0.5 * x) + 0.5


def _log_sigmoid(x):
    return jnp.minimum(x, 0.0) - jnp.log1p(jnp.exp(-jnp.abs(x)))


def _normalize_rows(x, g):
    ms = jnp.mean(x * x, axis=-1, keepdims=True)
    return x * lax.rsqrt(ms + NORM_EPS) * g


def _norm_matmul_kernel(x_ref, g_ref, w_ref, c_ref, o_ref, h_ref):
    @pl.when(pl.program_id(1) == 0)
    def _():
        h_ref[...] = _normalize_rows(x_ref[...], g_ref[...]).astype(BF16)

    acc = jnp.dot(h_ref[...], w_ref[...], preferred_element_type=F32)
    o_ref[...] = (acc * c_ref[...]).astype(o_ref.dtype)


def _norm_matmul(x, g, w, colscale, out_dtype, tm=1024, tn=512):
    M, D = x.shape
    N = w.shape[1]
    tm, tn = _tile(M, tm), _tile(N, tn)
    return pl.pallas_call(
        _norm_matmul_kernel,
        grid=(M // tm, N // tn),
        in_specs=[pl.BlockSpec((tm, D), lambda i, j: (i, 0)),
                  pl.BlockSpec((1, D), lambda i, j: (0, 0)),
                  pl.BlockSpec((D, tn), lambda i, j: (0, j)),
                  pl.BlockSpec((1, tn), lambda i, j: (0, j))],
        out_specs=pl.BlockSpec((tm, tn), lambda i, j: (i, j)),
        out_shape=jax.ShapeDtypeStruct((M, N), out_dtype),
        scratch_shapes=[pltpu.VMEM((tm, D), BF16)],
        compiler_params=_params("arbitrary", "arbitrary"),
        name="norm_matmul",
    )(x, g, w, colscale)


def _in_proj_kernel(x_ref, g_ref, w_ref, wf_ref, *rest, bounds):
    outs, f_ref, h_ref = rest[:-2], rest[-2], rest[-1]
    j = pl.program_id(1)

    @pl.when(j == 0)
    def _():
        h = _normalize_rows(x_ref[...], g_ref[...]).astype(BF16)
        h_ref[...] = h
        f_ref[...] = jnp.dot(h, wf_ref[...], preferred_element_type=F32)

    acc = jnp.dot(h_ref[...], w_ref[...], preferred_element_type=F32)
    tm, tn = acc.shape
    rows = min(tm, tn)
    for k, o_ref in enumerate(outs):
        owns = jnp.broadcast_to((j >= bounds[k]) & (j < bounds[k + 1]), (rows, tn))
        for r in range(0, tm, rows):
            pltpu.store(o_ref.at[r:r + rows, :], acc[r:r + rows, :], mask=owns)


def _in_proj(x, g, w, w_forget, widths, tm=1024, tn=512):
    M, D = x.shape
    N = w.shape[1]
    tm = _tile(M, tm)
    assert all(wd % tn == 0 for wd in widths) and sum(widths) == N
    bounds = [0]
    for wd in widths:
        bounds.append(bounds[-1] + wd // tn)

    def out_map(k):
        lo, n = bounds[k], bounds[k + 1] - bounds[k]
        return lambda i, j: (i, jnp.clip(j - lo, 0, n - 1))

    out_specs = [pl.BlockSpec((tm, tn), out_map(k)) for k in range(len(widths))]
    out_shape = [jax.ShapeDtypeStruct((M, wd), F32) for wd in widths]
    out_specs.append(pl.BlockSpec((tm, LANES), lambda i, j: (i, 0)))
    out_shape.append(jax.ShapeDtypeStruct((M, LANES), F32))
    return pl.pallas_call(
        functools.partial(_in_proj_kernel, bounds=tuple(bounds)),
        grid=(M // tm, N // tn),
        in_specs=[pl.BlockSpec((tm, D), lambda i, j: (i, 0)),
                  pl.BlockSpec((1, D), lambda i, j: (0, 0)),
                  pl.BlockSpec((D, tn), lambda i, j: (0, j)),
                  pl.BlockSpec((D, LANES), lambda i, j: (0, 0))],
        out_specs=out_specs,
        out_shape=out_shape,
        scratch_shapes=[pltpu.VMEM((tm, D), BF16)],
        compiler_params=_params("arbitrary", "arbitrary"),
        name="in_proj",
    )(x, g, w, w_forget)


def _norm_swiglu_kernel(x_ref, g_ref, wg_ref, wu_ref, o_ref, h_ref):
    @pl.when(pl.program_id(1) == 0)
    def _():
        h_ref[...] = _normalize_rows(x_ref[...], g_ref[...]).astype(BF16)

    h = h_ref[...]
    gate = jnp.dot(h, wg_ref[...], preferred_element_type=F32)
    up = jnp.dot(h, wu_ref[...], preferred_element_type=F32)
    o_ref[...] = (gate * _sigmoid(gate) * up).astype(o_ref.dtype)


def _norm_swiglu(x, g, w, tm=1024, tn=512):
    M, D = x.shape
    F = w.shape[1] // 2
    tm, tn = _tile(M, tm), _tile(F, tn)
    nf = F // tn
    return pl.pallas_call(
        _norm_swiglu_kernel,
        grid=(M // tm, nf),
        in_specs=[pl.BlockSpec((tm, D), lambda i, j: (i, 0)),
                  pl.BlockSpec((1, D), lambda i, j: (0, 0)),
                  pl.BlockSpec((D, tn), lambda i, j: (0, j)),
                  pl.BlockSpec((D, tn), lambda i, j: (0, j + nf))],
        out_specs=pl.BlockSpec((tm, tn), lambda i, j: (i, j)),
        out_shape=jax.ShapeDtypeStruct((M, F), BF16),
        scratch_shapes=[pltpu.VMEM((tm, D), BF16)],
        compiler_params=_params("parallel", "arbitrary"),
        name="norm_swiglu",
    )(x, g, w, w)


def _matmul_res_kernel(a_ref, w_ref, r_ref, o_ref):
    o_ref[...] = r_ref[...] + jnp.dot(a_ref[...], w_ref[...], preferred_element_type=F32)


def _matmul_res(a, w, res, tm=512, tn=512):
    M, K = a.shape
    N = w.shape[1]
    tm, tn = _tile(M, tm), _tile(N, tn)
    return pl.pallas_call(
        _matmul_res_kernel,
        grid=(M // tm, N // tn),
        in_specs=[pl.BlockSpec((tm, K), lambda i, j: (i, 0)),
                  pl.BlockSpec((K, tn), lambda i, j: (0, j)),
                  pl.BlockSpec((tm, tn), lambda i, j: (i, j))],
        out_specs=pl.BlockSpec((tm, tn), lambda i, j: (i, j)),
        out_shape=jax.ShapeDtypeStruct((M, N), F32),
        compiler_params=_params("parallel", "arbitrary"),
        name="matmul_res",
    )(a, w, res)


def _rmsnorm_kernel(x_ref, g_ref, o_ref):
    o_ref[...] = _normalize_rows(x_ref[...], g_ref[...])


def _rmsnorm(x, g, tm=512):
    M, D = x.shape
    tm = _tile(M, tm)
    return pl.pallas_call(
        _rmsnorm_kernel,
        grid=(M // tm,),
        in_specs=[pl.BlockSpec((tm, D), lambda i: (i, 0)),
                  pl.BlockSpec((1, D), lambda i: (0, 0))],
        out_specs=pl.BlockSpec((tm, D), lambda i: (i, 0)),
        out_shape=jax.ShapeDtypeStruct((M, D), F32),
        compiler_params=_params("parallel"),
        name="rmsnorm",
    )(x, g)


N_BIAS_COLS = 3
LOG2E = 1.4426950408889634


def _split3(x):
    hi = x.astype(BF16)
    rest = x - hi.astype(F32)
    mid = rest.astype(BF16)
    lo = (rest - mid.astype(F32)).astype(BF16)
    return jnp.concatenate([hi, mid, lo], axis=1)


def _forget_bias_kernel(f_ref, b_ref, tri_ref, sel_ref, o_ref, carry_ref):
    @pl.when(pl.program_id(1) == 0)
    def _():
        carry_ref[...] = jnp.zeros_like(carry_ref)

    x = _log_sigmoid(f_ref[...] + b_ref[...])
    cum3 = jnp.dot(tri_ref[...], _split3(x), preferred_element_type=F32)
    cum = (cum3[:, 0:LANES] + cum3[:, LANES:2 * LANES] + cum3[:, 2 * LANES:3 * LANES]
           + carry_ref[0:1, :])
    T = x.shape[0]
    carry_ref[0:1, :] = cum[T - 1:T, :]
    o_ref[...] = jnp.dot(_split3(cum * (-LOG2E)), sel_ref[...],
                         preferred_element_type=F32).astype(o_ref.dtype)


def _forget_bias(f, b, B, S, T=512):
    H = N_HEADS
    T = _tile(S, T)
    nt = S // T
    tri = jnp.tril(jnp.ones((T, T), BF16))
    r = jnp.arange(N_BIAS_COLS * LANES)[:, None]
    c = jnp.arange(H * LANES)[None, :]
    sel = ((r // LANES == c % LANES) & (r % LANES == c // LANES)).astype(BF16)
    return pl.pallas_call(
        _forget_bias_kernel,
        grid=(B, nt),
        in_specs=[pl.BlockSpec((T, LANES), lambda b_, t: (b_ * nt + t, 0)),
                  pl.BlockSpec((1, LANES), lambda b_, t: (0, 0)),
                  pl.BlockSpec((T, T), lambda b_, t: (0, 0)),
                  pl.BlockSpec(sel.shape, lambda b_, t: (0, 0))],
        out_specs=pl.BlockSpec((T, H * LANES), lambda b_, t: (b_ * nt + t, 0)),
        out_shape=jax.ShapeDtypeStruct((B * S, H * LANES), BF16),
        scratch_shapes=[pltpu.VMEM((8, LANES), F32)],
        compiler_params=_params("arbitrary", "arbitrary"),
        name="forget_bias",
    )(f, b, tri, sel)


SUM_ROWS = 16


def _attn_kernel(q_ref, k_ref, v_ref, kb_ref, o_ref, vt_ref, qt_ref, acc_ref, m_ref, sbuf_ref, *,
                 tq, tk, unroll, ahead):
    i = pl.program_id(2)
    S = k_ref.shape[0]
    Dh = HEAD_DIM

    @pl.when(i == 0)
    def _():
        for c in range(S // tq):
            vt_ref[0:Dh, c * tq:(c + 1) * tq] = v_ref[c * tq:(c + 1) * tq, :].T
        vt_ref[Dh:Dh + SUM_ROWS, :] = jnp.ones((SUM_ROWS, S), BF16)

    qt_ref[0:Dh, :] = q_ref[...].T
    row = lax.broadcasted_iota(jnp.int32, (Dh, tq), 0)
    qt_ref[Dh:2 * Dh, :] = jnp.where(row < N_BIAS_COLS, 1.0, 0.0).astype(BF16)

    m_ref[...] = jnp.full_like(m_ref, -jnp.inf)
    acc_ref[...] = jnp.zeros_like(acc_ref)

    def scores(j):
        start = pl.multiple_of(j * tk, tk)
        k_aug = jnp.concatenate([k_ref[pl.ds(start, tk), :], kb_ref[pl.ds(start, tk), :]], axis=1)
        return jnp.dot(k_aug, qt_ref[...], preferred_element_type=F32)

    per_q = tq // tk
    assert 2 <= ahead <= per_q

    def run(base, count, diag=False):
        pending = {a: sbuf_ref[a] for a in range(ahead)}

        def max_pass(c, m_in):
            s = pending.pop(c)
            if diag:
                key = lax.broadcasted_iota(jnp.int32, (tk, tq), 0) + c * tk
                qry = lax.broadcasted_iota(jnp.int32, (tk, tq), 1)
                s = jnp.where(key <= qry, s, -jnp.inf)
            return s, m_in, jnp.maximum(m_in, jnp.max(s, axis=0, keepdims=True))

        cur = max_pass(0, m_ref[0:1, :])
        for c in range(count):
            if not diag or c + ahead < count:
                pending[c + ahead] = scores(base + c + ahead)
            nxt = max_pass(c + 1, cur[2]) if c + 1 < count else None
            s, m_old, m_new = cur
            p = jnp.exp2(s - m_new).astype(BF16)
            start = pl.multiple_of((base + c) * tk, tk)
            pv = jnp.dot(vt_ref[:, pl.ds(start, tk)], p, preferred_element_type=F32)
            acc_ref[...] = jnp.exp2(m_old - m_new) * acc_ref[...] + pv
            cur = nxt
        m_ref[0:1, :] = m_new
        for t in sorted(pending):
            sbuf_ref[t - count] = pending.pop(t)

    for a in range(ahead):
        sbuf_ref[a] = scores(a)

    n_full = i * per_q
    n_groups = n_full // unroll

    def group(g, carry):
        run(g * unroll, unroll)
        return carry

    lax.fori_loop(0, n_groups, group, 0)

    rem = n_full - n_groups * unroll
    size = unroll // 2
    while size >= per_q:
        @pl.when((rem & size) != 0)
        def _(size=size):
            run(n_groups * unroll + (rem // (2 * size)) * (2 * size), size)
        size //= 2

    run(n_full, per_q, diag=True)
    acc = acc_ref[...]
    o_ref[...] = (acc[0:Dh, :] / acc[Dh:Dh + 1, :]).T.astype(o_ref.dtype)


def _attention(qkv, kbias, B, S, tq=512, tk=256, unroll=16, ahead=2):
    H, Dh = N_HEADS, HEAD_DIM
    tq = _tile(S, tq)
    tk = _tile(tq, tk)
    unroll = max(unroll, tq // tk)
    nq = S // tq
    return pl.pallas_call(
        functools.partial(_attn_kernel, tq=tq, tk=tk, unroll=unroll, ahead=ahead),
        grid=(B, H, nq),
        in_specs=[pl.BlockSpec((tq, Dh), lambda b, h, i: (b * nq + i, h)),
                  pl.BlockSpec((S, Dh), lambda b, h, i: (b, H + h)),
                  pl.BlockSpec((S, Dh), lambda b, h, i: (b, 2 * H + h)),
                  pl.BlockSpec((S, LANES), lambda b, h, i: (b, h))],
        out_specs=pl.BlockSpec((tq, Dh), lambda b, h, i: (b * nq + i, h)),
        out_shape=jax.ShapeDtypeStruct((B * S, H * Dh), BF16),
        scratch_shapes=[pltpu.VMEM((Dh + SUM_ROWS, S), BF16), pltpu.VMEM((2 * Dh, tq), BF16),
                        pltpu.VMEM((Dh + SUM_ROWS, tq), F32), pltpu.VMEM((8, tq), F32),
                        pltpu.VMEM((ahead, tk, tq), F32)],
        compiler_params=_params("arbitrary", "arbitrary", "arbitrary"),
        name="fox_attention",
    )(qkv, qkv, qkv, kbias)


def _gelu_tanh(x):
    c = 0.7978845608028654
    return 0.5 * x * (1.0 + jnp.tanh(c * (x + 0.044715 * (x * x * x))))


def _rglru_kernel(x_ref, y_ref, cw_ref, cb_ref, wr_ref, br_ref, wi_ref, bi_ref, lam_ref,
                  o_ref, ext_ref, a_ref, b_ref, h_ref, state_ref, *, T):
    C = x_ref.shape[1]
    cb = C // N_RNN_BLOCKS

    @pl.when(pl.program_id(1) == 0)
    def _():
        ext_ref[0:CONV_HALO, :] = jnp.zeros((CONV_HALO, C), F32)
        state_ref[...] = jnp.zeros_like(state_ref)

    x = x_ref[...]
    ext_ref[CONV_HALO:CONV_HALO + T, :] = x
    u = cb_ref[...] + x * cw_ref[CONV_WIDTH - 1:CONV_WIDTH, :]
    for k in range(CONV_WIDTH - 1):
        off = CONV_HALO + k - (CONV_WIDTH - 1)
        u = u + ext_ref[off:off + T, :] * cw_ref[k:k + 1, :]
    ext_ref[0:CONV_HALO, :] = x[T - CONV_HALO:T, :]

    ub = u.astype(BF16)
    r_parts, i_parts = [], []
    for n in range(N_RNN_BLOCKS):
        un = ub[:, n * cb:(n + 1) * cb]
        r_parts.append(jnp.dot(un, wr_ref[n], preferred_element_type=F32))
        i_parts.append(jnp.dot(un, wi_ref[n], preferred_element_type=F32))
    r = _sigmoid(jnp.concatenate(r_parts, axis=1) + br_ref[...])
    ig = _sigmoid(jnp.concatenate(i_parts, axis=1) + bi_ref[...])
    log_a = (LRU_C * r) * _log_sigmoid(lam_ref[...])
    a = jnp.exp(log_a)
    a_ref[...] = a
    b_ref[...] = jnp.sqrt(-jnp.tanh(log_a) * (a * a + 1.0)) * (ig * u)

    def group(g, h):
        base = pl.multiple_of(g * 8, 8)
        a8 = a_ref[pl.ds(base, 8), :]
        b8 = b_ref[pl.ds(base, 8), :]
        rows = []
        for s in range(8):
            h = a8[s:s + 1, :] * h + b8[s:s + 1, :]
            rows.append(h)
        h_ref[pl.ds(base, 8), :] = jnp.concatenate(rows, axis=0)
        return h

    h_last = lax.fori_loop(0, T // 8, group, state_ref[0:1, :])
    state_ref[0:1, :] = h_last
    o_ref[...] = (_gelu_tanh(y_ref[...]) * h_ref[...]).astype(o_ref.dtype)


def _rglru(xy, conv_w, conv_b, w_rg, b_rg, w_ig, b_ig, lam, B, S, T=512):
    C = xy.shape[1] // 2
    T = _tile(S, T)
    nt = S // T
    vec = pl.BlockSpec((1, C), lambda b, t: (0, 0))
    blockdiag = pl.BlockSpec(w_rg.shape, lambda b, t: (0, 0, 0))
    return pl.pallas_call(
        functools.partial(_rglru_kernel, T=T),
        grid=(B, nt),
        in_specs=[pl.BlockSpec((T, C), lambda b, t: (b * nt + t, 0)),
                  pl.BlockSpec((T, C), lambda b, t: (b * nt + t, 1)),
                  pl.BlockSpec((CONV_WIDTH, C), lambda b, t: (0, 0)),
                  vec, blockdiag, vec, blockdiag, vec, vec],
        out_specs=pl.BlockSpec((T, C), lambda b, t: (b * nt + t, 0)),
        out_shape=jax.ShapeDtypeStruct((B * S, C), BF16),
        scratch_shapes=[pltpu.VMEM((T + CONV_HALO, C), F32),
                        pltpu.VMEM((T, C), F32), pltpu.VMEM((T, C), F32), pltpu.VMEM((T, C), F32),
                        pltpu.VMEM((8, C), F32)],
        compiler_params=_params("arbitrary", "arbitrary"),
        name="rglru",
    )(xy, xy, conv_w, conv_b, w_rg, b_rg, w_ig, b_ig, lam)


def _pool_kernel(u_ref, w_ref, s_ref, o_ref, ext_ref, *, T):
    C = u_ref.shape[1]
    G = len(POOL_WINDOWS)
    cg = C // G
    t = pl.program_id(1)

    @pl.when(t == 0)
    def _():
        ext_ref[0:POOL_HALO, :] = jnp.zeros((POOL_HALO, C), F32)

    x = u_ref[...]
    ext_ref[POOL_HALO:POOL_HALO + T, :] = x
    pos = (t * T + 1 + lax.broadcasted_iota(jnp.int32, (T, 1), 0)).astype(F32)
    for g, w in enumerate(POOL_WINDOWS):
        lo = g * cg
        xs = x[:, lo:lo + cg]
        acc = xs
        for k in range(1, w):
            acc = acc + ext_ref[POOL_HALO - k:POOL_HALO - k + T, lo:lo + cg]
        pooled = acc / jnp.minimum(pos, float(w)) - xs
        y = jnp.dot(pooled.astype(BF16), w_ref[g], preferred_element_type=F32)
        o_ref[:, lo:lo + cg] = (y * s_ref[:, lo:lo + cg]).astype(o_ref.dtype)
    ext_ref[0:POOL_HALO, :] = x[T - POOL_HALO:T, :]


def _pool(u, w_pool, scale, B, S, T=512):
    C = u.shape[1]
    T = _tile(S, T)
    nt = S // T
    return pl.pallas_call(
        functools.partial(_pool_kernel, T=T),
        grid=(B, nt),
        in_specs=[pl.BlockSpec((T, C), lambda b, t: (b * nt + t, 0)),
                  pl.BlockSpec(w_pool.shape, lambda b, t: (0, 0, 0)),
                  pl.BlockSpec((1, C), lambda b, t: (0, 0))],
        out_specs=pl.BlockSpec((T, C), lambda b, t: (b * nt + t, 0)),
        out_shape=jax.ShapeDtypeStruct((B * S, C), BF16),
        scratch_shapes=[pltpu.VMEM((T + POOL_HALO, C), F32)],
        compiler_params=_params("arbitrary", "arbitrary"),
        name="multiscale_pool",
    )(u, w_pool, scale)


def _merge_kernel(ya_ref, yb_ref, yc_ref, wa_ref, wb_ref, wc_ref, ga_ref, gb_ref, gc_ref, o_ref):
    out = _sigmoid(ga_ref[...]) * jnp.dot(ya_ref[...], wa_ref[...], preferred_element_type=F32)
    out = out + _sigmoid(gb_ref[...]) * jnp.dot(yb_ref[...], wb_ref[...], preferred_element_type=F32)
    out = out + _sigmoid(gc_ref[...]) * jnp.dot(yc_ref[...], wc_ref[...], preferred_element_type=F32)
    o_ref[...] = out.astype(o_ref.dtype)


def _merge(ya, yb, yc, wa, wb, wc, gates, tm=1024, tn=512):
    M, C = ya.shape
    D = wa.shape[1]
    tm, tn = _tile(M, tm), _tile(D, tn)
    nd = D // tn
    y_spec = pl.BlockSpec((tm, C), lambda i, j: (i, 0))
    w_spec = pl.BlockSpec((C, tn), lambda i, j: (0, j))
    return pl.pallas_call(
        _merge_kernel,
        grid=(M // tm, nd),
        in_specs=[y_spec, y_spec, y_spec, w_spec, w_spec, w_spec,
                  pl.BlockSpec((tm, tn), lambda i, j: (i, j)),
                  pl.BlockSpec((tm, tn), lambda i, j: (i, j + nd)),
                  pl.BlockSpec((tm, tn), lambda i, j: (i, j + 2 * nd))],
        out_specs=pl.BlockSpec((tm, tn), lambda i, j: (i, j)),
        out_shape=jax.ShapeDtypeStruct((M, D), BF16),
        compiler_params=_params("parallel", "arbitrary"),
        name="gated_merge",
    )(ya, yb, yc, wa, wb, wc, gates, gates, gates)


def _layer(x, B, S, g_mix, w_in, b_forget, conv_w, conv_b, w_rg, b_rg, w_ig, b_ig, lam,
           w_pool, pool_scale, w_a, w_b, w_c, w_out, g_ffn, w_ffn_in, w_ffn_out):
    D = x.shape[1]
    H = N_HEADS
    d_rnn = conv_w.shape[1]
    d_attn = H * HEAD_DIM
    d_pool = pool_scale.shape[0]
    row = lambda v: v.reshape(1, -1)

    o_rnn = 2 * d_rnn
    o_qkv = o_rnn + 3 * d_attn
    o_f = o_qkv + H
    o_pool = o_f + d_pool
    w_f32 = jnp.concatenate([w_in[:, :o_rnn], w_in[:, o_f:]], axis=1).astype(BF16)
    w_forget = jnp.pad(w_in[:, o_qkv:o_f], ((0, 0), (0, LANES - H))).astype(BF16)
    xy, pool_in, gates, f_logit = _in_proj(x, row(g_mix), w_f32, w_forget, (o_rnn, d_pool, 3 * D))
    qkv_scale = jnp.ones((1, 3 * d_attn), F32).at[:, :d_attn].set(HEAD_DIM ** -0.5 * LOG2E)
    qkv = _norm_matmul(x, row(g_mix), w_in[:, o_rnn:o_qkv].astype(BF16), qkv_scale, BF16)

    y_a = _rglru(xy, conv_w, row(conv_b), w_rg.astype(BF16), row(b_rg), w_ig.astype(BF16),
                 row(b_ig), row(lam), B, S)

    kbias = _forget_bias(f_logit, jnp.pad(b_forget, (0, LANES - H)).reshape(1, LANES), B, S)
    y_b = _attention(qkv, kbias, B, S)

    y_c = _pool(pool_in, w_pool.astype(BF16), row(pool_scale), B, S)

    merged = _merge(y_a, y_b, y_c, w_a.astype(BF16), w_b.astype(BF16), w_c.astype(BF16), gates,
                    tm=256, tn=D)
    x = _matmul_res(merged, w_out.astype(BF16), x, tm=512, tn=D)
    act = _norm_swiglu(x, row(g_ffn), w_ffn_in.astype(BF16))
    return _matmul_res(act, w_ffn_out.astype(BF16), x, tm=1024, tn=512)


def kernel(x, g_mix, w_in, b_forget, conv_w, conv_b, w_rg, b_rg, w_ig, b_ig, lru_lambda, w_pool,
           pool_scale, w_branch_rnn, w_branch_attn, w_branch_pool, w_out, g_ffn, w_ffn_in,
           w_ffn_out, g_final):
    B, S, D = x.shape
    h = x.reshape(B * S, D)
    for l in range(g_mix.shape[0]):
        h = _layer(h, B, S, g_mix[l], w_in[l], b_forget[l], conv_w[l], conv_b[l], w_rg[l], b_rg[l],
                   w_ig[l], b_ig[l], lru_lambda[l], w_pool[l], pool_scale[l], w_branch_rnn[l],
                   w_branch_attn[l], w_branch_pool[l], w_out[l], g_ffn[l], w_ffn_in[l], w_ffn_out[l])
    return _rmsnorm(h, g_final.reshape(1, -1)).reshape(B, S, D)
```

```python
import functools

import jax
import jax.numpy as jnp
from jax import lax
from jax.experimental import pallas as pl
from jax.experimental.pallas import tpu as pltpu

F32 = jnp.float32
BF16 = jnp.bfloat16

N_RNN_BLOCKS = 8
CONV_WIDTH = 4
LRU_C = 8.0
N_HEADS = 8
HEAD_DIM = 128
POOL_WINDOWS = (2, 4, 8, 16)
POOL_HALO = 16
CONV_HALO = 8
NORM_EPS = 1e-6
LANES = 128

VMEM_LIMIT = 56 * 1024 * 1024


def _params(*sem):
    return pltpu.CompilerParams(dimension_semantics=sem, vmem_limit_bytes=VMEM_LIMIT)


def _tile(n, want):
    t = min(n, want)
    while n % t:
        t //= 2
    return t


def _sigmoid(x):
    return 0.5 * jnp.tanh(0.5 * x) + 0.5


def _log_sigmoid(x):
    return jnp.minimum(x, 0.0) - jnp.log1p(jnp.exp(-jnp.abs(x)))


def _normalize_rows(x, g):
    ms = jnp.mean(x * x, axis=-1, keepdims=True)
    return x * lax.rsqrt(ms + NORM_EPS) * g


def _norm_matmul_kernel(x_ref, g_ref, w_ref, c_ref, o_ref, h_ref):
    @pl.when(pl.program_id(1) == 0)
    def _():
        h_ref[...] = _normalize_rows(x_ref[...], g_ref[...]).astype(BF16)

    acc = jnp.dot(h_ref[...], w_ref[...], preferred_element_type=F32)
    o_ref[...] = (acc * c_ref[...]).astype(o_ref.dtype)


def _norm_matmul(x, g, w, colscale, out_dtype, tm=1024, tn=1024):
    M, D = x.shape
    N = w.shape[1]
    tm, tn = _tile(M, tm), _tile(N, tn)
    return pl.pallas_call(
        _norm_matmul_kernel,
        grid=(M // tm, N // tn),
        in_specs=[pl.BlockSpec((tm, D), lambda i, j: (i, 0)),
                  pl.BlockSpec((1, D), lambda i, j: (0, 0)),
                  pl.BlockSpec((D, tn), lambda i, j: (0, j)),
                  pl.BlockSpec((1, tn), lambda i, j: (0, j))],
        out_specs=[pl.BlockSpec((tm, tn), lambda i, j: (i, j)),
                   pl.BlockSpec((tm, D), lambda i, j: (i, 0))],
        out_shape=[jax.ShapeDtypeStruct((M, N), out_dtype), jax.ShapeDtypeStruct((M, D), BF16)],
        compiler_params=_params("arbitrary", "arbitrary"),
        name="norm_matmul",
    )(x, g, w, colscale)


def _in_proj_kernel(h_ref, w_ref, wf_ref, *rest, bounds):
    outs, f_ref = rest[:-1], rest[-1]
    j = pl.program_id(1)

    @pl.when(j == 0)
    def _():
        f_ref[...] = jnp.dot(h_ref[...], wf_ref[...], preferred_element_type=F32)

    acc = jnp.dot(h_ref[...], w_ref[...], preferred_element_type=F32)
    tm, tn = acc.shape
    rows = min(tm, tn)
    for k, o_ref in enumerate(outs):
        owns = jnp.broadcast_to((j >= bounds[k]) & (j < bounds[k + 1]), (rows, tn))
        for r in range(0, tm, rows):
            pltpu.store(o_ref.at[r:r + rows, :], acc[r:r + rows, :], mask=owns)


def _in_proj(h, w, w_forget, widths, tm=1024, tn=1024):
    M, D = h.shape
    N = w.shape[1]
    tm = _tile(M, tm)
    assert all(wd % tn == 0 for wd in widths) and sum(widths) == N
    bounds = [0]
    for wd in widths:
        bounds.append(bounds[-1] + wd // tn)

    def out_map(k):
        lo, n = bounds[k], bounds[k + 1] - bounds[k]
        return lambda i, j: (i, jnp.clip(j - lo, 0, n - 1))

    out_specs = [pl.BlockSpec((tm, tn), out_map(k)) for k in range(len(widths))]
    out_shape = [jax.ShapeDtypeStruct((M, wd), F32) for wd in widths]
    out_specs.append(pl.BlockSpec((tm, LANES), lambda i, j: (i, 0)))
    out_shape.append(jax.ShapeDtypeStruct((M, LANES), F32))
    return pl.pallas_call(
        functools.partial(_in_proj_kernel, bounds=tuple(bounds)),
        grid=(M // tm, N // tn),
        in_specs=[pl.BlockSpec((tm, D), lambda i, j: (i, 0)),
                  pl.BlockSpec((D, tn), lambda i, j: (0, j)),
                  pl.BlockSpec((D, LANES), lambda i, j: (0, 0))],
        out_specs=out_specs,
        out_shape=out_shape,
        compiler_params=_params("arbitrary", "arbitrary"),
        name="in_proj",
    )(h, w, w_forget)


def _norm_swiglu_kernel(x_ref, g_ref, wg_ref, wu_ref, o_ref, h_ref):
    @pl.when(pl.program_id(1) == 0)
    def _():
        h_ref[...] = _normalize_rows(x_ref[...], g_ref[...]).astype(BF16)

    h = h_ref[...]
    gate = jnp.dot(h, wg_ref[...], preferred_element_type=F32)
    up = jnp.dot(h, wu_ref[...], preferred_element_type=F32)
    o_ref[...] = (gate * _sigmoid(gate) * up).astype(o_ref.dtype)


def _norm_swiglu(x, g, w, tm=1024, tn=512):
    M, D = x.shape
    F = w.shape[1] // 2
    tm, tn = _tile(M, tm), _tile(F, tn)
    nf = F // tn
    return pl.pallas_call(
        _norm_swiglu_kernel,
        grid=(M // tm, nf),
        in_specs=[pl.BlockSpec((tm, D), lambda i, j: (i, 0)),
                  pl.BlockSpec((1, D), lambda i, j: (0, 0)),
                  pl.BlockSpec((D, tn), lambda i, j: (0, j)),
                  pl.BlockSpec((D, tn), lambda i, j: (0, j + nf))],
        out_specs=pl.BlockSpec((tm, tn), lambda i, j: (i, j)),
        out_shape=jax.ShapeDtypeStruct((M, F), BF16),
        scratch_shapes=[pltpu.VMEM((tm, D), BF16)],
        compiler_params=_params("parallel", "arbitrary"),
        name="norm_swiglu",
    )(x, g, w, w)


def _matmul_res_kernel(a_ref, w_ref, r_ref, o_ref):
    o_ref[...] = r_ref[...] + jnp.dot(a_ref[...], w_ref[...], preferred_element_type=F32)


def _matmul_res(a, w, res, tm=512, tn=512):
    M, K = a.shape
    N = w.shape[1]
    tm, tn = _tile(M, tm), _tile(N, tn)
    return pl.pallas_call(
        _matmul_res_kernel,
        grid=(M // tm, N // tn),
        in_specs=[pl.BlockSpec((tm, K), lambda i, j: (i, 0)),
                  pl.BlockSpec((K, tn), lambda i, j: (0, j)),
                  pl.BlockSpec((tm, tn), lambda i, j: (i, j))],
        out_specs=pl.BlockSpec((tm, tn), lambda i, j: (i, j)),
        out_shape=jax.ShapeDtypeStruct((M, N), F32),
        compiler_params=_params("parallel", "arbitrary"),
        name="matmul_res",
    )(a, w, res)


def _rmsnorm_kernel(x_ref, g_ref, o_ref):
    o_ref[...] = _normalize_rows(x_ref[...], g_ref[...])


def _rmsnorm(x, g, tm=512):
    M, D = x.shape
    tm = _tile(M, tm)
    return pl.pallas_call(
        _rmsnorm_kernel,
        grid=(M // tm,),
        in_specs=[pl.BlockSpec((tm, D), lambda i: (i, 0)),
                  pl.BlockSpec((1, D), lambda i: (0, 0))],
        out_specs=pl.BlockSpec((tm, D), lambda i: (i, 0)),
        out_shape=jax.ShapeDtypeStruct((M, D), F32),
        compiler_params=_params("parallel"),
        name="rmsnorm",
    )(x, g)


N_BIAS_COLS = 3
LOG2E = 1.4426950408889634


def _split3(x):
    hi = x.astype(BF16)
    rest = x - hi.astype(F32)
    mid = rest.astype(BF16)
    lo = (rest - mid.astype(F32)).astype(BF16)
    return jnp.concatenate([hi, mid, lo], axis=1)


def _forget_bias_kernel(f_ref, b_ref, tri_ref, sel_ref, o_ref, carry_ref):
    @pl.when(pl.program_id(1) == 0)
    def _():
        carry_ref[...] = jnp.zeros_like(carry_ref)

    x = _log_sigmoid(f_ref[...] + b_ref[...])
    cum3 = jnp.dot(tri_ref[...], _split3(x), preferred_element_type=F32)
    cum = (cum3[:, 0:LANES] + cum3[:, LANES:2 * LANES] + cum3[:, 2 * LANES:3 * LANES]
           + carry_ref[0:1, :])
    T = x.shape[0]
    carry_ref[0:1, :] = cum[T - 1:T, :]
    o_ref[...] = jnp.dot(_split3(cum * (-LOG2E)), sel_ref[...],
                         preferred_element_type=F32).astype(o_ref.dtype)


def _forget_bias(f, b, B, S, T=512):
    H = N_HEADS
    T = _tile(S, T)
    nt = S // T
    tri = jnp.tril(jnp.ones((T, T), BF16))
    r = jnp.arange(N_BIAS_COLS * LANES)[:, None]
    c = jnp.arange(H * LANES)[None, :]
    sel = ((r // LANES == c % LANES) & (r % LANES == c // LANES)).astype(BF16)
    return pl.pallas_call(
        _forget_bias_kernel,
        grid=(B, nt),
        in_specs=[pl.BlockSpec((T, LANES), lambda b_, t: (b_ * nt + t, 0)),
                  pl.BlockSpec((1, LANES), lambda b_, t: (0, 0)),
                  pl.BlockSpec((T, T), lambda b_, t: (0, 0)),
                  pl.BlockSpec(sel.shape, lambda b_, t: (0, 0))],
        out_specs=pl.BlockSpec((T, H * LANES), lambda b_, t: (b_ * nt + t, 0)),
        out_shape=jax.ShapeDtypeStruct((B * S, H * LANES), BF16),
        scratch_shapes=[pltpu.VMEM((8, LANES), F32)],
        compiler_params=_params("arbitrary", "arbitrary"),
        name="forget_bias",
    )(f, b, tri, sel)


SUM_ROWS = 16


def _attn_kernel(q_ref, k_ref, v_ref, kb_ref, o_ref, vt_ref, qt_ref, acc_ref, m_ref, sbuf_ref, *,
                 tq, tk, unroll, ahead):
    i = pl.program_id(2)
    S = k_ref.shape[0]
    Dh = HEAD_DIM

    @pl.when(i == 0)
    def _():
        for c in range(S // tq):
            vt_ref[0:Dh, c * tq:(c + 1) * tq] = v_ref[c * tq:(c + 1) * tq, :].T
        vt_ref[Dh:Dh + SUM_ROWS, :] = jnp.ones((SUM_ROWS, S), BF16)

    qt_ref[0:Dh, :] = q_ref[...].T
    row = lax.broadcasted_iota(jnp.int32, (Dh, tq), 0)
    qt_ref[Dh:2 * Dh, :] = jnp.where(row < N_BIAS_COLS, 1.0, 0.0).astype(BF16)

    m_ref[...] = jnp.full_like(m_ref, -jnp.inf)
    acc_ref[...] = jnp.zeros_like(acc_ref)

    def scores(j):
        start = pl.multiple_of(j * tk, tk)
        k_aug = jnp.concatenate([k_ref[pl.ds(start, tk), :], kb_ref[pl.ds(start, tk), :]], axis=1)
        return jnp.dot(k_aug, qt_ref[...], preferred_element_type=F32)

    per_q = tq // tk
    assert 2 <= ahead <= per_q

    def run(base, count, diag=False):
        pending = {a: sbuf_ref[a] for a in range(ahead)}

        def max_pass(c, m_in):
            s = pending.pop(c)
            if diag:
                key = lax.broadcasted_iota(jnp.int32, (tk, tq), 0) + c * tk
                qry = lax.broadcasted_iota(jnp.int32, (tk, tq), 1)
                s = jnp.where(key <= qry, s, -jnp.inf)
            return s, m_in, jnp.maximum(m_in, jnp.max(s, axis=0, keepdims=True))

        cur = max_pass(0, m_ref[0:1, :])
        for c in range(count):
            if not diag or c + ahead < count:
                pending[c + ahead] = scores(base + c + ahead)
            nxt = max_pass(c + 1, cur[2]) if c + 1 < count else None
            s, m_old, m_new = cur
            p = jnp.exp2(s - m_new).astype(BF16)
            start = pl.multiple_of((base + c) * tk, tk)
            pv = jnp.dot(vt_ref[:, pl.ds(start, tk)], p, preferred_element_type=F32)
            acc_ref[...] = jnp.exp2(m_old - m_new) * acc_ref[...] + pv
            cur = nxt
        m_ref[0:1, :] = m_new
        for t in sorted(pending):
            sbuf_ref[t - count] = pending.pop(t)

    for a in range(ahead):
        sbuf_ref[a] = scores(a)

    n_full = i * per_q
    n_groups = n_full // unroll

    def group(g, carry):
        run(g * unroll, unroll)
        return carry

    lax.fori_loop(0, n_groups, group, 0)

    rem = n_full - n_groups * unroll
    size = unroll // 2
    while size >= per_q:
        @pl.when((rem & size) != 0)
        def _(size=size):
            run(n_groups * unroll + (rem // (2 * size)) * (2 * size), size)
        size //= 2

    run(n_full, per_q, diag=True)
    acc = acc_ref[...]
    o_ref[...] = (acc[0:Dh, :] / acc[Dh:Dh + 1, :]).T.astype(o_ref.dtype)


def _attention(qkv, kbias, B, S, tq=512, tk=256, unroll=16, ahead=2):
    H, Dh = N_HEADS, HEAD_DIM
    tq = _tile(S, tq)
    tk = _tile(tq, tk)
    unroll = max(unroll, tq // tk)
    nq = S // tq
    return pl.pallas_call(
        functools.partial(_attn_kernel, tq=tq, tk=tk, unroll=unroll, ahead=ahead),
        grid=(B, H, nq),
        in_specs=[pl.BlockSpec((tq, Dh), lambda b, h, i: (b * nq + i, h)),
                  pl.BlockSpec((S, Dh), lambda b, h, i: (b, H + h)),
                  pl.BlockSpec((S, Dh), lambda b, h, i: (b, 2 * H + h)),
                  pl.BlockSpec((S, LANES), lambda b, h, i: (b, h))],
        out_specs=pl.BlockSpec((tq, Dh), lambda b, h, i: (b * nq + i, h)),
        out_shape=jax.ShapeDtypeStruct((B * S, H * Dh), BF16),
        scratch_shapes=[pltpu.VMEM((Dh + SUM_ROWS, S), BF16), pltpu.VMEM((2 * Dh, tq), BF16),
                        pltpu.VMEM((Dh + SUM_ROWS, tq), F32), pltpu.VMEM((8, tq), F32),
                        pltpu.VMEM((ahead, tk, tq), F32)],
        compiler_params=_params("arbitrary", "arbitrary", "arbitrary"),
        name="fox_attention",
    )(qkv, qkv, qkv, kbias)


def _gelu_tanh(x):
    c = 0.7978845608028654
    return 0.5 * x * (1.0 + jnp.tanh(c * (x + 0.044715 * (x * x * x))))


def _rglru_kernel(x_ref, y_ref, cw_ref, cb_ref, wr_ref, br_ref, wi_ref, bi_ref, lam_ref,
                  o_ref, ext_ref, a_ref, b_ref, h_ref, state_ref, *, T):
    C = x_ref.shape[1]
    cb = C // N_RNN_BLOCKS

    @pl.when(pl.program_id(1) == 0)
    def _():
        ext_ref[0:CONV_HALO, :] = jnp.zeros((CONV_HALO, C), F32)
        state_ref[...] = jnp.zeros_like(state_ref)

    x = x_ref[...]
    ext_ref[CONV_HALO:CONV_HALO + T, :] = x
    u = cb_ref[...] + x * cw_ref[CONV_WIDTH - 1:CONV_WIDTH, :]
    for k in range(CONV_WIDTH - 1):
        off = CONV_HALO + k - (CONV_WIDTH - 1)
        u = u + ext_ref[off:off + T, :] * cw_ref[k:k + 1, :]
    ext_ref[0:CONV_HALO, :] = x[T - CONV_HALO:T, :]

    ub = u.astype(BF16)
    r_parts, i_parts = [], []
    for n in range(N_RNN_BLOCKS):
        un = ub[:, n * cb:(n + 1) * cb]
        r_parts.append(jnp.dot(un, wr_ref[n], preferred_element_type=F32))
        i_parts.append(jnp.dot(un, wi_ref[n], preferred_element_type=F32))
    r = _sigmoid(jnp.concatenate(r_parts, axis=1) + br_ref[...])
    ig = _sigmoid(jnp.concatenate(i_parts, axis=1) + bi_ref[...])
    log_a = (LRU_C * r) * _log_sigmoid(lam_ref[...])
    a = jnp.exp(log_a)
    a_ref[...] = a
    b_ref[...] = jnp.sqrt(-jnp.tanh(log_a) * (a * a + 1.0)) * (ig * u)

    def group(g, h):
        base = pl.multiple_of(g * 8, 8)
        a8 = a_ref[pl.ds(base, 8), :]
        b8 = b_ref[pl.ds(base, 8), :]
        rows = []
        for s in range(8):
            h = a8[s:s + 1, :] * h + b8[s:s + 1, :]
            rows.append(h)
        h_ref[pl.ds(base, 8), :] = jnp.concatenate(rows, axis=0)
        return h

    h_last = lax.fori_loop(0, T // 8, group, state_ref[0:1, :])
    state_ref[0:1, :] = h_last
    o_ref[...] = (_gelu_tanh(y_ref[...]) * h_ref[...]).astype(o_ref.dtype)


def _rglru(xy, conv_w, conv_b, w_rg, b_rg, w_ig, b_ig, lam, B, S, T=512):
    C = xy.shape[1] // 2
    T = _tile(S, T)
    nt = S // T
    vec = pl.BlockSpec((1, C), lambda b, t: (0, 0))
    blockdiag = pl.BlockSpec(w_rg.shape, lambda b, t: (0, 0, 0))
    return pl.pallas_call(
        functools.partial(_rglru_kernel, T=T),
        grid=(B, nt),
        in_specs=[pl.BlockSpec((T, C), lambda b, t: (b * nt + t, 0)),
                  pl.BlockSpec((T, C), lambda b, t: (b * nt + t, 1)),
                  pl.BlockSpec((CONV_WIDTH, C), lambda b, t: (0, 0)),
                  vec, blockdiag, vec, blockdiag, vec, vec],
        out_specs=pl.BlockSpec((T, C), lambda b, t: (b * nt + t, 0)),
        out_shape=jax.ShapeDtypeStruct((B * S, C), BF16),
        scratch_shapes=[pltpu.VMEM((T + CONV_HALO, C), F32),
                        pltpu.VMEM((T, C), F32), pltpu.VMEM((T, C), F32), pltpu.VMEM((T, C), F32),
                        pltpu.VMEM((8, C), F32)],
        compiler_params=_params("arbitrary", "arbitrary"),
        name="rglru",
    )(xy, xy, conv_w, conv_b, w_rg, b_rg, w_ig, b_ig, lam)


def _pool_kernel(u_ref, w_ref, s_ref, o_ref, ext_ref, *, T):
    C = u_ref.shape[1]
    G = len(POOL_WINDOWS)
    cg = C // G
    t = pl.program_id(1)

    @pl.when(t == 0)
    def _():
        ext_ref[0:POOL_HALO, :] = jnp.zeros((POOL_HALO, C), F32)

    x = u_ref[...]
    ext_ref[POOL_HALO:POOL_HALO + T, :] = x
    pos = (t * T + 1 + lax.broadcasted_iota(jnp.int32, (T, 1), 0)).astype(F32)
    for g, w in enumerate(POOL_WINDOWS):
        lo = g * cg
        xs = x[:, lo:lo + cg]
        acc = xs
        for k in range(1, w):
            acc = acc + ext_ref[POOL_HALO - k:POOL_HALO - k + T, lo:lo + cg]
        pooled = acc / jnp.minimum(pos, float(w)) - xs
        y = jnp.dot(pooled.astype(BF16), w_ref[g], preferred_element_type=F32)
        o_ref[:, lo:lo + cg] = (y * s_ref[:, lo:lo + cg]).astype(o_ref.dtype)
    ext_ref[0:POOL_HALO, :] = x[T - POOL_HALO:T, :]


def _pool(u, w_pool, scale, B, S, T=512):
    C = u.shape[1]
    T = _tile(S, T)
    nt = S // T
    return pl.pallas_call(
        functools.partial(_pool_kernel, T=T),
        grid=(B, nt),
        in_specs=[pl.BlockSpec((T, C), lambda b, t: (b * nt + t, 0)),
                  pl.BlockSpec(w_pool.shape, lambda b, t: (0, 0, 0)),
                  pl.BlockSpec((1, C), lambda b, t: (0, 0))],
        out_specs=pl.BlockSpec((T, C), lambda b, t: (b * nt + t, 0)),
        out_shape=jax.ShapeDtypeStruct((B * S, C), BF16),
        scratch_shapes=[pltpu.VMEM((T + POOL_HALO, C), F32)],
        compiler_params=_params("arbitrary", "arbitrary"),
        name="multiscale_pool",
    )(u, w_pool, scale)


def _merge_kernel(ya_ref, yb_ref, yc_ref, wa_ref, wb_ref, wc_ref, ga_ref, gb_ref, gc_ref, o_ref):
    out = _sigmoid(ga_ref[...]) * jnp.dot(ya_ref[...], wa_ref[...], preferred_element_type=F32)
    out = out + _sigmoid(gb_ref[...]) * jnp.dot(yb_ref[...], wb_ref[...], preferred_element_type=F32)
    out = out + _sigmoid(gc_ref[...]) * jnp.dot(yc_ref[...], wc_ref[...], preferred_element_type=F32)
    o_ref[...] = out.astype(o_ref.dtype)


def _merge(ya, yb, yc, wa, wb, wc, gates, tm=1024, tn=512):
    M, C = ya.shape
    D = wa.shape[1]
    tm, tn = _tile(M, tm), _tile(D, tn)
    nd = D // tn
    y_spec = pl.BlockSpec((tm, C), lambda i, j: (i, 0))
    w_spec = pl.BlockSpec((C, tn), lambda i, j: (0, j))
    return pl.pallas_call(
        _merge_kernel,
        grid=(M // tm, nd),
        in_specs=[y_spec, y_spec, y_spec, w_spec, w_spec, w_spec,
                  pl.BlockSpec((tm, tn), lambda i, j: (i, j)),
                  pl.BlockSpec((tm, tn), lambda i, j: (i, j + nd)),
                  pl.BlockSpec((tm, tn), lambda i, j: (i, j + 2 * nd))],
        out_specs=pl.BlockSpec((tm, tn), lambda i, j: (i, j)),
        out_shape=jax.ShapeDtypeStruct((M, D), BF16),
        compiler_params=_params("parallel", "arbitrary"),
        name="gated_merge",
    )(ya, yb, yc, wa, wb, wc, gates, gates, gates)


def _layer(x, B, S, g_mix, w_in, b_forget, conv_w, conv_b, w_rg, b_rg, w_ig, b_ig, lam,
           w_pool, pool_scale, w_a, w_b, w_c, w_out, g_ffn, w_ffn_in, w_ffn_out):
    D = x.shape[1]
    H = N_HEADS
    d_rnn = conv_w.shape[1]
    d_attn = H * HEAD_DIM
    d_pool = pool_scale.shape[0]
    row = lambda v: v.reshape(1, -1)

    o_rnn = 2 * d_rnn
    o_qkv = o_rnn + 3 * d_attn
    o_f = o_qkv + H
    o_pool = o_f + d_pool
    w_f32 = jnp.concatenate([w_in[:, :o_rnn], w_in[:, o_f:]], axis=1).astype(BF16)
    w_forget = jnp.pad(w_in[:, o_qkv:o_f], ((0, 0), (0, LANES - H))).astype(BF16)
    qkv_scale = jnp.ones((1, 3 * d_attn), F32).at[:, :d_attn].set(HEAD_DIM ** -0.5 * LOG2E)
    qkv, h = _norm_matmul(x, row(g_mix), w_in[:, o_rnn:o_qkv].astype(BF16), qkv_scale, BF16)
    xy, pool_in, gates, f_logit = _in_proj(h, w_f32, w_forget, (o_rnn, d_pool, 3 * D))

    y_a = _rglru(xy, conv_w, row(conv_b), w_rg.astype(BF16), row(b_rg), w_ig.astype(BF16),
                 row(b_ig), row(lam), B, S)

    kbias = _forget_bias(f_logit, jnp.pad(b_forget, (0, LANES - H)).reshape(1, LANES), B, S)
    y_b = _attention(qkv, kbias, B, S)

    y_c = _pool(pool_in, w_pool.astype(BF16), row(pool_scale), B, S)

    merged = _merge(y_a, y_b, y_c, w_a.astype(BF16), w_b.astype(BF16), w_c.astype(BF16), gates,
                    tm=256, tn=D)
    x = _matmul_res(merged, w_out.astype(BF16), x, tm=512, tn=D)
    act = _norm_swiglu(x, row(g_ffn), w_ffn_in.astype(BF16))
    return _matmul_res(act, w_ffn_out.astype(BF16), x, tm=1024, tn=512)


def kernel(x, g_mix, w_in, b_forget, conv_w, conv_b, w_rg, b_rg, w_ig, b_ig, lru_lambda, w_pool,
           pool_scale, w_branch_rnn, w_branch_attn, w_branch_pool, w_out, g_ffn, w_ffn_in,
           w_ffn_out, g_final):
    B, S, D = x.shape
    h = x.reshape(B * S, D)
    for l in range(g_mix.shape[0]):
        h = _layer(h, B, S, g_mix[l], w_in[l], b_forget[l], conv_w[l], conv_b[l], w_rg[l], b_rg[l],
                   w_ig[l], b_ig[l], lru_lambda[l], w_pool[l], pool_scale[l], w_branch_rnn[l],
                   w_branch_attn[l], w_branch_pool[l], w_out[l], g_ffn[l], w_ffn_in[l], w_ffn_out[l])
    return _rmsnorm(h, g_final.reshape(1, -1)).reshape(B, S, D)
```

```python
import functools

import jax
import jax.numpy as jnp
from jax import lax
from jax.experimental import pallas as pl
from jax.experimental.pallas import tpu as pltpu

F32 = jnp.float32
BF16 = jnp.bfloat16

N_RNN_BLOCKS = 8
CONV_WIDTH = 4
LRU_C = 8.0
N_HEADS = 8
HEAD_DIM = 128
POOL_WINDOWS = (2, 4, 8, 16)
POOL_HALO = 16
CONV_HALO = 8
NORM_EPS = 1e-6
LANES = 128

VMEM_LIMIT = 56 * 1024 * 1024


def _params(*sem):
    return pltpu.CompilerParams(dimension_semantics=sem, vmem_limit_bytes=VMEM_LIMIT)


def _tile(n, want):
    t = min(n, want)
    while n % t:
        t //= 2
    return t


def _sigmoid(x):
    return 0.5 * jnp.tanh(0.5 * x) + 0.5


def _log_sigmoid(x):
    return jnp.minimum(x, 0.0) - jnp.log1p(jnp.exp(-jnp.abs(x)))


def _normalize_rows(x, g):
    ms = jnp.mean(x * x, axis=-1, keepdims=True)
    return x * lax.rsqrt(ms + NORM_EPS) * g


def _norm_matmul_kernel(x_ref, g_ref, w_ref, c_ref, o_ref, h_ref):
    @pl.when(pl.program_id(1) == 0)
    def _():
        h_ref[...] = _normalize_rows(x_ref[...], g_ref[...]).astype(BF16)

    acc = jnp.dot(h_ref[...], w_ref[...], preferred_element_type=F32)
    o_ref[...] = (acc * c_ref[...]).astype(o_ref.dtype)


def _norm_matmul(x, g, w, colscale, out_dtype, tm=1024, tn=1024):
    M, D = x.shape
    N = w.shape[1]
    tm, tn = _tile(M, tm), _tile(N, tn)
    return pl.pallas_call(
        _norm_matmul_kernel,
        grid=(M // tm, N // tn),
        in_specs=[pl.BlockSpec((tm, D), lambda i, j: (i, 0)),
                  pl.BlockSpec((1, D), lambda i, j: (0, 0)),
                  pl.BlockSpec((D, tn), lambda i, j: (0, j)),
                  pl.BlockSpec((1, tn), lambda i, j: (0, j))],
        out_specs=[pl.BlockSpec((tm, tn), lambda i, j: (i, j)),
                   pl.BlockSpec((tm, D), lambda i, j: (i, 0))],
        out_shape=[jax.ShapeDtypeStruct((M, N), out_dtype), jax.ShapeDtypeStruct((M, D), BF16)],
        compiler_params=_params("arbitrary", "arbitrary"),
        name="norm_matmul",
    )(x, g, w, colscale)


def _in_proj_kernel(h_ref, w_ref, wf_ref, *rest, bounds):
    outs, f_ref = rest[:-1], rest[-1]
    j = pl.program_id(1)

    @pl.when(j == 0)
    def _():
        f_ref[...] = jnp.dot(h_ref[...], wf_ref[...], preferred_element_type=F32)

    acc = jnp.dot(h_ref[...], w_ref[...], preferred_element_type=F32)
    tm, tn = acc.shape
    rows = min(tm, tn)
    for k, o_ref in enumerate(outs):
        owns = jnp.broadcast_to((j >= bounds[k]) & (j < bounds[k + 1]), (rows, tn))
        for r in range(0, tm, rows):
            pltpu.store(o_ref.at[r:r + rows, :], acc[r:r + rows, :], mask=owns)


def _in_proj(h, w, w_forget, widths, tm=1024, tn=1024):
    M, D = h.shape
    N = w.shape[1]
    tm = _tile(M, tm)
    assert all(wd % tn == 0 for wd in widths) and sum(widths) == N
    bounds = [0]
    for wd in widths:
        bounds.append(bounds[-1] + wd // tn)

    def out_map(k):
        lo, n = bounds[k], bounds[k + 1] - bounds[k]
        return lambda i, j: (i, jnp.clip(j - lo, 0, n - 1))

    out_specs = [pl.BlockSpec((tm, tn), out_map(k)) for k in range(len(widths))]
    out_shape = [jax.ShapeDtypeStruct((M, wd), F32) for wd in widths]
    out_specs.append(pl.BlockSpec((tm, LANES), lambda i, j: (i, 0)))
    out_shape.append(jax.ShapeDtypeStruct((M, LANES), F32))
    return pl.pallas_call(
        functools.partial(_in_proj_kernel, bounds=tuple(bounds)),
        grid=(M // tm, N // tn),
        in_specs=[pl.BlockSpec((tm, D), lambda i, j: (i, 0)),
                  pl.BlockSpec((D, tn), lambda i, j: (0, j)),
                  pl.BlockSpec((D, LANES), lambda i, j: (0, 0))],
        out_specs=out_specs,
        out_shape=out_shape,
        compiler_params=_params("arbitrary", "arbitrary"),
        name="in_proj",
    )(h, w, w_forget)


def _norm_swiglu_kernel(x_ref, g_ref, wg_ref, wu_ref, o_ref, h_ref):
    @pl.when(pl.program_id(1) == 0)
    def _():
        h_ref[...] = _normalize_rows(x_ref[...], g_ref[...]).astype(BF16)

    h = h_ref[...]
    gate = jnp.dot(h, wg_ref[...], preferred_element_type=F32)
    up = jnp.dot(h, wu_ref[...], preferred_element_type=F32)
    o_ref[...] = (gate * _sigmoid(gate) * up).astype(o_ref.dtype)


def _norm_swiglu(x, g, w, tm=1024, tn=512):
    M, D = x.shape
    F = w.shape[1] // 2
    tm, tn = _tile(M, tm), _tile(F, tn)
    nf = F // tn
    return pl.pallas_call(
        _norm_swiglu_kernel,
        grid=(M // tm, nf),
        in_specs=[pl.BlockSpec((tm, D), lambda i, j: (i, 0)),
                  pl.BlockSpec((1, D), lambda i, j: (0, 0)),
                  pl.BlockSpec((D, tn), lambda i, j: (0, j)),
                  pl.BlockSpec((D, tn), lambda i, j: (0, j + nf))],
        out_specs=pl.BlockSpec((tm, tn), lambda i, j: (i, j)),
        out_shape=jax.ShapeDtypeStruct((M, F), BF16),
        scratch_shapes=[pltpu.VMEM((tm, D), BF16)],
        compiler_params=_params("parallel", "arbitrary"),
        name="norm_swiglu",
    )(x, g, w, w)


def _matmul_res_kernel(a_ref, w_ref, r_ref, o_ref):
    o_ref[...] = r_ref[...] + jnp.dot(a_ref[...], w_ref[...], preferred_element_type=F32)


def _matmul_res(a, w, res, tm=512, tn=512):
    M, K = a.shape
    N = w.shape[1]
    tm, tn = _tile(M, tm), _tile(N, tn)
    return pl.pallas_call(
        _matmul_res_kernel,
        grid=(M // tm, N // tn),
        in_specs=[pl.BlockSpec((tm, K), lambda i, j: (i, 0)),
                  pl.BlockSpec((K, tn), lambda i, j: (0, j)),
                  pl.BlockSpec((tm, tn), lambda i, j: (i, j))],
        out_specs=pl.BlockSpec((tm, tn), lambda i, j: (i, j)),
        out_shape=jax.ShapeDtypeStruct((M, N), F32),
        compiler_params=_params("parallel", "arbitrary"),
        name="matmul_res",
    )(a, w, res)


def _rmsnorm_kernel(x_ref, g_ref, o_ref):
    o_ref[...] = _normalize_rows(x_ref[...], g_ref[...])


def _rmsnorm(x, g, tm=512):
    M, D = x.shape
    tm = _tile(M, tm)
    return pl.pallas_call(
        _rmsnorm_kernel,
        grid=(M // tm,),
        in_specs=[pl.BlockSpec((tm, D), lambda i: (i, 0)),
                  pl.BlockSpec((1, D), lambda i: (0, 0))],
        out_specs=pl.BlockSpec((tm, D), lambda i: (i, 0)),
        out_shape=jax.ShapeDtypeStruct((M, D), F32),
        compiler_params=_params("parallel"),
        name="rmsnorm",
    )(x, g)


N_BIAS_COLS = 3
LOG2E = 1.4426950408889634


def _split3(x):
    hi = x.astype(BF16)
    rest = x - hi.astype(F32)
    mid = rest.astype(BF16)
    lo = (rest - mid.astype(F32)).astype(BF16)
    return jnp.concatenate([hi, mid, lo], axis=1)


def _forget_bias_kernel(f_ref, b_ref, tri_ref, sel_ref, o_ref, carry_ref):
    @pl.when(pl.program_id(1) == 0)
    def _():
        carry_ref[...] = jnp.zeros_like(carry_ref)

    x = _log_sigmoid(f_ref[...] + b_ref[...])
    cum3 = jnp.dot(tri_ref[...], _split3(x), preferred_element_type=F32)
    cum = (cum3[:, 0:LANES] + cum3[:, LANES:2 * LANES] + cum3[:, 2 * LANES:3 * LANES]
           + carry_ref[0:1, :])
    T = x.shape[0]
    carry_ref[0:1, :] = cum[T - 1:T, :]
    o_ref[...] = jnp.dot(_split3(cum * (-LOG2E)), sel_ref[...],
                         preferred_element_type=F32).astype(o_ref.dtype)


def _forget_bias(f, b, B, S, T=512):
    H = N_HEADS
    T = _tile(S, T)
    nt = S // T
    tri = jnp.tril(jnp.ones((T, T), BF16))
    r = jnp.arange(N_BIAS_COLS * LANES)[:, None]
    c = jnp.arange(H * LANES)[None, :]
    sel = ((r // LANES == c % LANES) & (r % LANES == c // LANES)).astype(BF16)
    return pl.pallas_call(
        _forget_bias_kernel,
        grid=(B, nt),
        in_specs=[pl.BlockSpec((T, LANES), lambda b_, t: (b_ * nt + t, 0)),
                  pl.BlockSpec((1, LANES), lambda b_, t: (0, 0)),
                  pl.BlockSpec((T, T), lambda b_, t: (0, 0)),
                  pl.BlockSpec(sel.shape, lambda b_, t: (0, 0))],
        out_specs=pl.BlockSpec((T, H * LANES), lambda b_, t: (b_ * nt + t, 0)),
        out_shape=jax.ShapeDtypeStruct((B * S, H * LANES), BF16),
        scratch_shapes=[pltpu.VMEM((8, LANES), F32)],
        compiler_params=_params("arbitrary", "arbitrary"),
        name="forget_bias",
    )(f, b, tri, sel)


SUM_ROWS = 16


def _attn_kernel(q_ref, k_ref, v_ref, kb_ref, o_ref, vt_ref, qt_ref, acc_ref, m_ref, sbuf_ref, *,
                 tq, tk, unroll, ahead):
    i = pl.program_id(2)
    nq = pl.num_programs(2)
    S = k_ref.shape[0]
    Dh = HEAD_DIM
    per_q = tq // tk
    assert ahead == per_q >= 2
    slot = i % 2

    def load_queries(tile, dst):
        rows = pl.multiple_of(tile * tq, tq)
        qt_ref[dst, 0:Dh, :] = q_ref[pl.ds(rows, tq), :].T

    def scores(j, qslot):
        start = pl.multiple_of(j * tk, tk)
        k_aug = jnp.concatenate([k_ref[pl.ds(start, tk), :], kb_ref[pl.ds(start, tk), :]], axis=1)
        return jnp.dot(k_aug, qt_ref[qslot], preferred_element_type=F32)

    @pl.when(i == 0)
    def _():
        for c in range(S // tq):
            vt_ref[0:Dh, c * tq:(c + 1) * tq] = v_ref[c * tq:(c + 1) * tq, :].T
        vt_ref[Dh:Dh + SUM_ROWS, :] = jnp.ones((SUM_ROWS, S), BF16)
        row = lax.broadcasted_iota(jnp.int32, (Dh, tq), 0)
        for dst in range(2):
            qt_ref[dst, Dh:2 * Dh, :] = jnp.where(row < N_BIAS_COLS, 1.0, 0.0).astype(BF16)
        load_queries(0, 0)
        for a in range(ahead):
            sbuf_ref[a] = scores(a, 0)

    load_queries(jnp.minimum(i + 1, nq - 1), 1 - slot)

    m_ref[...] = jnp.full_like(m_ref, -jnp.inf)
    acc_ref[...] = jnp.zeros_like(acc_ref)

    def run(base, count, diag=False):
        pending = {a: sbuf_ref[a] for a in range(ahead)}

        def max_pass(c, m_in):
            s = pending.pop(c)
            if diag:
                key = lax.broadcasted_iota(jnp.int32, (tk, tq), 0) + c * tk
                qry = lax.broadcasted_iota(jnp.int32, (tk, tq), 1)
                s = jnp.where(key <= qry, s, -jnp.inf)
            return s, m_in, jnp.maximum(m_in, jnp.max(s, axis=0, keepdims=True))

        cur = max_pass(0, m_ref[0:1, :])
        for c in range(count):
            if diag:
                pending[c + ahead] = scores(c, 1 - slot)
            else:
                pending[c + ahead] = scores(base + c + ahead, slot)
            nxt = max_pass(c + 1, cur[2]) if c + 1 < count else None
            s, m_old, m_new = cur
            p = jnp.exp2(s - m_new).astype(BF16)
            start = pl.multiple_of((base + c) * tk, tk)
            pv = jnp.dot(vt_ref[:, pl.ds(start, tk)], p, preferred_element_type=F32)
            acc_ref[...] = jnp.exp2(m_old - m_new) * acc_ref[...] + pv
            cur = nxt
        m_ref[0:1, :] = m_new
        for t in sorted(pending):
            sbuf_ref[t - count] = pending.pop(t)

    n_full = i * per_q
    n_groups = n_full // unroll

    def group(g, carry):
        run(g * unroll, unroll)
        return carry

    lax.fori_loop(0, n_groups, group, 0)

    rem = n_full - n_groups * unroll
    size = unroll // 2
    while size >= per_q:
        @pl.when((rem & size) != 0)
        def _(size=size):
            run(n_groups * unroll + (rem // (2 * size)) * (2 * size), size)
        size //= 2

    run(n_full, per_q, diag=True)
    acc = acc_ref[...]
    o_ref[...] = (acc[0:Dh, :] / acc[Dh:Dh + 1, :]).T.astype(o_ref.dtype)


def _attention(qkv, kbias, B, S, tq=512, tk=256, unroll=16, ahead=2):
    H, Dh = N_HEADS, HEAD_DIM
    tq = _tile(S, tq)
    tk = _tile(tq, tk)
    unroll = max(unroll, tq // tk)
    nq = S // tq
    return pl.pallas_call(
        functools.partial(_attn_kernel, tq=tq, tk=tk, unroll=unroll, ahead=ahead),
        grid=(B, H, nq),
        in_specs=[pl.BlockSpec((S, Dh), lambda b, h, i: (b, h)),
                  pl.BlockSpec((S, Dh), lambda b, h, i: (b, H + h)),
                  pl.BlockSpec((S, Dh), lambda b, h, i: (b, 2 * H + h)),
                  pl.BlockSpec((S, LANES), lambda b, h, i: (b, h))],
        out_specs=pl.BlockSpec((tq, Dh), lambda b, h, i: (b * nq + i, h)),
        out_shape=jax.ShapeDtypeStruct((B * S, H * Dh), BF16),
        scratch_shapes=[pltpu.VMEM((Dh + SUM_ROWS, S), BF16), pltpu.VMEM((2, 2 * Dh, tq), BF16),
                        pltpu.VMEM((Dh + SUM_ROWS, tq), F32), pltpu.VMEM((8, tq), F32),
                        pltpu.VMEM((ahead, tk, tq), F32)],
        compiler_params=_params("arbitrary", "arbitrary", "arbitrary"),
        name="fox_attention",
    )(qkv, qkv, qkv, kbias)


def _gelu_tanh(x):
    c = 0.7978845608028654
    return 0.5 * x * (1.0 + jnp.tanh(c * (x + 0.044715 * (x * x * x))))


def _rglru_kernel(x_ref, y_ref, cw_ref, cb_ref, wr_ref, br_ref, wi_ref, bi_ref, lam_ref,
                  o_ref, ext_ref, a_ref, b_ref, h_ref, state_ref, *, T):
    C = x_ref.shape[1]
    cb = C // N_RNN_BLOCKS

    @pl.when(pl.program_id(1) == 0)
    def _():
        ext_ref[0:CONV_HALO, :] = jnp.zeros((CONV_HALO, C), F32)
        state_ref[...] = jnp.zeros_like(state_ref)

    x = x_ref[...]
    ext_ref[CONV_HALO:CONV_HALO + T, :] = x
    u = cb_ref[...] + x * cw_ref[CONV_WIDTH - 1:CONV_WIDTH, :]
    for k in range(CONV_WIDTH - 1):
        off = CONV_HALO + k - (CONV_WIDTH - 1)
        u = u + ext_ref[off:off + T, :] * cw_ref[k:k + 1, :]
    ext_ref[0:CONV_HALO, :] = x[T - CONV_HALO:T, :]

    ub = u.astype(BF16)
    r_parts, i_parts = [], []
    for n in range(N_RNN_BLOCKS):
        un = ub[:, n * cb:(n + 1) * cb]
        r_parts.append(jnp.dot(un, wr_ref[n], preferred_element_type=F32))
        i_parts.append(jnp.dot(un, wi_ref[n], preferred_element_type=F32))
    r = _sigmoid(jnp.concatenate(r_parts, axis=1) + br_ref[...])
    ig = _sigmoid(jnp.concatenate(i_parts, axis=1) + bi_ref[...])
    log_a = (LRU_C * r) * _log_sigmoid(lam_ref[...])
    a = jnp.exp(log_a)
    a_ref[...] = a
    b_ref[...] = jnp.sqrt(-jnp.tanh(log_a) * (a * a + 1.0)) * (ig * u)

    def group(g, h):
        base = pl.multiple_of(g * 8, 8)
        a8 = a_ref[pl.ds(base, 8), :]
        b8 = b_ref[pl.ds(base, 8), :]
        rows = []
        for s in range(8):
            h = a8[s:s + 1, :] * h + b8[s:s + 1, :]
            rows.append(h)
        h_ref[pl.ds(base, 8), :] = jnp.concatenate(rows, axis=0)
        return h

    h_last = lax.fori_loop(0, T // 8, group, state_ref[0:1, :])
    state_ref[0:1, :] = h_last
    o_ref[...] = (_gelu_tanh(y_ref[...]) * h_ref[...]).astype(o_ref.dtype)


def _rglru(xy, conv_w, conv_b, w_rg, b_rg, w_ig, b_ig, lam, B, S, T=512):
    C = xy.shape[1] // 2
    T = _tile(S, T)
    nt = S // T
    vec = pl.BlockSpec((1, C), lambda b, t: (0, 0))
    blockdiag = pl.BlockSpec(w_rg.shape, lambda b, t: (0, 0, 0))
    return pl.pallas_call(
        functools.partial(_rglru_kernel, T=T),
        grid=(B, nt),
        in_specs=[pl.BlockSpec((T, C), lambda b, t: (b * nt + t, 0)),
                  pl.BlockSpec((T, C), lambda b, t: (b * nt + t, 1)),
                  pl.BlockSpec((CONV_WIDTH, C), lambda b, t: (0, 0)),
                  vec, blockdiag, vec, blockdiag, vec, vec],
        out_specs=pl.BlockSpec((T, C), lambda b, t: (b * nt + t, 0)),
        out_shape=jax.ShapeDtypeStruct((B * S, C), BF16),
        scratch_shapes=[pltpu.VMEM((T + CONV_HALO, C), F32),
                        pltpu.VMEM((T, C), F32), pltpu.VMEM((T, C), F32), pltpu.VMEM((T, C), F32),
                        pltpu.VMEM((8, C), F32)],
        compiler_params=_params("arbitrary", "arbitrary"),
        name="rglru",
    )(xy, xy, conv_w, conv_b, w_rg, b_rg, w_ig, b_ig, lam)


def _pool_kernel(u_ref, w_ref, s_ref, o_ref, ext_ref, *, T):
    C = u_ref.shape[1]
    G = len(POOL_WINDOWS)
    cg = C // G
    t = pl.program_id(1)

    @pl.when(t == 0)
    def _():
        ext_ref[0:POOL_HALO, :] = jnp.zeros((POOL_HALO, C), F32)

    x = u_ref[...]
    ext_ref[POOL_HALO:POOL_HALO + T, :] = x
    pos = (t * T + 1 + lax.broadcasted_iota(jnp.int32, (T, 1), 0)).astype(F32)
    for g, w in enumerate(POOL_WINDOWS):
        lo = g * cg
        xs = x[:, lo:lo + cg]
        acc = xs
        for k in range(1, w):
            acc = acc + ext_ref[POOL_HALO - k:POOL_HALO - k + T, lo:lo + cg]
        pooled = acc / jnp.minimum(pos, float(w)) - xs
        y = jnp.dot(pooled.astype(BF16), w_ref[g], preferred_element_type=F32)
        o_ref[:, lo:lo + cg] = (y * s_ref[:, lo:lo + cg]).astype(o_ref.dtype)
    ext_ref[0:POOL_HALO, :] = x[T - POOL_HALO:T, :]


def _pool(u, w_pool, scale, B, S, T=512):
    C = u.shape[1]
    T = _tile(S, T)
    nt = S // T
    return pl.pallas_call(
        functools.partial(_pool_kernel, T=T),
        grid=(B, nt),
        in_specs=[pl.BlockSpec((T, C), lambda b, t: (b * nt + t, 0)),
                  pl.BlockSpec(w_pool.shape, lambda b, t: (0, 0, 0)),
                  pl.BlockSpec((1, C), lambda b, t: (0, 0))],
        out_specs=pl.BlockSpec((T, C), lambda b, t: (b * nt + t, 0)),
        out_shape=jax.ShapeDtypeStruct((B * S, C), BF16),
        scratch_shapes=[pltpu.VMEM((T + POOL_HALO, C), F32)],
        compiler_params=_params("arbitrary", "arbitrary"),
        name="multiscale_pool",
    )(u, w_pool, scale)


def _merge_kernel(ya_ref, yb_ref, yc_ref, wa_ref, wb_ref, wc_ref, ga_ref, gb_ref, gc_ref, o_ref):
    out = _sigmoid(ga_ref[...]) * jnp.dot(ya_ref[...], wa_ref[...], preferred_element_type=F32)
    out = out + _sigmoid(gb_ref[...]) * jnp.dot(yb_ref[...], wb_ref[...], preferred_element_type=F32)
    out = out + _sigmoid(gc_ref[...]) * jnp.dot(yc_ref[...], wc_ref[...], preferred_element_type=F32)
    o_ref[...] = out.astype(o_ref.dtype)


def _merge(ya, yb, yc, wa, wb, wc, gates, tm=1024, tn=512):
    M, C = ya.shape
    D = wa.shape[1]
    tm, tn = _tile(M, tm), _tile(D, tn)
    nd = D // tn
    y_spec = pl.BlockSpec((tm, C), lambda i, j: (i, 0))
    w_spec = pl.BlockSpec((C, tn), lambda i, j: (0, j))
    return pl.pallas_call(
        _merge_kernel,
        grid=(M // tm, nd),
        in_specs=[y_spec, y_spec, y_spec, w_spec, w_spec, w_spec,
                  pl.BlockSpec((tm, tn), lambda i, j: (i, j)),
                  pl.BlockSpec((tm, tn), lambda i, j: (i, j + nd)),
                  pl.BlockSpec((tm, tn), lambda i, j: (i, j + 2 * nd))],
        out_specs=pl.BlockSpec((tm, tn), lambda i, j: (i, j)),
        out_shape=jax.ShapeDtypeStruct((M, D), BF16),
        compiler_params=_params("parallel", "arbitrary"),
        name="gated_merge",
    )(ya, yb, yc, wa, wb, wc, gates, gates, gates)


def _layer(x, B, S, g_mix, w_in, b_forget, conv_w, conv_b, w_rg, b_rg, w_ig, b_ig, lam,
           w_pool, pool_scale, w_a, w_b, w_c, w_out, g_ffn, w_ffn_in, w_ffn_out):
    D = x.shape[1]
    H = N_HEADS
    d_rnn = conv_w.shape[1]
    d_attn = H * HEAD_DIM
    d_pool = pool_scale.shape[0]
    row = lambda v: v.reshape(1, -1)

    o_rnn = 2 * d_rnn
    o_qkv = o_rnn + 3 * d_attn
    o_f = o_qkv + H
    o_pool = o_f + d_pool
    w_f32 = jnp.concatenate([w_in[:, :o_rnn], w_in[:, o_f:]], axis=1).astype(BF16)
    w_forget = jnp.pad(w_in[:, o_qkv:o_f], ((0, 0), (0, LANES - H))).astype(BF16)
    qkv_scale = jnp.ones((1, 3 * d_attn), F32).at[:, :d_attn].set(HEAD_DIM ** -0.5 * LOG2E)
    qkv, h = _norm_matmul(x, row(g_mix), w_in[:, o_rnn:o_qkv].astype(BF16), qkv_scale, BF16)
    xy, pool_in, gates, f_logit = _in_proj(h, w_f32, w_forget, (o_rnn, d_pool, 3 * D))

    y_a = _rglru(xy, conv_w, row(conv_b), w_rg.astype(BF16), row(b_rg), w_ig.astype(BF16),
                 row(b_ig), row(lam), B, S)

    kbias = _forget_bias(f_logit, jnp.pad(b_forget, (0, LANES - H)).reshape(1, LANES), B, S)
    y_b = _attention(qkv, kbias, B, S)

    y_c = _pool(pool_in, w_pool.astype(BF16), row(pool_scale), B, S)

    merged = _merge(y_a, y_b, y_c, w_a.astype(BF16), w_b.astype(BF16), w_c.astype(BF16), gates,
                    tm=256, tn=D)
    x = _matmul_res(merged, w_out.astype(BF16), x, tm=512, tn=D)
    act = _norm_swiglu(x, row(g_ffn), w_ffn_in.astype(BF16))
    return _matmul_res(act, w_ffn_out.astype(BF16), x, tm=1024, tn=512)


def kernel(x, g_mix, w_in, b_forget, conv_w, conv_b, w_rg, b_rg, w_ig, b_ig, lru_lambda, w_pool,
           pool_scale, w_branch_rnn, w_branch_attn, w_branch_pool, w_out, g_ffn, w_ffn_in,
           w_ffn_out, g_final):
    B, S, D = x.shape
    h = x.reshape(B * S, D)
    for l in range(g_mix.shape[0]):
        h = _layer(h, B, S, g_mix[l], w_in[l], b_forget[l], conv_w[l], conv_b[l], w_rg[l], b_rg[l],
                   w_ig[l], b_ig[l], lru_lambda[l], w_pool[l], pool_scale[l], w_branch_rnn[l],
                   w_branch_attn[l], w_branch_pool[l], w_out[l], g_ffn[l], w_ffn_in[l], w_ffn_out[l])
    return _rmsnorm(h, g_final.reshape(1, -1)).reshape(B, S, D)
```

```python
import functools

import jax
import jax.numpy as jnp
from jax import lax
from jax.experimental import pallas as pl
from jax.experimental.pallas import tpu as pltpu

F32 = jnp.float32
BF16 = jnp.bfloat16

N_RNN_BLOCKS = 8
CONV_WIDTH = 4
LRU_C = 8.0
N_HEADS = 8
HEAD_DIM = 128
POOL_WINDOWS = (2, 4, 8, 16)
POOL_HALO = 16
CONV_HALO = 8
NORM_EPS = 1e-6
LANES = 128

VMEM_LIMIT = 56 * 1024 * 1024


def _params(*sem):
    return pltpu.CompilerParams(dimension_semantics=sem, vmem_limit_bytes=VMEM_LIMIT)


def _tile(n, want):
    t = min(n, want)
    while n % t:
        t //= 2
    return t


def _sigmoid(x):
    return 0.5 * jnp.tanh(0.5 * x) + 0.5


def _log_sigmoid(x):
    return jnp.minimum(x, 0.0) - jnp.log1p(jnp.exp(-jnp.abs(x)))


def _normalize_rows(x, g):
    ms = jnp.mean(x * x, axis=-1, keepdims=True)
    return x * lax.rsqrt(ms + NORM_EPS) * g


def _norm_matmul_kernel(x_ref, g_ref, w_ref, c_ref, o_ref, h_ref):
    @pl.when(pl.program_id(1) == 0)
    def _():
        h_ref[...] = _normalize_rows(x_ref[...], g_ref[...]).astype(BF16)

    acc = jnp.dot(h_ref[...], w_ref[...], preferred_element_type=F32)
    o_ref[...] = (acc * c_ref[...]).astype(o_ref.dtype)


def _norm_matmul(x, g, w, colscale, out_dtype, tm=1024, tn=1024):
    M, D = x.shape
    N = w.shape[1]
    tm, tn = _tile(M, tm), _tile(N, tn)
    return pl.pallas_call(
        _norm_matmul_kernel,
        grid=(M // tm, N // tn),
        in_specs=[pl.BlockSpec((tm, D), lambda i, j: (i, 0)),
                  pl.BlockSpec((1, D), lambda i, j: (0, 0)),
                  pl.BlockSpec((D, tn), lambda i, j: (0, j)),
                  pl.BlockSpec((1, tn), lambda i, j: (0, j))],
        out_specs=[pl.BlockSpec((tm, tn), lambda i, j: (i, j)),
                   pl.BlockSpec((tm, D), lambda i, j: (i, 0))],
        out_shape=[jax.ShapeDtypeStruct((M, N), out_dtype), jax.ShapeDtypeStruct((M, D), BF16)],
        compiler_params=_params("arbitrary", "arbitrary"),
        name="norm_matmul",
    )(x, g, w, colscale)


def _in_proj_kernel(h_ref, w_ref, wf_ref, *rest, bounds):
    outs, f_ref = rest[:-1], rest[-1]
    j = pl.program_id(1)

    @pl.when(j == 0)
    def _():
        f_ref[...] = jnp.dot(h_ref[...], wf_ref[...], preferred_element_type=F32)

    acc = jnp.dot(h_ref[...], w_ref[...], preferred_element_type=F32)
    tm, tn = acc.shape
    rows = min(tm, tn)
    for k, o_ref in enumerate(outs):
        owns = jnp.broadcast_to((j >= bounds[k]) & (j < bounds[k + 1]), (rows, tn))
        for r in range(0, tm, rows):
            pltpu.store(o_ref.at[r:r + rows, :], acc[r:r + rows, :], mask=owns)


def _in_proj(h, w, w_forget, widths, tm=1024, tn=1024):
    M, D = h.shape
    N = w.shape[1]
    tm = _tile(M, tm)
    assert all(wd % tn == 0 for wd in widths) and sum(widths) == N
    bounds = [0]
    for wd in widths:
        bounds.append(bounds[-1] + wd // tn)

    def out_map(k):
        lo, n = bounds[k], bounds[k + 1] - bounds[k]
        return lambda i, j: (i, jnp.clip(j - lo, 0, n - 1))

    out_specs = [pl.BlockSpec((tm, tn), out_map(k)) for k in range(len(widths))]
    out_shape = [jax.ShapeDtypeStruct((M, wd), F32) for wd in widths]
    out_specs.append(pl.BlockSpec((tm, LANES), lambda i, j: (i, 0)))
    out_shape.append(jax.ShapeDtypeStruct((M, LANES), F32))
    return pl.pallas_call(
        functools.partial(_in_proj_kernel, bounds=tuple(bounds)),
        grid=(M // tm, N // tn),
        in_specs=[pl.BlockSpec((tm, D), lambda i, j: (i, 0)),
                  pl.BlockSpec((D, tn), lambda i, j: (0, j)),
                  pl.BlockSpec((D, LANES), lambda i, j: (0, 0))],
        out_specs=out_specs,
        out_shape=out_shape,
        compiler_params=_params("arbitrary", "arbitrary"),
        name="in_proj",
    )(h, w, w_forget)


def _norm_swiglu_kernel(x_ref, g_ref, wg_ref, wu_ref, o_ref, h_ref):
    @pl.when(pl.program_id(1) == 0)
    def _():
        h_ref[...] = _normalize_rows(x_ref[...], g_ref[...]).astype(BF16)

    h = h_ref[...]
    gate = jnp.dot(h, wg_ref[...], preferred_element_type=F32)
    up = jnp.dot(h, wu_ref[...], preferred_element_type=F32)
    o_ref[...] = (gate * _sigmoid(gate) * up).astype(o_ref.dtype)


def _norm_swiglu(x, g, w, tm=1024, tn=512):
    M, D = x.shape
    F = w.shape[1] // 2
    tm, tn = _tile(M, tm), _tile(F, tn)
    nf = F // tn
    return pl.pallas_call(
        _norm_swiglu_kernel,
        grid=(M // tm, nf),
        in_specs=[pl.BlockSpec((tm, D), lambda i, j: (i, 0)),
                  pl.BlockSpec((1, D), lambda i, j: (0, 0)),
                  pl.BlockSpec((D, tn), lambda i, j: (0, j)),
                  pl.BlockSpec((D, tn), lambda i, j: (0, j + nf))],
        out_specs=pl.BlockSpec((tm, tn), lambda i, j: (i, j)),
        out_shape=jax.ShapeDtypeStruct((M, F), BF16),
        scratch_shapes=[pltpu.VMEM((tm, D), BF16)],
        compiler_params=_params("parallel", "arbitrary"),
        name="norm_swiglu",
    )(x, g, w, w)


def _matmul_res_kernel(a_ref, w_ref, r_ref, *rest):
    o_ref = rest[-1]
    out = r_ref[...] + jnp.dot(a_ref[...], w_ref[...], preferred_element_type=F32)
    if len(rest) == 2:
        out = _normalize_rows(out, rest[0][...])
    o_ref[...] = out


def _matmul_res(a, w, res, tm=512, tn=512, norm_gain=None):
    M, K = a.shape
    N = w.shape[1]
    tm, tn = _tile(M, tm), _tile(N, tn)
    resident = tn == N
    assert norm_gain is None or resident
    in_specs = [pl.BlockSpec((tm, K), lambda i, j: (i, 0)),
                pl.BlockSpec((K, tn), lambda i, j: (0, j),
                             pipeline_mode=pl.Buffered(1) if resident else None),
                pl.BlockSpec((tm, tn), lambda i, j: (i, j))]
    args = [a, w, res]
    if norm_gain is not None:
        in_specs.append(pl.BlockSpec((1, N), lambda i, j: (0, 0)))
        args.append(norm_gain)
    return pl.pallas_call(
        _matmul_res_kernel,
        grid=(M // tm, N // tn),
        in_specs=in_specs,
        out_specs=pl.BlockSpec((tm, tn), lambda i, j: (i, j)),
        out_shape=jax.ShapeDtypeStruct((M, N), F32),
        compiler_params=_params("parallel", "arbitrary"),
        name="matmul_res",
    )(*args)


N_BIAS_COLS = 3
LOG2E = 1.4426950408889634


def _split3(x):
    hi = x.astype(BF16)
    rest = x - hi.astype(F32)
    mid = rest.astype(BF16)
    lo = (rest - mid.astype(F32)).astype(BF16)
    return jnp.concatenate([hi, mid, lo], axis=1)


def _forget_bias_kernel(f_ref, b_ref, tri_ref, sel_ref, o_ref, carry_ref):
    @pl.when(pl.program_id(1) == 0)
    def _():
        carry_ref[...] = jnp.zeros_like(carry_ref)

    x = _log_sigmoid(f_ref[...] + b_ref[...])
    cum3 = jnp.dot(tri_ref[...], _split3(x), preferred_element_type=F32)
    cum = (cum3[:, 0:LANES] + cum3[:, LANES:2 * LANES] + cum3[:, 2 * LANES:3 * LANES]
           + carry_ref[0:1, :])
    T = x.shape[0]
    carry_ref[0:1, :] = cum[T - 1:T, :]
    o_ref[...] = jnp.dot(_split3(cum * (-LOG2E)), sel_ref[...],
                         preferred_element_type=F32).astype(o_ref.dtype)


def _forget_bias(f, b, B, S, T=512):
    H = N_HEADS
    T = _tile(S, T)
    nt = S // T
    tri = jnp.tril(jnp.ones((T, T), BF16))
    r = jnp.arange(N_BIAS_COLS * LANES)[:, None]
    c = jnp.arange(H * LANES)[None, :]
    sel = ((r // LANES == c % LANES) & (r % LANES == c // LANES)).astype(BF16)
    return pl.pallas_call(
        _forget_bias_kernel,
        grid=(B, nt),
        in_specs=[pl.BlockSpec((T, LANES), lambda b_, t: (b_ * nt + t, 0)),
                  pl.BlockSpec((1, LANES), lambda b_, t: (0, 0)),
                  pl.BlockSpec((T, T), lambda b_, t: (0, 0)),
                  pl.BlockSpec(sel.shape, lambda b_, t: (0, 0))],
        out_specs=pl.BlockSpec((T, H * LANES), lambda b_, t: (b_ * nt + t, 0)),
        out_shape=jax.ShapeDtypeStruct((B * S, H * LANES), BF16),
        scratch_shapes=[pltpu.VMEM((8, LANES), F32)],
        compiler_params=_params("arbitrary", "arbitrary"),
        name="forget_bias",
    )(f, b, tri, sel)


SUM_ROWS = 16


def _attn_kernel(q_ref, k_ref, v_ref, kb_ref, o_ref, vt_ref, qt_ref, acc_ref, m_ref, sbuf_ref, *,
                 tq, tk, unroll, ahead):
    i = pl.program_id(2)
    nq = pl.num_programs(2)
    S = k_ref.shape[0]
    Dh = HEAD_DIM
    per_q = tq // tk
    assert ahead == per_q >= 2
    slot = i % 2

    def load_queries(tile, dst):
        rows = pl.multiple_of(tile * tq, tq)
        qt_ref[dst, 0:Dh, :] = q_ref[pl.ds(rows, tq), :].T

    def scores(j, qslot):
        start = pl.multiple_of(j * tk, tk)
        k_aug = jnp.concatenate([k_ref[pl.ds(start, tk), :], kb_ref[pl.ds(start, tk), :]], axis=1)
        return jnp.dot(k_aug, qt_ref[qslot], preferred_element_type=F32)

    @pl.when(i == 0)
    def _():
        for c in range(S // tq):
            vt_ref[0:Dh, c * tq:(c + 1) * tq] = v_ref[c * tq:(c + 1) * tq, :].T
        vt_ref[Dh:Dh + SUM_ROWS, :] = jnp.ones((SUM_ROWS, S), BF16)
        row = lax.broadcasted_iota(jnp.int32, (Dh, tq), 0)
        for dst in range(2):
            qt_ref[dst, Dh:2 * Dh, :] = jnp.where(row < N_BIAS_COLS, 1.0, 0.0).astype(BF16)
        load_queries(0, 0)
        for a in range(ahead):
            sbuf_ref[a] = scores(a, 0)

    load_queries(jnp.minimum(i + 1, nq - 1), 1 - slot)

    m_ref[...] = jnp.full_like(m_ref, -jnp.inf)
    acc_ref[...] = jnp.zeros_like(acc_ref)

    def run(base, count, diag=False):
        pending = {a: sbuf_ref[a] for a in range(ahead)}

        def max_pass(c, m_in):
            s = pending.pop(c)
            if diag:
                key = lax.broadcasted_iota(jnp.int32, (tk, tq), 0) + c * tk
                qry = lax.broadcasted_iota(jnp.int32, (tk, tq), 1)
                s = jnp.where(key <= qry, s, -jnp.inf)
            return s, m_in, jnp.maximum(m_in, jnp.max(s, axis=0, keepdims=True))

        cur = max_pass(0, m_ref[0:1, :])
        for c in range(count):
            if diag:
                pending[c + ahead] = scores(c, 1 - slot)
            else:
                pending[c + ahead] = scores(base + c + ahead, slot)
            nxt = max_pass(c + 1, cur[2]) if c + 1 < count else None
            s, m_old, m_new = cur
            p = jnp.exp2(s - m_new).astype(BF16)
            start = pl.multiple_of((base + c) * tk, tk)
            pv = jnp.dot(vt_ref[:, pl.ds(start, tk)], p, preferred_element_type=F32)
            acc_ref[...] = jnp.exp2(m_old - m_new) * acc_ref[...] + pv
            cur = nxt
        m_ref[0:1, :] = m_new
        for t in sorted(pending):
            sbuf_ref[t - count] = pending.pop(t)

    n_full = i * per_q
    n_groups = n_full // unroll

    def group(g, carry):
        run(g * unroll, unroll)
        return carry

    lax.fori_loop(0, n_groups, group, 0)

    rem = n_full - n_groups * unroll
    size = unroll // 2
    while size >= per_q:
        @pl.when((rem & size) != 0)
        def _(size=size):
            run(n_groups * unroll + (rem // (2 * size)) * (2 * size), size)
        size //= 2

    run(n_full, per_q, diag=True)
    acc = acc_ref[...]
    o_ref[...] = (acc[0:Dh, :] / acc[Dh:Dh + 1, :]).T.astype(o_ref.dtype)


def _attention(qkv, kbias, B, S, tq=512, tk=256, unroll=16, ahead=2):
    H, Dh = N_HEADS, HEAD_DIM
    tq = _tile(S, tq)
    tk = _tile(tq, tk)
    unroll = max(unroll, tq // tk)
    nq = S // tq
    return pl.pallas_call(
        functools.partial(_attn_kernel, tq=tq, tk=tk, unroll=unroll, ahead=ahead),
        grid=(B, H, nq),
        in_specs=[pl.BlockSpec((S, Dh), lambda b, h, i: (b, h)),
                  pl.BlockSpec((S, Dh), lambda b, h, i: (b, H + h)),
                  pl.BlockSpec((S, Dh), lambda b, h, i: (b, 2 * H + h)),
                  pl.BlockSpec((S, LANES), lambda b, h, i: (b, h))],
        out_specs=pl.BlockSpec((tq, Dh), lambda b, h, i: (b * nq + i, h)),
        out_shape=jax.ShapeDtypeStruct((B * S, H * Dh), BF16),
        scratch_shapes=[pltpu.VMEM((Dh + SUM_ROWS, S), BF16), pltpu.VMEM((2, 2 * Dh, tq), BF16),
                        pltpu.VMEM((Dh + SUM_ROWS, tq), F32), pltpu.VMEM((8, tq), F32),
                        pltpu.VMEM((ahead, tk, tq), F32)],
        compiler_params=_params("arbitrary", "arbitrary", "arbitrary"),
        name="fox_attention",
    )(qkv, qkv, qkv, kbias)


def _gelu_tanh(x):
    c = 0.7978845608028654
    return 0.5 * x * (1.0 + jnp.tanh(c * (x + 0.044715 * (x * x * x))))


def _rglru_kernel(x_ref, y_ref, cw_ref, cb_ref, wr_ref, br_ref, wi_ref, bi_ref, lam_ref,
                  o_ref, ext_ref, a_ref, b_ref, h_ref, state_ref, *, T):
    nb, _, C = x_ref.shape
    cb = C // N_RNN_BLOCKS

    @pl.when(pl.program_id(0) == 0)
    def _():
        ext_ref[:, 0:CONV_HALO, :] = jnp.zeros((nb, CONV_HALO, C), F32)
        state_ref[...] = jnp.zeros_like(state_ref)

    log_sig_lam = _log_sigmoid(lam_ref[...])
    for n_b in range(nb):
        x = x_ref[n_b]
        ext_ref[n_b, CONV_HALO:CONV_HALO + T, :] = x
        u = cb_ref[...] + x * cw_ref[CONV_WIDTH - 1:CONV_WIDTH, :]
        for k in range(CONV_WIDTH - 1):
            off = CONV_HALO + k - (CONV_WIDTH - 1)
            u = u + ext_ref[n_b, off:off + T, :] * cw_ref[k:k + 1, :]
        ext_ref[n_b, 0:CONV_HALO, :] = x[T - CONV_HALO:T, :]

        ub = u.astype(BF16)
        r_parts, i_parts = [], []
        for n in range(N_RNN_BLOCKS):
            un = ub[:, n * cb:(n + 1) * cb]
            r_parts.append(jnp.dot(un, wr_ref[n], preferred_element_type=F32))
            i_parts.append(jnp.dot(un, wi_ref[n], preferred_element_type=F32))
        r = _sigmoid(jnp.concatenate(r_parts, axis=1) + br_ref[...])
        ig = _sigmoid(jnp.concatenate(i_parts, axis=1) + bi_ref[...])
        log_a = (LRU_C * r) * log_sig_lam
        a = jnp.exp(log_a)
        a_ref[n_b] = a
        b_ref[n_b] = jnp.sqrt(-jnp.tanh(log_a) * (a * a + 1.0)) * (ig * u)

    def group(g, hs):
        base = pl.multiple_of(g * 8, 8)
        a8 = [a_ref[n_b, pl.ds(base, 8), :] for n_b in range(nb)]
        b8 = [b_ref[n_b, pl.ds(base, 8), :] for n_b in range(nb)]
        hs = list(hs)
        rows = [[] for _ in range(nb)]
        for s in range(8):
            for n_b in range(nb):
                hs[n_b] = a8[n_b][s:s + 1, :] * hs[n_b] + b8[n_b][s:s + 1, :]
                rows[n_b].append(hs[n_b])
        for n_b in range(nb):
            h_ref[n_b, pl.ds(base, 8), :] = jnp.concatenate(rows[n_b], axis=0)
        return tuple(hs)

    h_last = lax.fori_loop(0, T // 8, group, tuple(state_ref[n_b, 0:1, :] for n_b in range(nb)))
    for n_b in range(nb):
        state_ref[n_b, 0:1, :] = h_last[n_b]
        o_ref[n_b] = (_gelu_tanh(y_ref[n_b]) * h_ref[n_b]).astype(o_ref.dtype)


def _rglru(xy, conv_w, conv_b, w_rg, b_rg, w_ig, b_ig, lam, B, S, T=512):
    C = xy.shape[1] // 2
    T = _tile(S, T)
    xy = xy.reshape(B, S, 2 * C)
    vec = pl.BlockSpec((1, C), lambda t: (0, 0))
    blockdiag = pl.BlockSpec(w_rg.shape, lambda t: (0, 0, 0))
    out = pl.pallas_call(
        functools.partial(_rglru_kernel, T=T),
        grid=(S // T,),
        in_specs=[pl.BlockSpec((B, T, C), lambda t: (0, t, 0)),
                  pl.BlockSpec((B, T, C), lambda t: (0, t, 1)),
                  pl.BlockSpec((CONV_WIDTH, C), lambda t: (0, 0)),
                  vec, blockdiag, vec, blockdiag, vec, vec],
        out_specs=pl.BlockSpec((B, T, C), lambda t: (0, t, 0)),
        out_shape=jax.ShapeDtypeStruct((B, S, C), BF16),
        scratch_shapes=[pltpu.VMEM((B, T + CONV_HALO, C), F32),
                        pltpu.VMEM((B, T, C), F32), pltpu.VMEM((B, T, C), F32),
                        pltpu.VMEM((B, T, C), F32), pltpu.VMEM((B, 8, C), F32)],
        compiler_params=_params("arbitrary"),
        name="rglru",
    )(xy, xy, conv_w, conv_b, w_rg, b_rg, w_ig, b_ig, lam)
    return out.reshape(B * S, C)


def _pool_kernel(u_ref, w_ref, s_ref, o_ref, ext_ref, *, T):
    C = u_ref.shape[1]
    G = len(POOL_WINDOWS)
    cg = C // G
    t = pl.program_id(1)

    @pl.when(t == 0)
    def _():
        ext_ref[0:POOL_HALO, :] = jnp.zeros((POOL_HALO, C), F32)

    x = u_ref[...]
    ext_ref[POOL_HALO:POOL_HALO + T, :] = x
    pos = (t * T + 1 + lax.broadcasted_iota(jnp.int32, (T, 1), 0)).astype(F32)
    for g, w in enumerate(POOL_WINDOWS):
        lo = g * cg
        xs = x[:, lo:lo + cg]
        acc = xs
        for k in range(1, w):
            acc = acc + ext_ref[POOL_HALO - k:POOL_HALO - k + T, lo:lo + cg]
        pooled = acc / jnp.minimum(pos, float(w)) - xs
        y = jnp.dot(pooled.astype(BF16), w_ref[g], preferred_element_type=F32)
        o_ref[:, lo:lo + cg] = (y * s_ref[:, lo:lo + cg]).astype(o_ref.dtype)
    ext_ref[0:POOL_HALO, :] = x[T - POOL_HALO:T, :]


def _pool(u, w_pool, scale, B, S, T=512):
    C = u.shape[1]
    T = _tile(S, T)
    nt = S // T
    return pl.pallas_call(
        functools.partial(_pool_kernel, T=T),
        grid=(B, nt),
        in_specs=[pl.BlockSpec((T, C), lambda b, t: (b * nt + t, 0)),
                  pl.BlockSpec(w_pool.shape, lambda b, t: (0, 0, 0)),
                  pl.BlockSpec((1, C), lambda b, t: (0, 0))],
        out_specs=pl.BlockSpec((T, C), lambda b, t: (b * nt + t, 0)),
        out_shape=jax.ShapeDtypeStruct((B * S, C), BF16),
        scratch_shapes=[pltpu.VMEM((T + POOL_HALO, C), F32)],
        compiler_params=_params("arbitrary", "arbitrary"),
        name="multiscale_pool",
    )(u, w_pool, scale)


def _merge_kernel(ya_ref, yb_ref, yc_ref, wa_ref, wb_ref, wc_ref, ga_ref, gb_ref, gc_ref, o_ref):
    out = _sigmoid(ga_ref[...]) * jnp.dot(ya_ref[...], wa_ref[...], preferred_element_type=F32)
    out = out + _sigmoid(gb_ref[...]) * jnp.dot(yb_ref[...], wb_ref[...], preferred_element_type=F32)
    out = out + _sigmoid(gc_ref[...]) * jnp.dot(yc_ref[...], wc_ref[...], preferred_element_type=F32)
    o_ref[...] = out.astype(o_ref.dtype)


def _merge(ya, yb, yc, wa, wb, wc, gates, tm=1024, tn=512):
    M, C = ya.shape
    D = wa.shape[1]
    tm, tn = _tile(M, tm), _tile(D, tn)
    nd = D // tn
    y_spec = pl.BlockSpec((tm, C), lambda i, j: (i, 0))
    w_spec = pl.BlockSpec((C, tn), lambda i, j: (0, j))
    return pl.pallas_call(
        _merge_kernel,
        grid=(M // tm, nd),
        in_specs=[y_spec, y_spec, y_spec, w_spec, w_spec, w_spec,
                  pl.BlockSpec((tm, tn), lambda i, j: (i, j)),
                  pl.BlockSpec((tm, tn), lambda i, j: (i, j + nd)),
                  pl.BlockSpec((tm, tn), lambda i, j: (i, j + 2 * nd))],
        out_specs=pl.BlockSpec((tm, tn), lambda i, j: (i, j)),
        out_shape=jax.ShapeDtypeStruct((M, D), BF16),
        compiler_params=_params("parallel", "arbitrary"),
        name="gated_merge",
    )(ya, yb, yc, wa, wb, wc, gates, gates, gates)


def _layer(x, B, S, g_mix, w_in, b_forget, conv_w, conv_b, w_rg, b_rg, w_ig, b_ig, lam,
           w_pool, pool_scale, w_a, w_b, w_c, w_out, g_ffn, w_ffn_in, w_ffn_out, g_out=None):
    D = x.shape[1]
    H = N_HEADS
    d_rnn = conv_w.shape[1]
    d_attn = H * HEAD_DIM
    d_pool = pool_scale.shape[0]
    row = lambda v: v.reshape(1, -1)

    o_rnn = 2 * d_rnn
    o_qkv = o_rnn + 3 * d_attn
    o_f = o_qkv + H
    o_pool = o_f + d_pool
    w_f32 = jnp.concatenate([w_in[:, :o_rnn], w_in[:, o_f:]], axis=1).astype(BF16)
    w_forget = jnp.pad(w_in[:, o_qkv:o_f], ((0, 0), (0, LANES - H))).astype(BF16)
    qkv_scale = jnp.ones((1, 3 * d_attn), F32).at[:, :d_attn].set(HEAD_DIM ** -0.5 * LOG2E)
    qkv, h = _norm_matmul(x, row(g_mix), w_in[:, o_rnn:o_qkv].astype(BF16), qkv_scale, BF16)
    xy, pool_in, gates, f_logit = _in_proj(h, w_f32, w_forget, (o_rnn, d_pool, 3 * D))

    y_a = _rglru(xy, conv_w, row(conv_b), w_rg.astype(BF16), row(b_rg), w_ig.astype(BF16),
                 row(b_ig), row(lam), B, S)

    kbias = _forget_bias(f_logit, jnp.pad(b_forget, (0, LANES - H)).reshape(1, LANES), B, S)
    y_b = _attention(qkv, kbias, B, S)

    y_c = _pool(pool_in, w_pool.astype(BF16), row(pool_scale), B, S)

    merged = _merge(y_a, y_b, y_c, w_a.astype(BF16), w_b.astype(BF16), w_c.astype(BF16), gates,
                    tm=256, tn=D)
    x = _matmul_res(merged, w_out.astype(BF16), x, tm=512, tn=D)
    act = _norm_swiglu(x, row(g_ffn), w_ffn_in.astype(BF16))
    return _matmul_res(act, w_ffn_out.astype(BF16), x, tm=256, tn=D, norm_gain=g_out)


def kernel(x, g_mix, w_in, b_forget, conv_w, conv_b, w_rg, b_rg, w_ig, b_ig, lru_lambda, w_pool,
           pool_scale, w_branch_rnn, w_branch_attn, w_branch_pool, w_out, g_ffn, w_ffn_in,
           w_ffn_out, g_final):
    B, S, D = x.shape
    h = x.reshape(B * S, D)
    depth = g_mix.shape[0]
    for l in range(depth):
        h = _layer(h, B, S, g_mix[l], w_in[l], b_forget[l], conv_w[l], conv_b[l], w_rg[l], b_rg[l],
                   w_ig[l], b_ig[l], lru_lambda[l], w_pool[l], pool_scale[l], w_branch_rnn[l],
                   w_branch_attn[l], w_branch_pool[l], w_out[l], g_ffn[l], w_ffn_in[l], w_ffn_out[l],
                   g_out=g_final.reshape(1, -1) if l == depth - 1 else None)
    return h.reshape(B, S, D)
```

```python
import functools

import jax
import jax.numpy as jnp
from jax import lax
from jax.experimental import pallas as pl
from jax.experimental.pallas import tpu as pltpu

F32 = jnp.float32
BF16 = jnp.bfloat16

N_RNN_BLOCKS = 8
CONV_WIDTH = 4
LRU_C = 8.0
N_HEADS = 8
HEAD_DIM = 128
POOL_WINDOWS = (2, 4, 8, 16)
POOL_HALO = 16
CONV_HALO = 8
NORM_EPS = 1e-6
LANES = 128

MIB = 1024 * 1024
VMEM_LIMIT = 56 * MIB
VMEM_BLOCK_BUDGET = 45 * MIB


def _params(*sem):
    return pltpu.CompilerParams(dimension_semantics=sem, vmem_limit_bytes=VMEM_LIMIT)


def _rows_that_fit(m, fixed_bytes, bytes_per_row, cap=1024):
    t = _tile(m, cap)
    while t > 8 and fixed_bytes + t * bytes_per_row > VMEM_BLOCK_BUDGET:
        t //= 2
    return t


def _tile(n, want):
    t = min(n, want)
    while n % t:
        t //= 2
    return t


def _sigmoid(x):
    return 0.5 * jnp.tanh(0.5 * x) + 0.5


def _log_sigmoid(x):
    return jnp.minimum(x, 0.0) - jnp.log1p(jnp.exp(-jnp.abs(x)))


def _normalize_rows(x, g):
    ms = jnp.mean(x * x, axis=-1, keepdims=True)
    return x * lax.rsqrt(ms + NORM_EPS) * g


def _norm_matmul_kernel(x_ref, g_ref, w_ref, c_ref, o_ref, h_ref):
    @pl.when(pl.program_id(1) == 0)
    def _():
        h_ref[...] = _normalize_rows(x_ref[...], g_ref[...]).astype(BF16)

    acc = jnp.dot(h_ref[...], w_ref[...], preferred_element_type=F32)
    o_ref[...] = (acc * c_ref[...]).astype(o_ref.dtype)


def _norm_matmul(x, g, w, layer, col0, N, colscale, out_dtype, tm=1024, tn=1024):
    M, D = x.shape
    tm, tn = _tile(M, tm), _tile(N, tn)
    assert col0 % tn == 0
    j0 = col0 // tn
    return pl.pallas_call(
        _norm_matmul_kernel,
        grid=(M // tm, N // tn),
        in_specs=[pl.BlockSpec((tm, D), lambda i, j: (i, 0)),
                  pl.BlockSpec((1, D), lambda i, j: (0, 0)),
                  pl.BlockSpec((None, D, tn), lambda i, j: (layer, 0, j0 + j)),
                  pl.BlockSpec((1, tn), lambda i, j: (0, j))],
        out_specs=[pl.BlockSpec((tm, tn), lambda i, j: (i, j)),
                   pl.BlockSpec((tm, D), lambda i, j: (i, 0))],
        out_shape=[jax.ShapeDtypeStruct((M, N), out_dtype), jax.ShapeDtypeStruct((M, D), BF16)],
        compiler_params=_params("arbitrary", "arbitrary"),
        name="norm_matmul",
    )(x, g, w, colscale)


def _in_proj_kernel(h_ref, w_ref, wf_ref, *rest, bounds):
    outs, f_ref = rest[:-1], rest[-1]
    j = pl.program_id(1)

    @pl.when(j == 0)
    def _():
        f_ref[...] = jnp.dot(h_ref[...], wf_ref[...], preferred_element_type=F32)

    acc = jnp.dot(h_ref[...], w_ref[...], preferred_element_type=F32)
    tm, tn = acc.shape
    rows = min(tm, tn)
    for k, o_ref in enumerate(outs):
        owns = jnp.broadcast_to((j >= bounds[k]) & (j < bounds[k + 1]), (rows, tn))
        for r in range(0, tm, rows):
            pltpu.store(o_ref.at[r:r + rows, :], acc[r:r + rows, :], mask=owns)


def _in_proj(h, w, w_forget, widths, tm=1024, tn=1024):
    M, D = h.shape
    N = w.shape[1]
    tm = _tile(M, tm)
    assert all(wd % tn == 0 for wd in widths) and sum(widths) == N
    bounds = [0]
    for wd in widths:
        bounds.append(bounds[-1] + wd // tn)

    def out_map(k):
        lo, n = bounds[k], bounds[k + 1] - bounds[k]
        return lambda i, j: (i, jnp.clip(j - lo, 0, n - 1))

    out_specs = [pl.BlockSpec((tm, tn), out_map(k)) for k in range(len(widths))]
    out_shape = [jax.ShapeDtypeStruct((M, wd), F32) for wd in widths]
    out_specs.append(pl.BlockSpec((tm, LANES), lambda i, j: (i, 0)))
    out_shape.append(jax.ShapeDtypeStruct((M, LANES), F32))
    return pl.pallas_call(
        functools.partial(_in_proj_kernel, bounds=tuple(bounds)),
        grid=(M // tm, N // tn),
        in_specs=[pl.BlockSpec((tm, D), lambda i, j: (i, 0)),
                  pl.BlockSpec((D, tn), lambda i, j: (0, j)),
                  pl.BlockSpec((D, LANES), lambda i, j: (0, 0))],
        out_specs=out_specs,
        out_shape=out_shape,
        compiler_params=_params("arbitrary", "arbitrary"),
        name="in_proj",
    )(h, w, w_forget)


def _swiglu_kernel(h_ref, wg_ref, wu_ref, o_ref):
    h = h_ref[...]
    gate = jnp.dot(h, wg_ref[...], preferred_element_type=F32)
    up = jnp.dot(h, wu_ref[...], preferred_element_type=F32)
    o_ref[...] = (gate * _sigmoid(gate) * up).astype(o_ref.dtype)


def _swiglu(h, w, layer, tm=1024, tn=512):
    M, D = h.shape
    F = w.shape[2] // 2
    tm, tn = _tile(M, tm), _tile(F, tn)
    nf = F // tn
    return pl.pallas_call(
        _swiglu_kernel,
        grid=(M // tm, nf),
        in_specs=[pl.BlockSpec((tm, D), lambda i, j: (i, 0)),
                  pl.BlockSpec((None, D, tn), lambda i, j: (layer, 0, j)),
                  pl.BlockSpec((None, D, tn), lambda i, j: (layer, 0, j + nf))],
        out_specs=pl.BlockSpec((tm, tn), lambda i, j: (i, j)),
        out_shape=jax.ShapeDtypeStruct((M, F), BF16),
        compiler_params=_params("parallel", "arbitrary"),
        name="swiglu",
    )(h, w, w)


def _matmul_res_kernel(a_ref, w_ref, r_ref, *rest, norm):
    out = r_ref[...] + jnp.dot(a_ref[...], w_ref[...], preferred_element_type=F32)
    if norm == "replace":
        g_ref, o_ref = rest
        o_ref[...] = _normalize_rows(out, g_ref[...])
    elif norm == "extra":
        g_ref, o_ref, h_ref = rest
        o_ref[...] = out
        h_ref[...] = _normalize_rows(out, g_ref[...]).astype(h_ref.dtype)
    else:
        rest[0][...] = out


def _matmul_res(a, w, layer, res, norm_gain=None, norm=None):
    M, K = a.shape
    N = w.shape[2]
    assert (norm_gain is None) == (norm is None)
    bytes_per_row = 2 * (2 * K + 4 * N + 4 * N + (2 * N if norm == "extra" else 0))
    tm = _rows_that_fit(M, 2 * K * N, bytes_per_row)
    row_tile = pl.BlockSpec((tm, N), lambda i: (i, 0))
    in_specs = [pl.BlockSpec((tm, K), lambda i: (i, 0)),
                pl.BlockSpec((None, K, N), lambda i: (layer, 0, 0), pipeline_mode=pl.Buffered(1)),
                row_tile]
    args = [a, w, res]
    out_specs, out_shape = [row_tile], [jax.ShapeDtypeStruct((M, N), F32)]
    if norm_gain is not None:
        in_specs.append(pl.BlockSpec((1, N), lambda i: (0, 0)))
        args.append(norm_gain)
    if norm == "extra":
        out_specs.append(row_tile)
        out_shape.append(jax.ShapeDtypeStruct((M, N), BF16))
    outs = pl.pallas_call(
        functools.partial(_matmul_res_kernel, norm=norm),
        grid=(M // tm,),
        in_specs=in_specs,
        out_specs=out_specs,
        out_shape=out_shape,
        compiler_params=_params("arbitrary"),
        name="matmul_res",
    )(*args)
    return outs if norm == "extra" else outs[0]


N_BIAS_COLS = 3
LOG2E = 1.4426950408889634


def _split3(x):
    hi = x.astype(BF16)
    rest = x - hi.astype(F32)
    mid = rest.astype(BF16)
    lo = (rest - mid.astype(F32)).astype(BF16)
    return jnp.concatenate([hi, mid, lo], axis=1)


def _forget_bias_kernel(f_ref, b_ref, tri_ref, sel_ref, o_ref, carry_ref):
    @pl.when(pl.program_id(1) == 0)
    def _():
        carry_ref[...] = jnp.zeros_like(carry_ref)

    x = _log_sigmoid(f_ref[...] + b_ref[...])
    cum3 = jnp.dot(tri_ref[...], _split3(x), preferred_element_type=F32)
    cum = (cum3[:, 0:LANES] + cum3[:, LANES:2 * LANES] + cum3[:, 2 * LANES:3 * LANES]
           + carry_ref[0:1, :])
    T = x.shape[0]
    carry_ref[0:1, :] = cum[T - 1:T, :]
    o_ref[...] = jnp.dot(_split3(cum * (-LOG2E)), sel_ref[...],
                         preferred_element_type=F32).astype(o_ref.dtype)


def _forget_bias(f, b, B, S, T=512):
    H = N_HEADS
    T = _tile(S, T)
    nt = S // T
    tri = jnp.tril(jnp.ones((T, T), BF16))
    r = jnp.arange(N_BIAS_COLS * LANES)[:, None]
    c = jnp.arange(H * LANES)[None, :]
    sel = ((r // LANES == c % LANES) & (r % LANES == c // LANES)).astype(BF16)
    return pl.pallas_call(
        _forget_bias_kernel,
        grid=(B, nt),
        in_specs=[pl.BlockSpec((T, LANES), lambda b_, t: (b_ * nt + t, 0)),
                  pl.BlockSpec((1, LANES), lambda b_, t: (0, 0)),
                  pl.BlockSpec((T, T), lambda b_, t: (0, 0)),
                  pl.BlockSpec(sel.shape, lambda b_, t: (0, 0))],
        out_specs=pl.BlockSpec((T, H * LANES), lambda b_, t: (b_ * nt + t, 0)),
        out_shape=jax.ShapeDtypeStruct((B * S, H * LANES), BF16),
        scratch_shapes=[pltpu.VMEM((8, LANES), F32)],
        compiler_params=_params("arbitrary", "arbitrary"),
        name="forget_bias",
    )(f, b, tri, sel)


def _attn_kernel(q_ref, k_ref, v_ref, kb_ref, o_ref, vt_ref, qt_ref, acc_ref, ml_ref, sbuf_ref, *,
                 tq, tk, unroll, ahead):
    i = pl.program_id(2)
    nq = pl.num_programs(2)
    S = k_ref.shape[0]
    Dh = HEAD_DIM
    per_q = tq // tk
    assert 2 <= ahead <= per_q
    slot = i % 2

    def load_queries(tile, dst):
        rows = pl.multiple_of(tile * tq, tq)
        qt_ref[dst, 0:Dh, :] = q_ref[pl.ds(rows, tq), :].T

    def scores(j, qslot):
        start = pl.multiple_of(j * tk, tk)
        k_aug = jnp.concatenate([k_ref[pl.ds(start, tk), :], kb_ref[pl.ds(start, tk), :]], axis=1)
        return jnp.dot(k_aug, qt_ref[qslot], preferred_element_type=F32)

    @pl.when(i == 0)
    def _():
        for c in range(S // tq):
            vt_ref[:, c * tq:(c + 1) * tq] = v_ref[c * tq:(c + 1) * tq, :].T
        row = lax.broadcasted_iota(jnp.int32, (Dh, tq), 0)
        for dst in range(2):
            qt_ref[dst, Dh:2 * Dh, :] = jnp.where(row < N_BIAS_COLS, 1.0, 0.0).astype(BF16)
        load_queries(0, 0)
        for a in range(ahead):
            sbuf_ref[a] = scores(a, 0)

    load_queries(jnp.minimum(i + 1, nq - 1), 1 - slot)

    ml_ref[0:1, :] = jnp.full((1, tq), -jnp.inf, F32)
    ml_ref[1:2, :] = jnp.zeros((1, tq), F32)
    acc_ref[...] = jnp.zeros_like(acc_ref)

    def run(base, count, diag=False):
        pending = {a: sbuf_ref[a] for a in range(ahead)}

        def max_pass(c, m_in):
            s = pending.pop(c)
            if diag:
                key = lax.broadcasted_iota(jnp.int32, (tk, tq), 0) + c * tk
                qry = lax.broadcasted_iota(jnp.int32, (tk, tq), 1)
                s = jnp.where(key <= qry, s, -jnp.inf)
            return s, m_in, jnp.maximum(m_in, jnp.max(s, axis=0, keepdims=True))

        cur = max_pass(0, ml_ref[0:1, :])
        l = ml_ref[1:2, :]
        for c in range(count):
            if diag and c + ahead >= count:
                pending[c + ahead] = scores(c + ahead - count, 1 - slot)
            else:
                pending[c + ahead] = scores(base + c + ahead, slot)
            nxt = max_pass(c + 1, cur[2]) if c + 1 < count else None
            s, m_old, m_new = cur
            p = jnp.exp2(s - m_new)
            alpha = jnp.exp2(m_old - m_new)
            l = alpha * l + jnp.sum(p, axis=0, keepdims=True)
            start = pl.multiple_of((base + c) * tk, tk)
            pv = jnp.dot(vt_ref[:, pl.ds(start, tk)], p.astype(BF16), preferred_element_type=F32)
            acc_ref[...] = alpha * acc_ref[...] + pv
            cur = nxt
        ml_ref[0:1, :] = m_new
        ml_ref[1:2, :] = l
        for t in sorted(pending):
            sbuf_ref[t - count] = pending.pop(t)

    n_full = i * per_q
    n_groups = n_full // unroll

    def group(g, carry):
        run(g * unroll, unroll)
        return carry

    lax.fori_loop(0, n_groups, group, 0)

    rem = n_full - n_groups * unroll
    size = unroll // 2
    while size >= per_q:
        @pl.when((rem & size) != 0)
        def _(size=size):
            run(n_groups * unroll + (rem // (2 * size)) * (2 * size), size)
        size //= 2

    run(n_full, per_q, diag=True)
    o_ref[...] = (acc_ref[...] / ml_ref[1:2, :]).T.astype(o_ref.dtype)


def _attention(qkv, kbias, B, S, tq=512, tk=256, unroll=16, ahead=2):
    H, Dh = N_HEADS, HEAD_DIM
    tq = _tile(S, tq)
    tk = _tile(tq, tk)
    unroll = max(unroll, tq // tk)
    nq = S // tq
    return pl.pallas_call(
        functools.partial(_attn_kernel, tq=tq, tk=tk, unroll=unroll, ahead=ahead),
        grid=(B, H, nq),
        in_specs=[pl.BlockSpec((S, Dh), lambda b, h, i: (b, h)),
                  pl.BlockSpec((S, Dh), lambda b, h, i: (b, H + h)),
                  pl.BlockSpec((S, Dh), lambda b, h, i: (b, 2 * H + h)),
                  pl.BlockSpec((S, LANES), lambda b, h, i: (b, h))],
        out_specs=pl.BlockSpec((tq, Dh), lambda b, h, i: (b * nq + i, h)),
        out_shape=jax.ShapeDtypeStruct((B * S, H * Dh), BF16),
        scratch_shapes=[pltpu.VMEM((Dh, S), BF16), pltpu.VMEM((2, 2 * Dh, tq), BF16),
                        pltpu.VMEM((Dh, tq), F32), pltpu.VMEM((8, tq), F32),
                        pltpu.VMEM((ahead, tk, tq), F32)],
        compiler_params=_params("arbitrary", "arbitrary", "arbitrary"),
        name="fox_attention",
    )(qkv, qkv, qkv, kbias)


def _gelu_tanh(x):
    c = 0.7978845608028654
    return 0.5 * x * (1.0 + jnp.tanh(c * (x + 0.044715 * (x * x * x))))


def _rglru_kernel(x_ref, y_ref, cw_ref, cb_ref, wr_ref, br_ref, wi_ref, bi_ref, lam_ref,
                  o_ref, ext_ref, a_ref, b_ref, h_ref, state_ref, *, T):
    nb, _, C = x_ref.shape
    cb = C // N_RNN_BLOCKS

    @pl.when(pl.program_id(0) == 0)
    def _():
        ext_ref[:, 0:CONV_HALO, :] = jnp.zeros((nb, CONV_HALO, C), F32)
        state_ref[...] = jnp.zeros_like(state_ref)

    log_sig_lam = _log_sigmoid(lam_ref[...])
    for n_b in range(nb):
        x = x_ref[n_b]
        ext_ref[n_b, CONV_HALO:CONV_HALO + T, :] = x
        u = cb_ref[...] + x * cw_ref[CONV_WIDTH - 1:CONV_WIDTH, :]
        for k in range(CONV_WIDTH - 1):
            off = CONV_HALO + k - (CONV_WIDTH - 1)
            u = u + ext_ref[n_b, off:off + T, :] * cw_ref[k:k + 1, :]
        ext_ref[n_b, 0:CONV_HALO, :] = x[T - CONV_HALO:T, :]

        ub = u.astype(BF16)
        r_parts, i_parts = [], []
        for n in range(N_RNN_BLOCKS):
            un = ub[:, n * cb:(n + 1) * cb]
            r_parts.append(jnp.dot(un, wr_ref[n], preferred_element_type=F32))
            i_parts.append(jnp.dot(un, wi_ref[n], preferred_element_type=F32))
        r = _sigmoid(jnp.concatenate(r_parts, axis=1) + br_ref[...])
        ig = _sigmoid(jnp.concatenate(i_parts, axis=1) + bi_ref[...])
        log_a = (LRU_C * r) * log_sig_lam
        a = jnp.exp(log_a)
        a_ref[n_b] = a
        b_ref[n_b] = jnp.sqrt(-jnp.tanh(log_a) * (a * a + 1.0)) * (ig * u)

    def group(g, hs):
        base = pl.multiple_of(g * 8, 8)
        a8 = [a_ref[n_b, pl.ds(base, 8), :] for n_b in range(nb)]
        b8 = [b_ref[n_b, pl.ds(base, 8), :] for n_b in range(nb)]
        hs = list(hs)
        rows = [[] for _ in range(nb)]
        for s in range(8):
            for n_b in range(nb):
                hs[n_b] = a8[n_b][s:s + 1, :] * hs[n_b] + b8[n_b][s:s + 1, :]
                rows[n_b].append(hs[n_b])
        for n_b in range(nb):
            h_ref[n_b, pl.ds(base, 8), :] = jnp.concatenate(rows[n_b], axis=0)
        return tuple(hs)

    h_last = lax.fori_loop(0, T // 8, group, tuple(state_ref[n_b, 0:1, :] for n_b in range(nb)))
    for n_b in range(nb):
        state_ref[n_b, 0:1, :] = h_last[n_b]
        o_ref[n_b] = (_gelu_tanh(y_ref[n_b]) * h_ref[n_b]).astype(o_ref.dtype)


def _rglru(xy, conv_w, conv_b, w_rg, b_rg, w_ig, b_ig, lam, B, S, T=512):
    C = xy.shape[1] // 2
    T = _tile(S, T)
    xy = xy.reshape(B, S, 2 * C)
    vec = pl.BlockSpec((1, C), lambda t: (0, 0))
    blockdiag = pl.BlockSpec(w_rg.shape, lambda t: (0, 0, 0))
    out = pl.pallas_call(
        functools.partial(_rglru_kernel, T=T),
        grid=(S // T,),
        in_specs=[pl.BlockSpec((B, T, C), lambda t: (0, t, 0)),
                  pl.BlockSpec((B, T, C), lambda t: (0, t, 1)),
                  pl.BlockSpec((CONV_WIDTH, C), lambda t: (0, 0)),
                  vec, blockdiag, vec, blockdiag, vec, vec],
        out_specs=pl.BlockSpec((B, T, C), lambda t: (0, t, 0)),
        out_shape=jax.ShapeDtypeStruct((B, S, C), BF16),
        scratch_shapes=[pltpu.VMEM((B, T + CONV_HALO, C), F32),
                        pltpu.VMEM((B, T, C), F32), pltpu.VMEM((B, T, C), F32),
                        pltpu.VMEM((B, T, C), F32), pltpu.VMEM((B, 8, C), F32)],
        compiler_params=_params("arbitrary"),
        name="rglru",
    )(xy, xy, conv_w, conv_b, w_rg, b_rg, w_ig, b_ig, lam)
    return out.reshape(B * S, C)


def _pool_kernel(u_ref, w_ref, s_ref, o_ref, ext_ref, *, T):
    C = u_ref.shape[1]
    G = len(POOL_WINDOWS)
    cg = C // G
    t = pl.program_id(1)

    @pl.when(t == 0)
    def _():
        ext_ref[0:POOL_HALO, :] = jnp.zeros((POOL_HALO, C), F32)

    x = u_ref[...]
    ext_ref[POOL_HALO:POOL_HALO + T, :] = x
    pos = (t * T + 1 + lax.broadcasted_iota(jnp.int32, (T, 1), 0)).astype(F32)
    for g, w in enumerate(POOL_WINDOWS):
        lo = g * cg
        xs = x[:, lo:lo + cg]
        acc = xs
        for k in range(1, w):
            acc = acc + ext_ref[POOL_HALO - k:POOL_HALO - k + T, lo:lo + cg]
        pooled = acc / jnp.minimum(pos, float(w)) - xs
        y = jnp.dot(pooled.astype(BF16), w_ref[g], preferred_element_type=F32)
        o_ref[:, lo:lo + cg] = (y * s_ref[:, lo:lo + cg]).astype(o_ref.dtype)
    ext_ref[0:POOL_HALO, :] = x[T - POOL_HALO:T, :]


def _pool(u, w_pool, scale, B, S, T=512):
    C = u.shape[1]
    T = _tile(S, T)
    nt = S // T
    return pl.pallas_call(
        functools.partial(_pool_kernel, T=T),
        grid=(B, nt),
        in_specs=[pl.BlockSpec((T, C), lambda b, t: (b * nt + t, 0)),
                  pl.BlockSpec(w_pool.shape, lambda b, t: (0, 0, 0)),
                  pl.BlockSpec((1, C), lambda b, t: (0, 0))],
        out_specs=pl.BlockSpec((T, C), lambda b, t: (b * nt + t, 0)),
        out_shape=jax.ShapeDtypeStruct((B * S, C), BF16),
        scratch_shapes=[pltpu.VMEM((T + POOL_HALO, C), F32)],
        compiler_params=_params("arbitrary", "arbitrary"),
        name="multiscale_pool",
    )(u, w_pool, scale)


def _merge_kernel(ya_ref, yb_ref, yc_ref, wa_ref, wb_ref, wc_ref, ga_ref, gb_ref, gc_ref, o_ref):
    out = _sigmoid(ga_ref[...]) * jnp.dot(ya_ref[...], wa_ref[...], preferred_element_type=F32)
    out = out + _sigmoid(gb_ref[...]) * jnp.dot(yb_ref[...], wb_ref[...], preferred_element_type=F32)
    out = out + _sigmoid(gc_ref[...]) * jnp.dot(yc_ref[...], wc_ref[...], preferred_element_type=F32)
    o_ref[...] = out.astype(o_ref.dtype)


def _merge(ya, yb, yc, wa, wb, wc, layer, gates):
    M, C = ya.shape
    D = wa.shape[2]
    tm = _rows_that_fit(M, 2 * 3 * 2 * C * D, 2 * (3 * 2 * C + 3 * 4 * D + 2 * D))
    y_spec = pl.BlockSpec((tm, C), lambda i: (i, 0))
    w_spec = pl.BlockSpec((None, C, D), lambda i: (layer, 0, 0))
    return pl.pallas_call(
        _merge_kernel,
        grid=(M // tm,),
        in_specs=[y_spec, y_spec, y_spec, w_spec, w_spec, w_spec,
                  pl.BlockSpec((tm, D), lambda i: (i, 0)),
                  pl.BlockSpec((tm, D), lambda i: (i, 1)),
                  pl.BlockSpec((tm, D), lambda i: (i, 2))],
        out_specs=pl.BlockSpec((tm, D), lambda i: (i, 0)),
        out_shape=jax.ShapeDtypeStruct((M, D), BF16),
        compiler_params=_params("arbitrary"),
        name="gated_merge",
    )(ya, yb, yc, wa, wb, wc, gates, gates, gates)


def _layer(l, x, B, S, g_mix, w_in, b_forget, conv_w, conv_b, w_rg, b_rg, w_ig, b_ig, lam,
           w_pool, pool_scale, w_a, w_b, w_c, w_out, g_ffn, w_ffn_in, w_ffn_out, g_out=None):
    D = x.shape[1]
    H = N_HEADS
    d_rnn = conv_w.shape[1]
    d_attn = H * HEAD_DIM
    d_pool = pool_scale.shape[0]
    row = lambda v: v.reshape(1, -1)

    o_rnn = 2 * d_rnn
    o_qkv = o_rnn + 3 * d_attn
    o_f = o_qkv + H
    o_pool = o_f + d_pool
    w_f32 = jnp.concatenate([w_in[l, :, :o_rnn], w_in[l, :, o_f:]], axis=1)
    w_forget = jnp.pad(w_in[l, :, o_qkv:o_f], ((0, 0), (0, LANES - H)))
    qkv_scale = jnp.ones((1, 3 * d_attn), F32).at[:, :d_attn].set(HEAD_DIM ** -0.5 * LOG2E)
    qkv, h = _norm_matmul(x, row(g_mix), w_in, l, o_rnn, 3 * d_attn, qkv_scale, BF16)
    xy, pool_in, gates, f_logit = _in_proj(h, w_f32, w_forget, (o_rnn, d_pool, 3 * D))

    y_a = _rglru(xy, conv_w, row(conv_b), w_rg.astype(BF16), row(b_rg), w_ig.astype(BF16),
                 row(b_ig), row(lam), B, S)

    kbias = _forget_bias(f_logit, jnp.pad(b_forget, (0, LANES - H)).reshape(1, LANES), B, S)
    y_b = _attention(qkv, kbias, B, S)

    y_c = _pool(pool_in, w_pool.astype(BF16), row(pool_scale), B, S)

    merged = _merge(y_a, y_b, y_c, w_a, w_b, w_c, l, gates)
    x, h2 = _matmul_res(merged, w_out, l, x, norm_gain=row(g_ffn), norm="extra")
    act = _swiglu(h2, w_ffn_in, l)
    return _matmul_res(act, w_ffn_out, l, x, norm_gain=g_out,
                       norm=None if g_out is None else "replace")


def kernel(x, g_mix, w_in, b_forget, conv_w, conv_b, w_rg, b_rg, w_ig, b_ig, lru_lambda, w_pool,
           pool_scale, w_branch_rnn, w_branch_attn, w_branch_pool, w_out, g_ffn, w_ffn_in,
           w_ffn_out, g_final):
    B, S, D = x.shape
    h = x.reshape(B * S, D)
    depth = g_mix.shape[0]
    big = [w.astype(BF16) for w in (w_in, w_branch_rnn, w_branch_attn, w_branch_pool, w_out,
                                    w_ffn_in, w_ffn_out)]
    w_in, w_branch_rnn, w_branch_attn, w_branch_pool, w_out, w_ffn_in, w_ffn_out = big
    for l in range(depth):
        h = _layer(l, h, B, S, g_mix[l], w_in, b_forget[l], conv_w[l], conv_b[l], w_rg[l], b_rg[l],
                   w_ig[l], b_ig[l], lru_lambda[l], w_pool[l], pool_scale[l], w_branch_rnn,
                   w_branch_attn, w_branch_pool, w_out, g_ffn[l], w_ffn_in, w_ffn_out,
                   g_out=g_final.reshape(1, -1) if l == depth - 1 else None)
    return h.reshape(B, S, D)
```

```python
import functools

import jax
import jax.numpy as jnp
from jax import lax
from jax.experimental import pallas as pl
from jax.experimental.pallas import tpu as pltpu

F32 = jnp.float32
BF16 = jnp.bfloat16

N_RNN_BLOCKS = 8
CONV_WIDTH = 4
LRU_C = 8.0
N_HEADS = 8
HEAD_DIM = 128
POOL_WINDOWS = (2, 4, 8, 16)
POOL_HALO = 16
CONV_HALO = 8
NORM_EPS = 1e-6
LANES = 128

MIB = 1024 * 1024
VMEM_LIMIT = 56 * MIB
VMEM_BLOCK_BUDGET = 45 * MIB


def _params(*sem):
    return pltpu.CompilerParams(dimension_semantics=sem, vmem_limit_bytes=VMEM_LIMIT)


def _rows_that_fit(m, fixed_bytes, bytes_per_row, cap=1024):
    t = _tile(m, cap)
    while t > 8 and fixed_bytes + t * bytes_per_row > VMEM_BLOCK_BUDGET:
        t //= 2
    return t


def _tile(n, want):
    t = min(n, want)
    while n % t:
        t //= 2
    return t


def _sigmoid(x):
    return 0.5 * jnp.tanh(0.5 * x) + 0.5


def _log_sigmoid(x):
    return jnp.minimum(x, 0.0) - jnp.log1p(jnp.exp(-jnp.abs(x)))


def _dot_nt(a, w):
    return lax.dot_general(a, w, (((1,), (1,)), ((), ())), preferred_element_type=F32)


def _normalize_rows(x, g):
    ms = jnp.mean(x * x, axis=-1, keepdims=True)
    return x * lax.rsqrt(ms + NORM_EPS) * g


def _norm_matmul_kernel(x_ref, g_ref, w_ref, c_ref, o_ref, h_ref):
    @pl.when(pl.program_id(1) == 0)
    def _():
        h_ref[...] = _normalize_rows(x_ref[...], g_ref[...]).astype(BF16)

    o_ref[...] = (_dot_nt(h_ref[...], w_ref[...]) * c_ref[...]).astype(o_ref.dtype)


def _norm_matmul(x, g, w, layer, col0, N, colscale, out_dtype, tm=1024, tn=1024):
    M, D = x.shape
    tm, tn = _tile(M, tm), _tile(N, tn)
    assert col0 % tn == 0
    j0 = col0 // tn
    return pl.pallas_call(
        _norm_matmul_kernel,
        grid=(M // tm, N // tn),
        in_specs=[pl.BlockSpec((tm, D), lambda i, j: (i, 0)),
                  pl.BlockSpec((1, D), lambda i, j: (0, 0)),
                  pl.BlockSpec((None, tn, D), lambda i, j: (layer, j0 + j, 0)),
                  pl.BlockSpec((1, tn), lambda i, j: (0, j))],
        out_specs=[pl.BlockSpec((tm, tn), lambda i, j: (i, j)),
                   pl.BlockSpec((tm, D), lambda i, j: (i, 0))],
        out_shape=[jax.ShapeDtypeStruct((M, N), out_dtype), jax.ShapeDtypeStruct((M, D), BF16)],
        compiler_params=_params("arbitrary", "arbitrary"),
        name="norm_matmul",
    )(x, g, w, colscale)


def _in_proj_kernel(h_ref, w_ref, wf_ref, *rest, bounds):
    outs, f_ref = rest[:-1], rest[-1]
    j = pl.program_id(1)

    @pl.when(j == 0)
    def _():
        f_ref[...] = _dot_nt(h_ref[...], wf_ref[...])

    acc = _dot_nt(h_ref[...], w_ref[...])
    tm, tn = acc.shape
    rows = min(tm, tn)
    for k, o_ref in enumerate(outs):
        owns = jnp.broadcast_to((j >= bounds[k]) & (j < bounds[k + 1]), (rows, tn))
        for r in range(0, tm, rows):
            pltpu.store(o_ref.at[r:r + rows, :], acc[r:r + rows, :], mask=owns)


def _in_proj(h, w, w_forget, widths, tm=1024, tn=1024):
    M, D = h.shape
    N = w.shape[0]
    tm = _tile(M, tm)
    assert all(wd % tn == 0 for wd in widths) and sum(widths) == N
    bounds = [0]
    for wd in widths:
        bounds.append(bounds[-1] + wd // tn)

    def out_map(k):
        lo, n = bounds[k], bounds[k + 1] - bounds[k]
        return lambda i, j: (i, jnp.clip(j - lo, 0, n - 1))

    out_specs = [pl.BlockSpec((tm, tn), out_map(k)) for k in range(len(widths))]
    out_shape = [jax.ShapeDtypeStruct((M, wd), F32) for wd in widths]
    out_specs.append(pl.BlockSpec((tm, LANES), lambda i, j: (i, 0)))
    out_shape.append(jax.ShapeDtypeStruct((M, LANES), F32))
    return pl.pallas_call(
        functools.partial(_in_proj_kernel, bounds=tuple(bounds)),
        grid=(M // tm, N // tn),
        in_specs=[pl.BlockSpec((tm, D), lambda i, j: (i, 0)),
                  pl.BlockSpec((tn, D), lambda i, j: (j, 0)),
                  pl.BlockSpec((LANES, D), lambda i, j: (0, 0))],
        out_specs=out_specs,
        out_shape=out_shape,
        compiler_params=_params("arbitrary", "arbitrary"),
        name="in_proj",
    )(h, w, w_forget)


def _swiglu_kernel(h_ref, wg_ref, wu_ref, o_ref):
    h = h_ref[...]
    gate = jnp.dot(h, wg_ref[...], preferred_element_type=F32)
    up = jnp.dot(h, wu_ref[...], preferred_element_type=F32)
    o_ref[...] = (gate * _sigmoid(gate) * up).astype(o_ref.dtype)


def _swiglu(h, w, layer, tm=1024, tn=512):
    M, D = h.shape
    F = w.shape[2] // 2
    tm, tn = _tile(M, tm), _tile(F, tn)
    nf = F // tn
    return pl.pallas_call(
        _swiglu_kernel,
        grid=(M // tm, nf),
        in_specs=[pl.BlockSpec((tm, D), lambda i, j: (i, 0)),
                  pl.BlockSpec((None, D, tn), lambda i, j: (layer, 0, j)),
                  pl.BlockSpec((None, D, tn), lambda i, j: (layer, 0, j + nf))],
        out_specs=pl.BlockSpec((tm, tn), lambda i, j: (i, j)),
        out_shape=jax.ShapeDtypeStruct((M, F), BF16),
        compiler_params=_params("parallel", "arbitrary"),
        name="swiglu",
    )(h, w, w)


def _matmul_res_kernel(a_ref, w_ref, r_ref, *rest, norm):
    out = r_ref[...] + jnp.dot(a_ref[...], w_ref[...], preferred_element_type=F32)
    if norm == "replace":
        g_ref, o_ref = rest
        o_ref[...] = _normalize_rows(out, g_ref[...])
    elif norm == "extra":
        g_ref, o_ref, h_ref = rest
        o_ref[...] = out
        h_ref[...] = _normalize_rows(out, g_ref[...]).astype(h_ref.dtype)
    else:
        rest[0][...] = out


def _matmul_res(a, w, layer, res, norm_gain=None, norm=None):
    M, K = a.shape
    N = w.shape[2]
    assert (norm_gain is None) == (norm is None)
    bytes_per_row = 2 * (2 * K + 4 * N + 4 * N + (2 * N if norm == "extra" else 0))
    tm = _rows_that_fit(M, 2 * K * N, bytes_per_row)
    row_tile = pl.BlockSpec((tm, N), lambda i: (i, 0))
    in_specs = [pl.BlockSpec((tm, K), lambda i: (i, 0)),
                pl.BlockSpec((None, K, N), lambda i: (layer, 0, 0), pipeline_mode=pl.Buffered(1)),
                row_tile]
    args = [a, w, res]
    out_specs, out_shape = [row_tile], [jax.ShapeDtypeStruct((M, N), F32)]
    if norm_gain is not None:
        in_specs.append(pl.BlockSpec((1, N), lambda i: (0, 0)))
        args.append(norm_gain)
    if norm == "extra":
        out_specs.append(row_tile)
        out_shape.append(jax.ShapeDtypeStruct((M, N), BF16))
    outs = pl.pallas_call(
        functools.partial(_matmul_res_kernel, norm=norm),
        grid=(M // tm,),
        in_specs=in_specs,
        out_specs=out_specs,
        out_shape=out_shape,
        compiler_params=_params("arbitrary"),
        name="matmul_res",
    )(*args)
    return outs if norm == "extra" else outs[0]


N_BIAS_COLS = 3
LOG2E = 1.4426950408889634


def _split3(x):
    hi = x.astype(BF16)
    rest = x - hi.astype(F32)
    mid = rest.astype(BF16)
    lo = (rest - mid.astype(F32)).astype(BF16)
    return jnp.concatenate([hi, mid, lo], axis=1)


def _forget_bias_kernel(f_ref, b_ref, tri_ref, sel_ref, o_ref, carry_ref):
    @pl.when(pl.program_id(1) == 0)
    def _():
        carry_ref[...] = jnp.zeros_like(carry_ref)

    x = _log_sigmoid(f_ref[...] + b_ref[...])
    cum3 = jnp.dot(tri_ref[...], _split3(x), preferred_element_type=F32)
    cum = (cum3[:, 0:LANES] + cum3[:, LANES:2 * LANES] + cum3[:, 2 * LANES:3 * LANES]
           + carry_ref[0:1, :])
    T = x.shape[0]
    carry_ref[0:1, :] = cum[T - 1:T, :]
    o_ref[...] = jnp.dot(_split3(cum * (-LOG2E)), sel_ref[...],
                         preferred_element_type=F32).astype(o_ref.dtype)


def _forget_bias(f, b, B, S, T=512):
    H = N_HEADS
    T = _tile(S, T)
    nt = S // T
    tri = jnp.tril(jnp.ones((T, T), BF16))
    r = jnp.arange(N_BIAS_COLS * LANES)[:, None]
    c = jnp.arange(H * LANES)[None, :]
    sel = ((r // LANES == c % LANES) & (r % LANES == c // LANES)).astype(BF16)
    return pl.pallas_call(
        _forget_bias_kernel,
        grid=(B, nt),
        in_specs=[pl.BlockSpec((T, LANES), lambda b_, t: (b_ * nt + t, 0)),
                  pl.BlockSpec((1, LANES), lambda b_, t: (0, 0)),
                  pl.BlockSpec((T, T), lambda b_, t: (0, 0)),
                  pl.BlockSpec(sel.shape, lambda b_, t: (0, 0))],
        out_specs=pl.BlockSpec((T, H * LANES), lambda b_, t: (b_ * nt + t, 0)),
        out_shape=jax.ShapeDtypeStruct((B * S, H * LANES), BF16),
        scratch_shapes=[pltpu.VMEM((8, LANES), F32)],
        compiler_params=_params("arbitrary", "arbitrary"),
        name="forget_bias",
    )(f, b, tri, sel)


def _attn_kernel(q_ref, k_ref, v_ref, kb_ref, o_ref, vt_ref, qt_ref, acc_ref, ml_ref, sbuf_ref, *,
                 tq, tk, unroll, ahead):
    i = pl.program_id(2)
    nq = pl.num_programs(2)
    S = k_ref.shape[0]
    Dh = HEAD_DIM
    per_q = tq // tk
    assert 2 <= ahead <= per_q
    slot = i % 2

    def load_queries(tile, dst):
        rows = pl.multiple_of(tile * tq, tq)
        qt_ref[dst, 0:Dh, :] = q_ref[pl.ds(rows, tq), :].T

    def scores(j, qslot):
        start = pl.multiple_of(j * tk, tk)
        k_aug = jnp.concatenate([k_ref[pl.ds(start, tk), :], kb_ref[pl.ds(start, tk), :]], axis=1)
        return jnp.dot(k_aug, qt_ref[qslot], preferred_element_type=F32)

    @pl.when(i == 0)
    def _():
        for c in range(S // tq):
            vt_ref[:, c * tq:(c + 1) * tq] = v_ref[c * tq:(c + 1) * tq, :].T
        row = lax.broadcasted_iota(jnp.int32, (Dh, tq), 0)
        for dst in range(2):
            qt_ref[dst, Dh:2 * Dh, :] = jnp.where(row < N_BIAS_COLS, 1.0, 0.0).astype(BF16)
        load_queries(0, 0)
        for a in range(ahead):
            sbuf_ref[a] = scores(a, 0)

    load_queries(jnp.minimum(i + 1, nq - 1), 1 - slot)

    ml_ref[0:1, :] = jnp.full((1, tq), -jnp.inf, F32)
    ml_ref[1:2, :] = jnp.zeros((1, tq), F32)
    acc_ref[...] = jnp.zeros_like(acc_ref)

    def run(base, count, diag=False):
        pending = {a: sbuf_ref[a] for a in range(ahead)}

        def max_pass(c, m_in):
            s = pending.pop(c)
            if diag:
                key = lax.broadcasted_iota(jnp.int32, (tk, tq), 0) + c * tk
                qry = lax.broadcasted_iota(jnp.int32, (tk, tq), 1)
                s = jnp.where(key <= qry, s, -jnp.inf)
            return s, m_in, jnp.maximum(m_in, jnp.max(s, axis=0, keepdims=True))

        cur = max_pass(0, ml_ref[0:1, :])
        l = ml_ref[1:2, :]
        for c in range(count):
            if diag and c + ahead >= count:
                pending[c + ahead] = scores(c + ahead - count, 1 - slot)
            else:
                pending[c + ahead] = scores(base + c + ahead, slot)
            nxt = max_pass(c + 1, cur[2]) if c + 1 < count else None
            s, m_old, m_new = cur
            p = jnp.exp2(s - m_new)
            alpha = jnp.exp2(m_old - m_new)
            l = alpha * l + jnp.sum(p, axis=0, keepdims=True)
            start = pl.multiple_of((base + c) * tk, tk)
            pv = jnp.dot(vt_ref[:, pl.ds(start, tk)], p.astype(BF16), preferred_element_type=F32)
            acc_ref[...] = alpha * acc_ref[...] + pv
            cur = nxt
        ml_ref[0:1, :] = m_new
        ml_ref[1:2, :] = l
        for t in sorted(pending):
            sbuf_ref[t - count] = pending.pop(t)

    n_full = i * per_q
    n_groups = n_full // unroll

    def group(g, carry):
        run(g * unroll, unroll)
        return carry

    lax.fori_loop(0, n_groups, group, 0)

    rem = n_full - n_groups * unroll
    size = unroll // 2
    while size >= per_q:
        @pl.when((rem & size) != 0)
        def _(size=size):
            run(n_groups * unroll + (rem // (2 * size)) * (2 * size), size)
        size //= 2

    run(n_full, per_q, diag=True)
    o_ref[...] = (acc_ref[...] / ml_ref[1:2, :]).T.astype(o_ref.dtype)


def _attention(qkv, kbias, B, S, tq=512, tk=256, unroll=16, ahead=2):
    H, Dh = N_HEADS, HEAD_DIM
    tq = _tile(S, tq)
    tk = _tile(tq, tk)
    unroll = max(unroll, tq // tk)
    nq = S // tq
    return pl.pallas_call(
        functools.partial(_attn_kernel, tq=tq, tk=tk, unroll=unroll, ahead=ahead),
        grid=(B, H, nq),
        in_specs=[pl.BlockSpec((S, Dh), lambda b, h, i: (b, h)),
                  pl.BlockSpec((S, Dh), lambda b, h, i: (b, H + h)),
                  pl.BlockSpec((S, Dh), lambda b, h, i: (b, 2 * H + h)),
                  pl.BlockSpec((S, LANES), lambda b, h, i: (b, h))],
        out_specs=pl.BlockSpec((tq, Dh), lambda b, h, i: (b * nq + i, h)),
        out_shape=jax.ShapeDtypeStruct((B * S, H * Dh), BF16),
        scratch_shapes=[pltpu.VMEM((Dh, S), BF16), pltpu.VMEM((2, 2 * Dh, tq), BF16),
                        pltpu.VMEM((Dh, tq), F32), pltpu.VMEM((8, tq), F32),
                        pltpu.VMEM((ahead, tk, tq), F32)],
        compiler_params=_params("arbitrary", "arbitrary", "arbitrary"),
        name="fox_attention",
    )(qkv, qkv, qkv, kbias)


def _gelu_tanh(x):
    c = 0.7978845608028654
    return 0.5 * x * (1.0 + jnp.tanh(c * (x + 0.044715 * (x * x * x))))


def _rglru_kernel(x_ref, y_ref, cw_ref, cb_ref, wr_ref, br_ref, wi_ref, bi_ref, lam_ref,
                  o_ref, ext_ref, a_ref, b_ref, h_ref, state_ref, *, T):
    nb, _, C = x_ref.shape
    cb = C // N_RNN_BLOCKS

    @pl.when(pl.program_id(0) == 0)
    def _():
        ext_ref[:, 0:CONV_HALO, :] = jnp.zeros((nb, CONV_HALO, C), F32)
        state_ref[...] = jnp.zeros_like(state_ref)

    log_sig_lam = _log_sigmoid(lam_ref[...])
    for n_b in range(nb):
        x = x_ref[n_b]
        ext_ref[n_b, CONV_HALO:CONV_HALO + T, :] = x
        u = cb_ref[...] + x * cw_ref[CONV_WIDTH - 1:CONV_WIDTH, :]
        for k in range(CONV_WIDTH - 1):
            off = CONV_HALO + k - (CONV_WIDTH - 1)
            u = u + ext_ref[n_b, off:off + T, :] * cw_ref[k:k + 1, :]
        ext_ref[n_b, 0:CONV_HALO, :] = x[T - CONV_HALO:T, :]

        ub = u.astype(BF16)
        r_parts, i_parts = [], []
        for n in range(N_RNN_BLOCKS):
            un = ub[:, n * cb:(n + 1) * cb]
            r_parts.append(jnp.dot(un, wr_ref[n], preferred_element_type=F32))
            i_parts.append(jnp.dot(un, wi_ref[n], preferred_element_type=F32))
        r = _sigmoid(jnp.concatenate(r_parts, axis=1) + br_ref[...])
        ig = _sigmoid(jnp.concatenate(i_parts, axis=1) + bi_ref[...])
        log_a = (LRU_C * r) * log_sig_lam
        a = jnp.exp(log_a)
        a_ref[n_b] = a
        b_ref[n_b] = jnp.sqrt(-jnp.tanh(log_a) * (a * a + 1.0)) * (ig * u)

    def group(g, hs):
        base = pl.multiple_of(g * 8, 8)
        a8 = [a_ref[n_b, pl.ds(base, 8), :] for n_b in range(nb)]
        b8 = [b_ref[n_b, pl.ds(base, 8), :] for n_b in range(nb)]
        hs = list(hs)
        rows = [[] for _ in range(nb)]
        for s in range(8):
            for n_b in range(nb):
                hs[n_b] = a8[n_b][s:s + 1, :] * hs[n_b] + b8[n_b][s:s + 1, :]
                rows[n_b].append(hs[n_b])
        for n_b in range(nb):
            h_ref[n_b, pl.ds(base, 8), :] = jnp.concatenate(rows[n_b], axis=0)
        return tuple(hs)

    h_last = lax.fori_loop(0, T // 8, group, tuple(state_ref[n_b, 0:1, :] for n_b in range(nb)))
    for n_b in range(nb):
        state_ref[n_b, 0:1, :] = h_last[n_b]
        o_ref[n_b] = (_gelu_tanh(y_ref[n_b]) * h_ref[n_b]).astype(o_ref.dtype)


def _rglru(xy, conv_w, conv_b, w_rg, b_rg, w_ig, b_ig, lam, B, S, T=512):
    C = xy.shape[1] // 2
    T = _tile(S, T)
    xy = xy.reshape(B, S, 2 * C)
    vec = pl.BlockSpec((1, C), lambda t: (0, 0))
    blockdiag = pl.BlockSpec(w_rg.shape, lambda t: (0, 0, 0))
    out = pl.pallas_call(
        functools.partial(_rglru_kernel, T=T),
        grid=(S // T,),
        in_specs=[pl.BlockSpec((B, T, C), lambda t: (0, t, 0)),
                  pl.BlockSpec((B, T, C), lambda t: (0, t, 1)),
                  pl.BlockSpec((CONV_WIDTH, C), lambda t: (0, 0)),
                  vec, blockdiag, vec, blockdiag, vec, vec],
        out_specs=pl.BlockSpec((B, T, C), lambda t: (0, t, 0)),
        out_shape=jax.ShapeDtypeStruct((B, S, C), BF16),
        scratch_shapes=[pltpu.VMEM((B, T + CONV_HALO, C), F32),
                        pltpu.VMEM((B, T, C), F32), pltpu.VMEM((B, T, C), F32),
                        pltpu.VMEM((B, T, C), F32), pltpu.VMEM((B, 8, C), F32)],
        compiler_params=_params("arbitrary"),
        name="rglru",
    )(xy, xy, conv_w, conv_b, w_rg, b_rg, w_ig, b_ig, lam)
    return out.reshape(B * S, C)


def _pool_kernel(u_ref, w_ref, s_ref, o_ref, ext_ref, *, T):
    C = u_ref.shape[1]
    G = len(POOL_WINDOWS)
    cg = C // G
    t = pl.program_id(1)

    @pl.when(t == 0)
    def _():
        ext_ref[0:POOL_HALO, :] = jnp.zeros((POOL_HALO, C), F32)

    x = u_ref[...]
    ext_ref[POOL_HALO:POOL_HALO + T, :] = x
    pos = (t * T + 1 + lax.broadcasted_iota(jnp.int32, (T, 1), 0)).astype(F32)
    for g, w in enumerate(POOL_WINDOWS):
        lo = g * cg
        xs = x[:, lo:lo + cg]
        acc = xs
        for k in range(1, w):
            acc = acc + ext_ref[POOL_HALO - k:POOL_HALO - k + T, lo:lo + cg]
        pooled = acc / jnp.minimum(pos, float(w)) - xs
        y = jnp.dot(pooled.astype(BF16), w_ref[g], preferred_element_type=F32)
        o_ref[:, lo:lo + cg] = (y * s_ref[:, lo:lo + cg]).astype(o_ref.dtype)
    ext_ref[0:POOL_HALO, :] = x[T - POOL_HALO:T, :]


def _pool(u, w_pool, scale, B, S, T=512):
    C = u.shape[1]
    T = _tile(S, T)
    nt = S // T
    return pl.pallas_call(
        functools.partial(_pool_kernel, T=T),
        grid=(B, nt),
        in_specs=[pl.BlockSpec((T, C), lambda b, t: (b * nt + t, 0)),
                  pl.BlockSpec(w_pool.shape, lambda b, t: (0, 0, 0)),
                  pl.BlockSpec((1, C), lambda b, t: (0, 0))],
        out_specs=pl.BlockSpec((T, C), lambda b, t: (b * nt + t, 0)),
        out_shape=jax.ShapeDtypeStruct((B * S, C), BF16),
        scratch_shapes=[pltpu.VMEM((T + POOL_HALO, C), F32)],
        compiler_params=_params("arbitrary", "arbitrary"),
        name="multiscale_pool",
    )(u, w_pool, scale)


def _merge_kernel(ya_ref, yb_ref, yc_ref, wa_ref, wb_ref, wc_ref, ga_ref, gb_ref, gc_ref, o_ref):
    out = _sigmoid(ga_ref[...]) * jnp.dot(ya_ref[...], wa_ref[...], preferred_element_type=F32)
    out = out + _sigmoid(gb_ref[...]) * jnp.dot(yb_ref[...], wb_ref[...], preferred_element_type=F32)
    out = out + _sigmoid(gc_ref[...]) * jnp.dot(yc_ref[...], wc_ref[...], preferred_element_type=F32)
    o_ref[...] = out.astype(o_ref.dtype)


def _merge(ya, yb, yc, wa, wb, wc, layer, gates):
    M, C = ya.shape
    D = wa.shape[2]
    tm = _rows_that_fit(M, 2 * 3 * 2 * C * D, 2 * (3 * 2 * C + 3 * 4 * D + 2 * D))
    y_spec = pl.BlockSpec((tm, C), lambda i: (i, 0))
    w_spec = pl.BlockSpec((None, C, D), lambda i: (layer, 0, 0))
    return pl.pallas_call(
        _merge_kernel,
        grid=(M // tm,),
        in_specs=[y_spec, y_spec, y_spec, w_spec, w_spec, w_spec,
                  pl.BlockSpec((tm, D), lambda i: (i, 0)),
                  pl.BlockSpec((tm, D), lambda i: (i, 1)),
                  pl.BlockSpec((tm, D), lambda i: (i, 2))],
        out_specs=pl.BlockSpec((tm, D), lambda i: (i, 0)),
        out_shape=jax.ShapeDtypeStruct((M, D), BF16),
        compiler_params=_params("arbitrary"),
        name="gated_merge",
    )(ya, yb, yc, wa, wb, wc, gates, gates, gates)


def _layer(l, x, B, S, g_mix, w_in, b_forget, conv_w, conv_b, w_rg, b_rg, w_ig, b_ig, lam,
           w_pool, pool_scale, w_a, w_b, w_c, w_out, g_ffn, w_ffn_in, w_ffn_out, g_out=None):
    D = x.shape[1]
    H = N_HEADS
    d_rnn = conv_w.shape[1]
    d_attn = H * HEAD_DIM
    d_pool = pool_scale.shape[0]
    row = lambda v: v.reshape(1, -1)

    o_rnn = 2 * d_rnn
    o_qkv = o_rnn + 3 * d_attn
    o_f = o_qkv + H
    o_pool = o_f + d_pool
    w_f32 = jnp.concatenate([w_in[l, :o_rnn], w_in[l, o_f:]], axis=0)
    w_forget = jnp.pad(w_in[l, o_qkv:o_f], ((0, LANES - H), (0, 0)))
    qkv_scale = jnp.ones((1, 3 * d_attn), F32).at[:, :d_attn].set(HEAD_DIM ** -0.5 * LOG2E)
    qkv, h = _norm_matmul(x, row(g_mix), w_in, l, o_rnn, 3 * d_attn, qkv_scale, BF16)
    xy, pool_in, gates, f_logit = _in_proj(h, w_f32, w_forget, (o_rnn, d_pool, 3 * D))

    y_a = _rglru(xy, conv_w, row(conv_b), w_rg.astype(BF16), row(b_rg), w_ig.astype(BF16),
                 row(b_ig), row(lam), B, S)

    kbias = _forget_bias(f_logit, jnp.pad(b_forget, (0, LANES - H)).reshape(1, LANES), B, S)
    y_b = _attention(qkv, kbias, B, S)

    y_c = _pool(pool_in, w_pool.astype(BF16), row(pool_scale), B, S)

    merged = _merge(y_a, y_b, y_c, w_a, w_b, w_c, l, gates)
    x, h2 = _matmul_res(merged, w_out, l, x, norm_gain=row(g_ffn), norm="extra")
    act = _swiglu(h2, w_ffn_in, l)
    return _matmul_res(act, w_ffn_out, l, x, norm_gain=g_out,
                       norm=None if g_out is None else "replace")


def kernel(x, g_mix, w_in, b_forget, conv_w, conv_b, w_rg, b_rg, w_ig, b_ig, lru_lambda, w_pool,
           pool_scale, w_branch_rnn, w_branch_attn, w_branch_pool, w_out, g_ffn, w_ffn_in,
           w_ffn_out, g_final):
    B, S, D = x.shape
    h = x.reshape(B * S, D)
    depth = g_mix.shape[0]
    big = [w.astype(BF16) for w in (jnp.swapaxes(w_in, 1, 2), w_branch_rnn, w_branch_attn,
                                    w_branch_pool, w_out, w_ffn_in, w_ffn_out)]
    w_in, w_branch_rnn, w_branch_attn, w_branch_pool, w_out, w_ffn_in, w_ffn_out = big
    for l in range(depth):
        h = _layer(l, h, B, S, g_mix[l], w_in, b_forget[l], conv_w[l], conv_b[l], w_rg[l], b_rg[l],
                   w_ig[l], b_ig[l], lru_lambda[l], w_pool[l], pool_scale[l], w_branch_rnn,
                   w_branch_attn, w_branch_pool, w_out, g_ffn[l], w_ffn_in, w_ffn_out,
                   g_out=g_final.reshape(1, -1) if l == depth - 1 else None)
    return h.reshape(B, S, D)
```

```python
import functools

import jax
import jax.numpy as jnp
from jax import lax
from jax.experimental import pallas as pl
from jax.experimental.pallas import tpu as pltpu

F32 = jnp.float32
BF16 = jnp.bfloat16

N_RNN_BLOCKS = 8
CONV_WIDTH = 4
LRU_C = 8.0
N_HEADS = 8
HEAD_DIM = 128
POOL_WINDOWS = (2, 4, 8, 16)
POOL_HALO = 16
CONV_HALO = 8
NORM_EPS = 1e-6
LANES = 128

MIB = 1024 * 1024
VMEM_LIMIT = 56 * MIB
VMEM_BLOCK_BUDGET = 45 * MIB


def _params(*sem):
    return pltpu.CompilerParams(dimension_semantics=sem, vmem_limit_bytes=VMEM_LIMIT)


def _rows_that_fit(m, fixed_bytes, bytes_per_row, cap=1024):
    t = _tile(m, cap)
    while t > 8 and fixed_bytes + t * bytes_per_row > VMEM_BLOCK_BUDGET:
        t //= 2
    return t


def _tile(n, want):
    t = min(n, want)
    while n % t:
        t //= 2
    return t


def _sigmoid(x):
    return 0.5 * jnp.tanh(0.5 * x) + 0.5


def _log_sigmoid(x):
    return jnp.minimum(x, 0.0) - jnp.log1p(jnp.exp(-jnp.abs(x)))


def _normalize_rows(x, g):
    ms = jnp.mean(x * x, axis=-1, keepdims=True)
    return x * lax.rsqrt(ms + NORM_EPS) * g


def _norm_matmul_kernel(x_ref, g_ref, w_ref, c_ref, o_ref, h_ref):
    @pl.when(pl.program_id(1) == 0)
    def _():
        h_ref[...] = _normalize_rows(x_ref[...], g_ref[...]).astype(BF16)

    acc = jnp.dot(h_ref[...], w_ref[...], preferred_element_type=F32)
    o_ref[...] = (acc * c_ref[...]).astype(o_ref.dtype)


def _norm_matmul(x, g, w, layer, col0, N, colscale, out_dtype, tm=1024, tn=1024):
    M, D = x.shape
    tm, tn = _tile(M, tm), _tile(N, tn)
    assert col0 % tn == 0
    j0 = col0 // tn
    return pl.pallas_call(
        _norm_matmul_kernel,
        grid=(M // tm, N // tn),
        in_specs=[pl.BlockSpec((tm, D), lambda i, j: (i, 0)),
                  pl.BlockSpec((1, D), lambda i, j: (0, 0)),
                  pl.BlockSpec((None, D, tn), lambda i, j: (layer, 0, j0 + j)),
                  pl.BlockSpec((1, tn), lambda i, j: (0, j))],
        out_specs=[pl.BlockSpec((tm, tn), lambda i, j: (i, j)),
                   pl.BlockSpec((tm, D), lambda i, j: (i, 0))],
        out_shape=[jax.ShapeDtypeStruct((M, N), out_dtype), jax.ShapeDtypeStruct((M, D), BF16)],
        compiler_params=_params("arbitrary", "arbitrary"),
        name="norm_matmul",
    )(x, g, w, colscale)


def _in_proj_kernel(h_ref, w_ref, wf_ref, *rest, bounds):
    outs, f_ref = rest[:-1], rest[-1]
    j = pl.program_id(1)

    @pl.when(j == 0)
    def _():
        f_ref[...] = jnp.dot(h_ref[...], wf_ref[...], preferred_element_type=F32)

    acc = jnp.dot(h_ref[...], w_ref[...], preferred_element_type=F32)
    tm, tn = acc.shape
    rows = min(tm, tn)
    for k, o_ref in enumerate(outs):
        owns = jnp.broadcast_to((j >= bounds[k]) & (j < bounds[k + 1]), (rows, tn))
        for r in range(0, tm, rows):
            pltpu.store(o_ref.at[r:r + rows, :], acc[r:r + rows, :], mask=owns)


def _in_proj(h, w, w_forget, widths, tm=1024, tn=1024):
    M, D = h.shape
    N = w.shape[1]
    tm = _tile(M, tm)
    assert all(wd % tn == 0 for wd in widths) and sum(widths) == N
    bounds = [0]
    for wd in widths:
        bounds.append(bounds[-1] + wd // tn)

    def out_map(k):
        lo, n = bounds[k], bounds[k + 1] - bounds[k]
        return lambda i, j: (i, jnp.clip(j - lo, 0, n - 1))

    out_specs = [pl.BlockSpec((tm, tn), out_map(k)) for k in range(len(widths))]
    out_shape = [jax.ShapeDtypeStruct((M, wd), F32) for wd in widths]
    out_specs.append(pl.BlockSpec((tm, LANES), lambda i, j: (i, 0)))
    out_shape.append(jax.ShapeDtypeStruct((M, LANES), F32))
    return pl.pallas_call(
        functools.partial(_in_proj_kernel, bounds=tuple(bounds)),
        grid=(M // tm, N // tn),
        in_specs=[pl.BlockSpec((tm, D), lambda i, j: (i, 0)),
                  pl.BlockSpec((D, tn), lambda i, j: (0, j)),
                  pl.BlockSpec((D, LANES), lambda i, j: (0, 0))],
        out_specs=out_specs,
        out_shape=out_shape,
        compiler_params=_params("arbitrary", "arbitrary"),
        name="in_proj",
    )(h, w, w_forget)


def _swiglu_kernel(h_ref, wg_ref, wu_ref, o_ref):
    h = h_ref[...]
    gate = jnp.dot(h, wg_ref[...], preferred_element_type=F32)
    up = jnp.dot(h, wu_ref[...], preferred_element_type=F32)
    o_ref[...] = (gate * _sigmoid(gate) * up).astype(o_ref.dtype)


def _swiglu(h, w, layer, tm=1024, tn=512):
    M, D = h.shape
    F = w.shape[2] // 2
    tm, tn = _tile(M, tm), _tile(F, tn)
    nf = F // tn
    return pl.pallas_call(
        _swiglu_kernel,
        grid=(M // tm, nf),
        in_specs=[pl.BlockSpec((tm, D), lambda i, j: (i, 0)),
                  pl.BlockSpec((None, D, tn), lambda i, j: (layer, 0, j)),
                  pl.BlockSpec((None, D, tn), lambda i, j: (layer, 0, j + nf))],
        out_specs=pl.BlockSpec((tm, tn), lambda i, j: (i, j)),
        out_shape=jax.ShapeDtypeStruct((M, F), BF16),
        compiler_params=_params("parallel", "arbitrary"),
        name="swiglu",
    )(h, w, w)


def _matmul_res_kernel(a_ref, w_ref, r_ref, *rest, norm):
    out = r_ref[...] + jnp.dot(a_ref[...], w_ref[...], preferred_element_type=F32)
    if norm == "replace":
        g_ref, o_ref = rest
        o_ref[...] = _normalize_rows(out, g_ref[...])
    elif norm == "extra":
        g_ref, o_ref, h_ref = rest
        o_ref[...] = out
        h_ref[...] = _normalize_rows(out, g_ref[...]).astype(h_ref.dtype)
    else:
        rest[0][...] = out


def _matmul_res(a, w, layer, res, norm_gain=None, norm=None):
    M, K = a.shape
    N = w.shape[2]
    assert (norm_gain is None) == (norm is None)
    bytes_per_row = 2 * (2 * K + 4 * N + 4 * N + (2 * N if norm == "extra" else 0))
    tm = _rows_that_fit(M, 2 * K * N, bytes_per_row)
    row_tile = pl.BlockSpec((tm, N), lambda i: (i, 0))
    in_specs = [pl.BlockSpec((tm, K), lambda i: (i, 0)),
                pl.BlockSpec((None, K, N), lambda i: (layer, 0, 0), pipeline_mode=pl.Buffered(1)),
                row_tile]
    args = [a, w, res]
    out_specs, out_shape = [row_tile], [jax.ShapeDtypeStruct((M, N), F32)]
    if norm_gain is not None:
        in_specs.append(pl.BlockSpec((1, N), lambda i: (0, 0)))
        args.append(norm_gain)
    if norm == "extra":
        out_specs.append(row_tile)
        out_shape.append(jax.ShapeDtypeStruct((M, N), BF16))
    outs = pl.pallas_call(
        functools.partial(_matmul_res_kernel, norm=norm),
        grid=(M // tm,),
        in_specs=in_specs,
        out_specs=out_specs,
        out_shape=out_shape,
        compiler_params=_params("arbitrary"),
        name="matmul_res",
    )(*args)
    return outs if norm == "extra" else outs[0]


N_BIAS_COLS = 3
LOG2E = 1.4426950408889634


def _split3(x):
    hi = x.astype(BF16)
    rest = x - hi.astype(F32)
    mid = rest.astype(BF16)
    lo = (rest - mid.astype(F32)).astype(BF16)
    return jnp.concatenate([hi, mid, lo], axis=1)


def _forget_bias_kernel(f_ref, b_ref, tri_ref, sel_ref, o_ref, carry_ref):
    @pl.when(pl.program_id(1) == 0)
    def _():
        carry_ref[...] = jnp.zeros_like(carry_ref)

    x = _log_sigmoid(f_ref[...] + b_ref[...])
    cum3 = jnp.dot(tri_ref[...], _split3(x), preferred_element_type=F32)
    cum = (cum3[:, 0:LANES] + cum3[:, LANES:2 * LANES] + cum3[:, 2 * LANES:3 * LANES]
           + carry_ref[0:1, :])
    T = x.shape[0]
    carry_ref[0:1, :] = cum[T - 1:T, :]
    o_ref[...] = jnp.dot(_split3(cum * (-LOG2E)), sel_ref[...],
                         preferred_element_type=F32).astype(o_ref.dtype)


def _forget_bias(f, b, B, S, T=512):
    H = N_HEADS
    T = _tile(S, T)
    nt = S // T
    tri = jnp.tril(jnp.ones((T, T), BF16))
    r = jnp.arange(N_BIAS_COLS * LANES)[:, None]
    c = jnp.arange(H * LANES)[None, :]
    sel = ((r // LANES == c % LANES) & (r % LANES == c // LANES)).astype(BF16)
    return pl.pallas_call(
        _forget_bias_kernel,
        grid=(B, nt),
        in_specs=[pl.BlockSpec((T, LANES), lambda b_, t: (b_ * nt + t, 0)),
                  pl.BlockSpec((1, LANES), lambda b_, t: (0, 0)),
                  pl.BlockSpec((T, T), lambda b_, t: (0, 0)),
                  pl.BlockSpec(sel.shape, lambda b_, t: (0, 0))],
        out_specs=pl.BlockSpec((T, H * LANES), lambda b_, t: (b_ * nt + t, 0)),
        out_shape=jax.ShapeDtypeStruct((B * S, H * LANES), BF16),
        scratch_shapes=[pltpu.VMEM((8, LANES), F32)],
        compiler_params=_params("arbitrary", "arbitrary"),
        name="forget_bias",
    )(f, b, tri, sel)


def _attn_kernel(q_ref, k_ref, v_ref, kb_ref, o_ref, vt_ref, qt_ref, acc_ref, ml_ref, sbuf_ref, *,
                 tq, tk, unroll, ahead):
    i = pl.program_id(2)
    nq = pl.num_programs(2)
    S = k_ref.shape[0]
    Dh = HEAD_DIM
    per_q = tq // tk
    assert 2 <= ahead <= per_q
    slot = i % 2

    def load_queries(tile, dst):
        rows = pl.multiple_of(tile * tq, tq)
        qt_ref[dst, 0:Dh, :] = q_ref[pl.ds(rows, tq), :].T

    def scores(j, qslot):
        start = pl.multiple_of(j * tk, tk)
        k_aug = jnp.concatenate([k_ref[pl.ds(start, tk), :], kb_ref[pl.ds(start, tk), :]], axis=1)
        return jnp.dot(k_aug, qt_ref[qslot], preferred_element_type=F32)

    @pl.when(i == 0)
    def _():
        for c in range(S // tq):
            vt_ref[:, c * tq:(c + 1) * tq] = v_ref[c * tq:(c + 1) * tq, :].T
        row = lax.broadcasted_iota(jnp.int32, (Dh, tq), 0)
        for dst in range(2):
            qt_ref[dst, Dh:2 * Dh, :] = jnp.where(row < N_BIAS_COLS, 1.0, 0.0).astype(BF16)
        load_queries(0, 0)
        for a in range(ahead):
            sbuf_ref[a] = scores(a, 0)

    load_queries(jnp.minimum(i + 1, nq - 1), 1 - slot)

    ml_ref[0:1, :] = jnp.full((1, tq), -jnp.inf, F32)
    ml_ref[1:2, :] = jnp.zeros((1, tq), F32)
    acc_ref[...] = jnp.zeros_like(acc_ref)

    def run(base, count, diag=False):
        pending = {a: sbuf_ref[a] for a in range(ahead)}

        def max_pass(c, m_in):
            s = pending.pop(c)
            if diag:
                key = lax.broadcasted_iota(jnp.int32, (tk, tq), 0) + c * tk
                qry = lax.broadcasted_iota(jnp.int32, (tk, tq), 1)
                s = jnp.where(key <= qry, s, -jnp.inf)
            return s, m_in, jnp.maximum(m_in, jnp.max(s, axis=0, keepdims=True))

        cur = max_pass(0, ml_ref[0:1, :])
        l = ml_ref[1:2, :]
        for c in range(count):
            if diag and c + ahead >= count:
                pending[c + ahead] = scores(c + ahead - count, 1 - slot)
            else:
                pending[c + ahead] = scores(base + c + ahead, slot)
            nxt = max_pass(c + 1, cur[2]) if c + 1 < count else None
            s, m_old, m_new = cur
            p = jnp.exp2(s - m_new)
            alpha = jnp.exp2(m_old - m_new)
            l = alpha * l + jnp.sum(p, axis=0, keepdims=True)
            start = pl.multiple_of((base + c) * tk, tk)
            pv = jnp.dot(vt_ref[:, pl.ds(start, tk)], p.astype(BF16), preferred_element_type=F32)
            acc_ref[...] = alpha * acc_ref[...] + pv
            cur = nxt
        ml_ref[0:1, :] = m_new
        ml_ref[1:2, :] = l
        for t in sorted(pending):
            sbuf_ref[t - count] = pending.pop(t)

    n_full = i * per_q
    n_groups = n_full // unroll

    def group(g, carry):
        run(g * unroll, unroll)
        return carry

    lax.fori_loop(0, n_groups, group, 0)

    rem = n_full - n_groups * unroll
    size = unroll // 2
    while size >= per_q:
        @pl.when((rem & size) != 0)
        def _(size=size):
            run(n_groups * unroll + (rem // (2 * size)) * (2 * size), size)
        size //= 2

    run(n_full, per_q, diag=True)
    o_ref[...] = (acc_ref[...] / ml_ref[1:2, :]).T.astype(o_ref.dtype)


def _attention(qkv, kbias, B, S, tq=512, tk=256, unroll=16, ahead=2):
    H, Dh = N_HEADS, HEAD_DIM
    tq = _tile(S, tq)
    tk = _tile(tq, tk)
    unroll = max(unroll, tq // tk)
    nq = S // tq
    return pl.pallas_call(
        functools.partial(_attn_kernel, tq=tq, tk=tk, unroll=unroll, ahead=ahead),
        grid=(B, H, nq),
        in_specs=[pl.BlockSpec((S, Dh), lambda b, h, i: (b, h)),
                  pl.BlockSpec((S, Dh), lambda b, h, i: (b, H + h)),
                  pl.BlockSpec((S, Dh), lambda b, h, i: (b, 2 * H + h)),
                  pl.BlockSpec((S, LANES), lambda b, h, i: (b, h))],
        out_specs=pl.BlockSpec((tq, Dh), lambda b, h, i: (b * nq + i, h)),
        out_shape=jax.ShapeDtypeStruct((B * S, H * Dh), BF16),
        scratch_shapes=[pltpu.VMEM((Dh, S), BF16), pltpu.VMEM((2, 2 * Dh, tq), BF16),
                        pltpu.VMEM((Dh, tq), F32), pltpu.VMEM((8, tq), F32),
                        pltpu.VMEM((ahead, tk, tq), F32)],
        compiler_params=_params("arbitrary", "arbitrary", "arbitrary"),
        name="fox_attention",
    )(qkv, qkv, qkv, kbias)


def _gelu_tanh(x):
    c = 0.7978845608028654
    return 0.5 * x * (1.0 + jnp.tanh(c * (x + 0.044715 * (x * x * x))))


def _rglru_kernel(x_ref, y_ref, cw_ref, cb_ref, wr_ref, br_ref, wi_ref, bi_ref, lam_ref,
                  o_ref, ext_ref, a_ref, b_ref, h_ref, state_ref, *, T):
    nb, _, C = x_ref.shape
    cb = C // N_RNN_BLOCKS

    @pl.when(pl.program_id(0) == 0)
    def _():
        ext_ref[:, 0:CONV_HALO, :] = jnp.zeros((nb, CONV_HALO, C), F32)
        state_ref[...] = jnp.zeros_like(state_ref)

    log_sig_lam = _log_sigmoid(lam_ref[...])
    for n_b in range(nb):
        x = x_ref[n_b]
        ext_ref[n_b, CONV_HALO:CONV_HALO + T, :] = x
        u = cb_ref[...] + x * cw_ref[CONV_WIDTH - 1:CONV_WIDTH, :]
        for k in range(CONV_WIDTH - 1):
            off = CONV_HALO + k - (CONV_WIDTH - 1)
            u = u + ext_ref[n_b, off:off + T, :] * cw_ref[k:k + 1, :]
        ext_ref[n_b, 0:CONV_HALO, :] = x[T - CONV_HALO:T, :]

        ub = u.astype(BF16)
        r_parts, i_parts = [], []
        for n in range(N_RNN_BLOCKS):
            un = ub[:, n * cb:(n + 1) * cb]
            r_parts.append(jnp.dot(un, wr_ref[n], preferred_element_type=F32))
            i_parts.append(jnp.dot(un, wi_ref[n], preferred_element_type=F32))
        r = _sigmoid(jnp.concatenate(r_parts, axis=1) + br_ref[...])
        ig = _sigmoid(jnp.concatenate(i_parts, axis=1) + bi_ref[...])
        log_a = (LRU_C * r) * log_sig_lam
        a = jnp.exp(log_a)
        a_ref[n_b] = a
        b_ref[n_b] = jnp.sqrt(-jnp.tanh(log_a) * (a * a + 1.0)) * (ig * u)

    def group(g, hs):
        base = pl.multiple_of(g * 8, 8)
        a8 = [a_ref[n_b, pl.ds(base, 8), :] for n_b in range(nb)]
        b8 = [b_ref[n_b, pl.ds(base, 8), :] for n_b in range(nb)]
        hs = list(hs)
        rows = [[] for _ in range(nb)]
        for s in range(8):
            for n_b in range(nb):
                hs[n_b] = a8[n_b][s:s + 1, :] * hs[n_b] + b8[n_b][s:s + 1, :]
                rows[n_b].append(hs[n_b])
        for n_b in range(nb):
            h_ref[n_b, pl.ds(base, 8), :] = jnp.concatenate(rows[n_b], axis=0)
        return tuple(hs)

    h_last = lax.fori_loop(0, T // 8, group, tuple(state_ref[n_b, 0:1, :] for n_b in range(nb)))
    for n_b in range(nb):
        state_ref[n_b, 0:1, :] = h_last[n_b]
        o_ref[n_b] = (_gelu_tanh(y_ref[n_b]) * h_ref[n_b]).astype(o_ref.dtype)


def _rglru(xy, conv_w, conv_b, w_rg, b_rg, w_ig, b_ig, lam, B, S, T=512):
    C = xy.shape[1] // 2
    T = _tile(S, T)
    xy = xy.reshape(B, S, 2 * C)
    vec = pl.BlockSpec((1, C), lambda t: (0, 0))
    blockdiag = pl.BlockSpec(w_rg.shape, lambda t: (0, 0, 0))
    out = pl.pallas_call(
        functools.partial(_rglru_kernel, T=T),
        grid=(S // T,),
        in_specs=[pl.BlockSpec((B, T, C), lambda t: (0, t, 0)),
                  pl.BlockSpec((B, T, C), lambda t: (0, t, 1)),
                  pl.BlockSpec((CONV_WIDTH, C), lambda t: (0, 0)),
                  vec, blockdiag, vec, blockdiag, vec, vec],
        out_specs=pl.BlockSpec((B, T, C), lambda t: (0, t, 0)),
        out_shape=jax.ShapeDtypeStruct((B, S, C), BF16),
        scratch_shapes=[pltpu.VMEM((B, T + CONV_HALO, C), F32),
                        pltpu.VMEM((B, T, C), F32), pltpu.VMEM((B, T, C), F32),
                        pltpu.VMEM((B, T, C), F32), pltpu.VMEM((B, 8, C), F32)],
        compiler_params=_params("arbitrary"),
        name="rglru",
    )(xy, xy, conv_w, conv_b, w_rg, b_rg, w_ig, b_ig, lam)
    return out.reshape(B * S, C)


def _pool_kernel(u_ref, w_ref, s_ref, o_ref, ext_ref, *, T):
    C = u_ref.shape[1]
    G = len(POOL_WINDOWS)
    cg = C // G
    t = pl.program_id(1)

    @pl.when(t == 0)
    def _():
        ext_ref[0:POOL_HALO, :] = jnp.zeros((POOL_HALO, C), F32)

    x = u_ref[...]
    ext_ref[POOL_HALO:POOL_HALO + T, :] = x
    pos = (t * T + 1 + lax.broadcasted_iota(jnp.int32, (T, 1), 0)).astype(F32)
    for g, w in enumerate(POOL_WINDOWS):
        lo = g * cg
        xs = x[:, lo:lo + cg]
        acc = xs
        for k in range(1, w):
            acc = acc + ext_ref[POOL_HALO - k:POOL_HALO - k + T, lo:lo + cg]
        pooled = acc / jnp.minimum(pos, float(w)) - xs
        y = jnp.dot(pooled.astype(BF16), w_ref[g], preferred_element_type=F32)
        o_ref[:, lo:lo + cg] = (y * s_ref[:, lo:lo + cg]).astype(o_ref.dtype)
    ext_ref[0:POOL_HALO, :] = x[T - POOL_HALO:T, :]


def _pool(u, w_pool, scale, B, S, T=512):
    C = u.shape[1]
    T = _tile(S, T)
    nt = S // T
    return pl.pallas_call(
        functools.partial(_pool_kernel, T=T),
        grid=(B, nt),
        in_specs=[pl.BlockSpec((T, C), lambda b, t: (b * nt + t, 0)),
                  pl.BlockSpec(w_pool.shape, lambda b, t: (0, 0, 0)),
                  pl.BlockSpec((1, C), lambda b, t: (0, 0))],
        out_specs=pl.BlockSpec((T, C), lambda b, t: (b * nt + t, 0)),
        out_shape=jax.ShapeDtypeStruct((B * S, C), BF16),
        scratch_shapes=[pltpu.VMEM((T + POOL_HALO, C), F32)],
        compiler_params=_params("arbitrary", "arbitrary"),
        name="multiscale_pool",
    )(u, w_pool, scale)


def _merge_kernel(ya_ref, yb_ref, yc_ref, wa_ref, wb_ref, wc_ref, ga_ref, gb_ref, gc_ref, o_ref):
    out = _sigmoid(ga_ref[...]) * jnp.dot(ya_ref[...], wa_ref[...], preferred_element_type=F32)
    out = out + _sigmoid(gb_ref[...]) * jnp.dot(yb_ref[...], wb_ref[...], preferred_element_type=F32)
    out = out + _sigmoid(gc_ref[...]) * jnp.dot(yc_ref[...], wc_ref[...], preferred_element_type=F32)
    o_ref[...] = out.astype(o_ref.dtype)


def _merge(ya, yb, yc, wa, wb, wc, layer, gates):
    M, C = ya.shape
    D = wa.shape[2]
    tm = _rows_that_fit(M, 2 * 3 * 2 * C * D, 2 * (3 * 2 * C + 3 * 4 * D + 2 * D))
    y_spec = pl.BlockSpec((tm, C), lambda i: (i, 0))
    w_spec = pl.BlockSpec((None, C, D), lambda i: (layer, 0, 0))
    return pl.pallas_call(
        _merge_kernel,
        grid=(M // tm,),
        in_specs=[y_spec, y_spec, y_spec, w_spec, w_spec, w_spec,
                  pl.BlockSpec((tm, D), lambda i: (i, 0)),
                  pl.BlockSpec((tm, D), lambda i: (i, 1)),
                  pl.BlockSpec((tm, D), lambda i: (i, 2))],
        out_specs=pl.BlockSpec((tm, D), lambda i: (i, 0)),
        out_shape=jax.ShapeDtypeStruct((M, D), BF16),
        compiler_params=_params("arbitrary"),
        name="gated_merge",
    )(ya, yb, yc, wa, wb, wc, gates, gates, gates)


def _layer(l, x, B, S, g_mix, w_in, b_forget, conv_w, conv_b, w_rg, b_rg, w_ig, b_ig, lam,
           w_pool, pool_scale, w_a, w_b, w_c, w_out, g_ffn, w_ffn_in, w_ffn_out, g_out=None):
    D = x.shape[1]
    H = N_HEADS
    d_rnn = conv_w.shape[1]
    d_attn = H * HEAD_DIM
    d_pool = pool_scale.shape[0]
    row = lambda v: v.reshape(1, -1)

    o_rnn = 2 * d_rnn
    o_qkv = o_rnn + 3 * d_attn
    o_f = o_qkv + H
    o_pool = o_f + d_pool
    w_f32 = jnp.concatenate([w_in[l, :, :o_rnn], w_in[l, :, o_f:]], axis=1)
    w_forget = jnp.pad(w_in[l, :, o_qkv:o_f], ((0, 0), (0, LANES - H)))
    qkv_scale = jnp.ones((1, 3 * d_attn), F32).at[:, :d_attn].set(HEAD_DIM ** -0.5 * LOG2E)
    w_qkv = w_in[l:l + 1, :, o_rnn:o_qkv]
    qkv, h = _norm_matmul(x, row(g_mix), w_qkv, 0, 0, 3 * d_attn, qkv_scale, BF16)
    xy, pool_in, gates, f_logit = _in_proj(h, w_f32, w_forget, (o_rnn, d_pool, 3 * D))

    y_a = _rglru(xy, conv_w, row(conv_b), w_rg.astype(BF16), row(b_rg), w_ig.astype(BF16),
                 row(b_ig), row(lam), B, S)

    kbias = _forget_bias(f_logit, jnp.pad(b_forget, (0, LANES - H)).reshape(1, LANES), B, S)
    y_b = _attention(qkv, kbias, B, S)

    y_c = _pool(pool_in, w_pool.astype(BF16), row(pool_scale), B, S)

    merged = _merge(y_a, y_b, y_c, w_a, w_b, w_c, l, gates)
    x, h2 = _matmul_res(merged, w_out, l, x, norm_gain=row(g_ffn), norm="extra")
    act = _swiglu(h2, w_ffn_in, l)
    return _matmul_res(act, w_ffn_out, l, x, norm_gain=g_out,
                       norm=None if g_out is None else "replace")


def kernel(x, g_mix, w_in, b_forget, conv_w, conv_b, w_rg, b_rg, w_ig, b_ig, lru_lambda, w_pool,
           pool_scale, w_branch_rnn, w_branch_attn, w_branch_pool, w_out, g_ffn, w_ffn_in,
           w_ffn_out, g_final):
    B, S, D = x.shape
    h = x.reshape(B * S, D)
    depth = g_mix.shape[0]
    big = [w.astype(BF16) for w in (w_in, w_branch_rnn, w_branch_attn, w_branch_pool, w_out,
                                    w_ffn_in, w_ffn_out)]
    w_in, w_branch_rnn, w_branch_attn, w_branch_pool, w_out, w_ffn_in, w_ffn_out = big
    for l in range(depth):
        h = _layer(l, h, B, S, g_mix[l], w_in, b_forget[l], conv_w[l], conv_b[l], w_rg[l], b_rg[l],
                   w_ig[l], b_ig[l], lru_lambda[l], w_pool[l], pool_scale[l], w_branch_rnn,
                   w_branch_attn, w_branch_pool, w_out, g_ffn[l], w_ffn_in, w_ffn_out,
                   g_out=g_final.reshape(1, -1) if l == depth - 1 else None)
    return h.reshape(B, S, D)
```

```python
import functools

import jax
import jax.numpy as jnp
from jax import lax
from jax.experimental import pallas as pl
from jax.experimental.pallas import tpu as pltpu

F32 = jnp.float32
BF16 = jnp.bfloat16

N_RNN_BLOCKS = 8
CONV_WIDTH = 4
LRU_C = 8.0
N_HEADS = 8
HEAD_DIM = 128
POOL_WINDOWS = (2, 4, 8, 16)
POOL_HALO = 16
CONV_HALO = 8
NORM_EPS = 1e-6
LANES = 128

MIB = 1024 * 1024
VMEM_LIMIT = 56 * MIB
VMEM_BLOCK_BUDGET = 45 * MIB


def _params(*sem):
    return pltpu.CompilerParams(dimension_semantics=sem, vmem_limit_bytes=VMEM_LIMIT)


def _rows_that_fit(m, fixed_bytes, bytes_per_row, cap=1024):
    t = _tile(m, cap)
    while t > 8 and fixed_bytes + t * bytes_per_row > VMEM_BLOCK_BUDGET:
        t //= 2
    return t


def _tile(n, want):
    t = min(n, want)
    while n % t:
        t //= 2
    return t


def _sigmoid(x):
    return 0.5 * jnp.tanh(0.5 * x) + 0.5


def _log_sigmoid(x):
    return jnp.minimum(x, 0.0) - jnp.log1p(jnp.exp(-jnp.abs(x)))


def _normalize_rows(x, g):
    ms = jnp.mean(x * x, axis=-1, keepdims=True)
    return x * lax.rsqrt(ms + NORM_EPS) * g


def _norm_matmul_kernel(x_ref, g_ref, w_ref, c_ref, o_ref, h_ref):
    @pl.when(pl.program_id(1) == 0)
    def _():
        h_ref[...] = _normalize_rows(x_ref[...], g_ref[...]).astype(BF16)

    acc = jnp.dot(h_ref[...], w_ref[...], preferred_element_type=F32)
    o_ref[...] = (acc * c_ref[...]).astype(o_ref.dtype)


def _norm_matmul(x, g, w, layer, col0, N, colscale, out_dtype, tm=1024, tn=1024):
    M, D = x.shape
    tm, tn = _tile(M, tm), _tile(N, tn)
    assert col0 % tn == 0
    j0 = col0 // tn
    return pl.pallas_call(
        _norm_matmul_kernel,
        grid=(M // tm, N // tn),
        in_specs=[pl.BlockSpec((tm, D), lambda i, j: (i, 0)),
                  pl.BlockSpec((1, D), lambda i, j: (0, 0)),
                  pl.BlockSpec((None, D, tn), lambda i, j: (layer, 0, j0 + j)),
                  pl.BlockSpec((1, tn), lambda i, j: (0, j))],
        out_specs=[pl.BlockSpec((tm, tn), lambda i, j: (i, j)),
                   pl.BlockSpec((tm, D), lambda i, j: (i, 0))],
        out_shape=[jax.ShapeDtypeStruct((M, N), out_dtype), jax.ShapeDtypeStruct((M, D), BF16)],
        compiler_params=_params("arbitrary", "arbitrary"),
        name="norm_matmul",
    )(x, g, w, colscale)


def _in_proj_kernel(h_ref, w_ref, wf_ref, *rest, bounds):
    outs, f_ref = rest[:-1], rest[-1]
    j = pl.program_id(1)

    @pl.when(j == 0)
    def _():
        f_ref[...] = jnp.dot(h_ref[...], wf_ref[...], preferred_element_type=F32)

    acc = jnp.dot(h_ref[...], w_ref[...], preferred_element_type=F32)
    tm, tn = acc.shape
    rows = min(tm, tn)
    for k, o_ref in enumerate(outs):
        owns = jnp.broadcast_to((j >= bounds[k]) & (j < bounds[k + 1]), (rows, tn))
        for r in range(0, tm, rows):
            pltpu.store(o_ref.at[r:r + rows, :], acc[r:r + rows, :], mask=owns)


def _in_proj(h, w, w_forget, widths, tm=1024, tn=1024):
    M, D = h.shape
    N = w.shape[1]
    tm = _tile(M, tm)
    assert all(wd % tn == 0 for wd in widths) and sum(widths) == N
    bounds = [0]
    for wd in widths:
        bounds.append(bounds[-1] + wd // tn)

    def out_map(k):
        lo, n = bounds[k], bounds[k + 1] - bounds[k]
        return lambda i, j: (i, jnp.clip(j - lo, 0, n - 1))

    out_specs = [pl.BlockSpec((tm, tn), out_map(k)) for k in range(len(widths))]
    out_shape = [jax.ShapeDtypeStruct((M, wd), F32) for wd in widths]
    out_specs.append(pl.BlockSpec((tm, LANES), lambda i, j: (i, 0)))
    out_shape.append(jax.ShapeDtypeStruct((M, LANES), F32))
    return pl.pallas_call(
        functools.partial(_in_proj_kernel, bounds=tuple(bounds)),
        grid=(M // tm, N // tn),
        in_specs=[pl.BlockSpec((tm, D), lambda i, j: (i, 0)),
                  pl.BlockSpec((D, tn), lambda i, j: (0, j)),
                  pl.BlockSpec((D, LANES), lambda i, j: (0, 0))],
        out_specs=out_specs,
        out_shape=out_shape,
        compiler_params=_params("arbitrary", "arbitrary"),
        name="in_proj",
    )(h, w, w_forget)


def _swiglu_kernel(h_ref, wg_ref, wu_ref, o_ref):
    h = h_ref[...]
    gate = jnp.dot(h, wg_ref[...], preferred_element_type=F32)
    up = jnp.dot(h, wu_ref[...], preferred_element_type=F32)
    o_ref[...] = (gate * _sigmoid(gate) * up).astype(o_ref.dtype)


def _swiglu(h, w, layer, tm=1024, tn=512):
    M, D = h.shape
    F = w.shape[2] // 2
    tm, tn = _tile(M, tm), _tile(F, tn)
    nf = F // tn
    return pl.pallas_call(
        _swiglu_kernel,
        grid=(M // tm, nf),
        in_specs=[pl.BlockSpec((tm, D), lambda i, j: (i, 0)),
                  pl.BlockSpec((None, D, tn), lambda i, j: (layer, 0, j)),
                  pl.BlockSpec((None, D, tn), lambda i, j: (layer, 0, j + nf))],
        out_specs=pl.BlockSpec((tm, tn), lambda i, j: (i, j)),
        out_shape=jax.ShapeDtypeStruct((M, F), BF16),
        compiler_params=_params("parallel", "arbitrary"),
        name="swiglu",
    )(h, w, w)


def _matmul_res_kernel(a_ref, w_ref, r_ref, *rest, norm):
    out = r_ref[...] + jnp.dot(a_ref[...], w_ref[...], preferred_element_type=F32)
    if norm == "replace":
        g_ref, o_ref = rest
        o_ref[...] = _normalize_rows(out, g_ref[...])
    elif norm == "extra":
        g_ref, o_ref, h_ref = rest
        o_ref[...] = out
        h_ref[...] = _normalize_rows(out, g_ref[...]).astype(h_ref.dtype)
    else:
        rest[0][...] = out


def _matmul_res(a, w, layer, res, norm_gain=None, norm=None):
    M, K = a.shape
    N = w.shape[2]
    assert (norm_gain is None) == (norm is None)
    bytes_per_row = 2 * (2 * K + 4 * N + 4 * N + (2 * N if norm == "extra" else 0))
    tm = _rows_that_fit(M, 2 * K * N, bytes_per_row)
    row_tile = pl.BlockSpec((tm, N), lambda i: (i, 0))
    in_specs = [pl.BlockSpec((tm, K), lambda i: (i, 0)),
                pl.BlockSpec((None, K, N), lambda i: (layer, 0, 0), pipeline_mode=pl.Buffered(1)),
                row_tile]
    args = [a, w, res]
    out_specs, out_shape = [row_tile], [jax.ShapeDtypeStruct((M, N), F32)]
    if norm_gain is not None:
        in_specs.append(pl.BlockSpec((1, N), lambda i: (0, 0)))
        args.append(norm_gain)
    if norm == "extra":
        out_specs.append(row_tile)
        out_shape.append(jax.ShapeDtypeStruct((M, N), BF16))
    outs = pl.pallas_call(
        functools.partial(_matmul_res_kernel, norm=norm),
        grid=(M // tm,),
        in_specs=in_specs,
        out_specs=out_specs,
        out_shape=out_shape,
        compiler_params=_params("arbitrary"),
        name="matmul_res",
    )(*args)
    return outs if norm == "extra" else outs[0]


N_BIAS_COLS = 3
LOG2E = 1.4426950408889634


def _split3(x):
    hi = x.astype(BF16)
    rest = x - hi.astype(F32)
    mid = rest.astype(BF16)
    lo = (rest - mid.astype(F32)).astype(BF16)
    return jnp.concatenate([hi, mid, lo], axis=1)


def _forget_bias_kernel(f_ref, b_ref, tri_ref, sel_ref, o_ref, carry_ref):
    @pl.when(pl.program_id(1) == 0)
    def _():
        carry_ref[...] = jnp.zeros_like(carry_ref)

    x = _log_sigmoid(f_ref[...] + b_ref[...])
    cum3 = jnp.dot(tri_ref[...], _split3(x), preferred_element_type=F32)
    cum = (cum3[:, 0:LANES] + cum3[:, LANES:2 * LANES] + cum3[:, 2 * LANES:3 * LANES]
           + carry_ref[0:1, :])
    T = x.shape[0]
    carry_ref[0:1, :] = cum[T - 1:T, :]
    o_ref[...] = jnp.dot(_split3(cum * (-LOG2E)), sel_ref[...],
                         preferred_element_type=F32).astype(o_ref.dtype)


def _forget_bias(f, b, B, S, T=512):
    H = N_HEADS
    T = _tile(S, T)
    nt = S // T
    tri = jnp.tril(jnp.ones((T, T), BF16))
    r = jnp.arange(N_BIAS_COLS * LANES)[:, None]
    c = jnp.arange(H * LANES)[None, :]
    sel = ((r // LANES == c % LANES) & (r % LANES == c // LANES)).astype(BF16)
    return pl.pallas_call(
        _forget_bias_kernel,
        grid=(B, nt),
        in_specs=[pl.BlockSpec((T, LANES), lambda b_, t: (b_ * nt + t, 0)),
                  pl.BlockSpec((1, LANES), lambda b_, t: (0, 0)),
                  pl.BlockSpec((T, T), lambda b_, t: (0, 0)),
                  pl.BlockSpec(sel.shape, lambda b_, t: (0, 0))],
        out_specs=pl.BlockSpec((T, H * LANES), lambda b_, t: (b_ * nt + t, 0)),
        out_shape=jax.ShapeDtypeStruct((B * S, H * LANES), BF16),
        scratch_shapes=[pltpu.VMEM((8, LANES), F32)],
        compiler_params=_params("arbitrary", "arbitrary"),
        name="forget_bias",
    )(f, b, tri, sel)


def _attn_kernel(q_ref, k_ref, v_ref, kb_ref, o_ref, vt_ref, qt_ref, acc_ref, ml_ref, sbuf_ref, *,
                 tq, tk, unroll, ahead):
    i = pl.program_id(2)
    nq = pl.num_programs(2)
    S = k_ref.shape[0]
    Dh = HEAD_DIM
    per_q = tq // tk
    assert 2 <= ahead <= per_q
    slot = i % 2

    def load_queries(tile, dst):
        rows = pl.multiple_of(tile * tq, tq)
        qt_ref[dst, 0:Dh, :] = q_ref[pl.ds(rows, tq), :].T

    def scores(j, qslot):
        start = pl.multiple_of(j * tk, tk)
        k_aug = jnp.concatenate([k_ref[pl.ds(start, tk), :], kb_ref[pl.ds(start, tk), :]], axis=1)
        return jnp.dot(k_aug, qt_ref[qslot], preferred_element_type=F32)

    @pl.when(i == 0)
    def _():
        for c in range(S // tq):
            vt_ref[:, c * tq:(c + 1) * tq] = v_ref[c * tq:(c + 1) * tq, :].T
        row = lax.broadcasted_iota(jnp.int32, (Dh, tq), 0)
        for dst in range(2):
            qt_ref[dst, Dh:2 * Dh, :] = jnp.where(row < N_BIAS_COLS, 1.0, 0.0).astype(BF16)
        load_queries(0, 0)
        for a in range(ahead):
            sbuf_ref[a] = scores(a, 0)

    load_queries(jnp.minimum(i + 1, nq - 1), 1 - slot)

    ml_ref[0:1, :] = jnp.full((1, tq), -jnp.inf, F32)
    ml_ref[1:2, :] = jnp.zeros((1, tq), F32)
    acc_ref[...] = jnp.zeros_like(acc_ref)

    def run(base, count, last=False):
        first_diag = count - per_q if last else count
        pending = {a: sbuf_ref[a] for a in range(ahead)}

        def max_pass(c, m_in):
            s = pending.pop(c)
            if c >= first_diag:
                key = lax.broadcasted_iota(jnp.int32, (tk, tq), 0) + (c - first_diag) * tk
                qry = lax.broadcasted_iota(jnp.int32, (tk, tq), 1)
                s = jnp.where(key <= qry, s, -jnp.inf)
            return s, m_in, jnp.maximum(m_in, jnp.max(s, axis=0, keepdims=True))

        cur = max_pass(0, ml_ref[0:1, :])
        l = ml_ref[1:2, :]
        for c in range(count):
            if last and c + ahead >= count:
                pending[c + ahead] = scores(c + ahead - count, 1 - slot)
            else:
                pending[c + ahead] = scores(base + c + ahead, slot)
            nxt = max_pass(c + 1, cur[2]) if c + 1 < count else None
            s, m_old, m_new = cur
            p = jnp.exp2(s - m_new)
            alpha = jnp.exp2(m_old - m_new)
            l = alpha * l + jnp.sum(p, axis=0, keepdims=True)
            start = pl.multiple_of((base + c) * tk, tk)
            pv = jnp.dot(vt_ref[:, pl.ds(start, tk)], p.astype(BF16), preferred_element_type=F32)
            acc_ref[...] = alpha * acc_ref[...] + pv
            cur = nxt
        ml_ref[0:1, :] = m_new
        ml_ref[1:2, :] = l
        for t in sorted(pending):
            sbuf_ref[t - count] = pending.pop(t)

    total = (i + 1) * per_q
    n_groups = (total - 1) // unroll

    def group(g, carry):
        run(g * unroll, unroll)
        return carry

    lax.fori_loop(0, n_groups, group, 0)

    tail = total - n_groups * unroll
    for size in range(per_q, unroll + 1, per_q):
        @pl.when(tail == size)
        def _(size=size):
            run(n_groups * unroll, size, last=True)

    o_ref[...] = (acc_ref[...] / ml_ref[1:2, :]).T.astype(o_ref.dtype)


def _attention(qkv, kbias, B, S, tq=512, tk=256, unroll=16, ahead=2):
    H, Dh = N_HEADS, HEAD_DIM
    tq = _tile(S, tq)
    tk = _tile(tq, tk)
    unroll = max(unroll, tq // tk)
    nq = S // tq
    return pl.pallas_call(
        functools.partial(_attn_kernel, tq=tq, tk=tk, unroll=unroll, ahead=ahead),
        grid=(B, H, nq),
        in_specs=[pl.BlockSpec((S, Dh), lambda b, h, i: (b, h)),
                  pl.BlockSpec((S, Dh), lambda b, h, i: (b, H + h)),
                  pl.BlockSpec((S, Dh), lambda b, h, i: (b, 2 * H + h)),
                  pl.BlockSpec((S, LANES), lambda b, h, i: (b, h))],
        out_specs=pl.BlockSpec((tq, Dh), lambda b, h, i: (b * nq + i, h)),
        out_shape=jax.ShapeDtypeStruct((B * S, H * Dh), BF16),
        scratch_shapes=[pltpu.VMEM((Dh, S), BF16), pltpu.VMEM((2, 2 * Dh, tq), BF16),
                        pltpu.VMEM((Dh, tq), F32), pltpu.VMEM((8, tq), F32),
                        pltpu.VMEM((ahead, tk, tq), F32)],
        compiler_params=_params("arbitrary", "arbitrary", "arbitrary"),
        name="fox_attention",
    )(qkv, qkv, qkv, kbias)


def _gelu_tanh(x):
    c = 0.7978845608028654
    return 0.5 * x * (1.0 + jnp.tanh(c * (x + 0.044715 * (x * x * x))))


def _rglru_kernel(x_ref, y_ref, cw_ref, cb_ref, wr_ref, br_ref, wi_ref, bi_ref, lam_ref,
                  o_ref, ext_ref, a_ref, b_ref, h_ref, state_ref, *, T):
    nb, _, C = x_ref.shape
    cb = C // N_RNN_BLOCKS

    @pl.when(pl.program_id(0) == 0)
    def _():
        ext_ref[:, 0:CONV_HALO, :] = jnp.zeros((nb, CONV_HALO, C), F32)
        state_ref[...] = jnp.zeros_like(state_ref)

    log_sig_lam = _log_sigmoid(lam_ref[...])
    for n_b in range(nb):
        x = x_ref[n_b]
        ext_ref[n_b, CONV_HALO:CONV_HALO + T, :] = x
        u = cb_ref[...] + x * cw_ref[CONV_WIDTH - 1:CONV_WIDTH, :]
        for k in range(CONV_WIDTH - 1):
            off = CONV_HALO + k - (CONV_WIDTH - 1)
            u = u + ext_ref[n_b, off:off + T, :] * cw_ref[k:k + 1, :]
        ext_ref[n_b, 0:CONV_HALO, :] = x[T - CONV_HALO:T, :]

        ub = u.astype(BF16)
        r_parts, i_parts = [], []
        for n in range(N_RNN_BLOCKS):
            un = ub[:, n * cb:(n + 1) * cb]
            r_parts.append(jnp.dot(un, wr_ref[n], preferred_element_type=F32))
            i_parts.append(jnp.dot(un, wi_ref[n], preferred_element_type=F32))
        r = _sigmoid(jnp.concatenate(r_parts, axis=1) + br_ref[...])
        ig = _sigmoid(jnp.concatenate(i_parts, axis=1) + bi_ref[...])
        log_a = (LRU_C * r) * log_sig_lam
        a = jnp.exp(log_a)
        a_ref[n_b] = a
        b_ref[n_b] = jnp.sqrt(-jnp.tanh(log_a) * (a * a + 1.0)) * (ig * u)

    def group(g, hs):
        base = pl.multiple_of(g * 8, 8)
        a8 = [a_ref[n_b, pl.ds(base, 8), :] for n_b in range(nb)]
        b8 = [b_ref[n_b, pl.ds(base, 8), :] for n_b in range(nb)]
        hs = list(hs)
        rows = [[] for _ in range(nb)]
        for s in range(8):
            for n_b in range(nb):
                hs[n_b] = a8[n_b][s:s + 1, :] * hs[n_b] + b8[n_b][s:s + 1, :]
                rows[n_b].append(hs[n_b])
        for n_b in range(nb):
            h_ref[n_b, pl.ds(base, 8), :] = jnp.concatenate(rows[n_b], axis=0)
        return tuple(hs)

    h_last = lax.fori_loop(0, T // 8, group, tuple(state_ref[n_b, 0:1, :] for n_b in range(nb)))
    for n_b in range(nb):
        state_ref[n_b, 0:1, :] = h_last[n_b]
        o_ref[n_b] = (_gelu_tanh(y_ref[n_b]) * h_ref[n_b]).astype(o_ref.dtype)


def _rglru(xy, conv_w, conv_b, w_rg, b_rg, w_ig, b_ig, lam, B, S, T=512):
    C = xy.shape[1] // 2
    T = _tile(S, T)
    xy = xy.reshape(B, S, 2 * C)
    vec = pl.BlockSpec((1, C), lambda t: (0, 0))
    blockdiag = pl.BlockSpec(w_rg.shape, lambda t: (0, 0, 0))
    out = pl.pallas_call(
        functools.partial(_rglru_kernel, T=T),
        grid=(S // T,),
        in_specs=[pl.BlockSpec((B, T, C), lambda t: (0, t, 0)),
                  pl.BlockSpec((B, T, C), lambda t: (0, t, 1)),
                  pl.BlockSpec((CONV_WIDTH, C), lambda t: (0, 0)),
                  vec, blockdiag, vec, blockdiag, vec, vec],
        out_specs=pl.BlockSpec((B, T, C), lambda t: (0, t, 0)),
        out_shape=jax.ShapeDtypeStruct((B, S, C), BF16),
        scratch_shapes=[pltpu.VMEM((B, T + CONV_HALO, C), F32),
                        pltpu.VMEM((B, T, C), F32), pltpu.VMEM((B, T, C), F32),
                        pltpu.VMEM((B, T, C), F32), pltpu.VMEM((B, 8, C), F32)],
        compiler_params=_params("arbitrary"),
        name="rglru",
    )(xy, xy, conv_w, conv_b, w_rg, b_rg, w_ig, b_ig, lam)
    return out.reshape(B * S, C)


def _pool_kernel(u_ref, w_ref, s_ref, o_ref, ext_ref, *, T):
    C = u_ref.shape[1]
    G = len(POOL_WINDOWS)
    cg = C // G
    t = pl.program_id(1)

    @pl.when(t == 0)
    def _():
        ext_ref[0:POOL_HALO, :] = jnp.zeros((POOL_HALO, C), F32)

    x = u_ref[...]
    ext_ref[POOL_HALO:POOL_HALO + T, :] = x
    pos = (t * T + 1 + lax.broadcasted_iota(jnp.int32, (T, 1), 0)).astype(F32)
    for g, w in enumerate(POOL_WINDOWS):
        lo = g * cg
        xs = x[:, lo:lo + cg]
        acc = xs
        for k in range(1, w):
            acc = acc + ext_ref[POOL_HALO - k:POOL_HALO - k + T, lo:lo + cg]
        pooled = acc / jnp.minimum(pos, float(w)) - xs
        y = jnp.dot(pooled.astype(BF16), w_ref[g], preferred_element_type=F32)
        o_ref[:, lo:lo + cg] = (y * s_ref[:, lo:lo + cg]).astype(o_ref.dtype)
    ext_ref[0:POOL_HALO, :] = x[T - POOL_HALO:T, :]


def _pool(u, w_pool, scale, B, S, T=512):
    C = u.shape[1]
    T = _tile(S, T)
    nt = S // T
    return pl.pallas_call(
        functools.partial(_pool_kernel, T=T),
        grid=(B, nt),
        in_specs=[pl.BlockSpec((T, C), lambda b, t: (b * nt + t, 0)),
                  pl.BlockSpec(w_pool.shape, lambda b, t: (0, 0, 0)),
                  pl.BlockSpec((1, C), lambda b, t: (0, 0))],
        out_specs=pl.BlockSpec((T, C), lambda b, t: (b * nt + t, 0)),
        out_shape=jax.ShapeDtypeStruct((B * S, C), BF16),
        scratch_shapes=[pltpu.VMEM((T + POOL_HALO, C), F32)],
        compiler_params=_params("arbitrary", "arbitrary"),
        name="multiscale_pool",
    )(u, w_pool, scale)


def _merge_kernel(ya_ref, yb_ref, yc_ref, wa_ref, wb_ref, wc_ref, ga_ref, gb_ref, gc_ref, o_ref):
    out = _sigmoid(ga_ref[...]) * jnp.dot(ya_ref[...], wa_ref[...], preferred_element_type=F32)
    out = out + _sigmoid(gb_ref[...]) * jnp.dot(yb_ref[...], wb_ref[...], preferred_element_type=F32)
    out = out + _sigmoid(gc_ref[...]) * jnp.dot(yc_ref[...], wc_ref[...], preferred_element_type=F32)
    o_ref[...] = out.astype(o_ref.dtype)


def _merge(ya, yb, yc, wa, wb, wc, layer, gates):
    M, C = ya.shape
    D = wa.shape[2]
    tm = _rows_that_fit(M, 2 * 3 * 2 * C * D, 2 * (3 * 2 * C + 3 * 4 * D + 2 * D))
    y_spec = pl.BlockSpec((tm, C), lambda i: (i, 0))
    w_spec = pl.BlockSpec((None, C, D), lambda i: (layer, 0, 0))
    return pl.pallas_call(
        _merge_kernel,
        grid=(M // tm,),
        in_specs=[y_spec, y_spec, y_spec, w_spec, w_spec, w_spec,
                  pl.BlockSpec((tm, D), lambda i: (i, 0)),
                  pl.BlockSpec((tm, D), lambda i: (i, 1)),
                  pl.BlockSpec((tm, D), lambda i: (i, 2))],
        out_specs=pl.BlockSpec((tm, D), lambda i: (i, 0)),
        out_shape=jax.ShapeDtypeStruct((M, D), BF16),
        compiler_params=_params("arbitrary"),
        name="gated_merge",
    )(ya, yb, yc, wa, wb, wc, gates, gates, gates)


def _layer(l, x, B, S, g_mix, w_in, b_forget, conv_w, conv_b, w_rg, b_rg, w_ig, b_ig, lam,
           w_pool, pool_scale, w_a, w_b, w_c, w_out, g_ffn, w_ffn_in, w_ffn_out, g_out=None):
    D = x.shape[1]
    H = N_HEADS
    d_rnn = conv_w.shape[1]
    d_attn = H * HEAD_DIM
    d_pool = pool_scale.shape[0]
    row = lambda v: v.reshape(1, -1)

    o_rnn = 2 * d_rnn
    o_qkv = o_rnn + 3 * d_attn
    o_f = o_qkv + H
    o_pool = o_f + d_pool
    w_f32 = jnp.concatenate([w_in[l, :, :o_rnn], w_in[l, :, o_f:]], axis=1)
    w_forget = jnp.pad(w_in[l, :, o_qkv:o_f], ((0, 0), (0, LANES - H)))
    qkv_scale = jnp.ones((1, 3 * d_attn), F32).at[:, :d_attn].set(HEAD_DIM ** -0.5 * LOG2E)
    w_qkv = w_in[l:l + 1, :, o_rnn:o_qkv]
    qkv, h = _norm_matmul(x, row(g_mix), w_qkv, 0, 0, 3 * d_attn, qkv_scale, BF16)
    xy, pool_in, gates, f_logit = _in_proj(h, w_f32, w_forget, (o_rnn, d_pool, 3 * D))

    y_a = _rglru(xy, conv_w, row(conv_b), w_rg.astype(BF16), row(b_rg), w_ig.astype(BF16),
                 row(b_ig), row(lam), B, S)

    kbias = _forget_bias(f_logit, jnp.pad(b_forget, (0, LANES - H)).reshape(1, LANES), B, S)
    y_b = _attention(qkv, kbias, B, S)

    y_c = _pool(pool_in, w_pool.astype(BF16), row(pool_scale), B, S)

    merged = _merge(y_a, y_b, y_c, w_a, w_b, w_c, l, gates)
    x, h2 = _matmul_res(merged, w_out, l, x, norm_gain=row(g_ffn), norm="extra")
    act = _swiglu(h2, w_ffn_in, l)
    return _matmul_res(act, w_ffn_out, l, x, norm_gain=g_out,
                       norm=None if g_out is None else "replace")


def kernel(x, g_mix, w_in, b_forget, conv_w, conv_b, w_rg, b_rg, w_ig, b_ig, lru_lambda, w_pool,
           pool_scale, w_branch_rnn, w_branch_attn, w_branch_pool, w_out, g_ffn, w_ffn_in,
           w_ffn_out, g_final):
    B, S, D = x.shape
    h = x.reshape(B * S, D)
    depth = g_mix.shape[0]
    big = [w.astype(BF16) for w in (w_in, w_branch_rnn, w_branch_attn, w_branch_pool, w_out,
                                    w_ffn_in, w_ffn_out)]
    w_in, w_branch_rnn, w_branch_attn, w_branch_pool, w_out, w_ffn_in, w_ffn_out = big
    for l in range(depth):
        h = _layer(l, h, B, S, g_mix[l], w_in, b_forget[l], conv_w[l], conv_b[l], w_rg[l], b_rg[l],
                   w_ig[l], b_ig[l], lru_lambda[l], w_pool[l], pool_scale[l], w_branch_rnn,
                   w_branch_attn, w_branch_pool, w_out, g_ffn[l], w_ffn_in, w_ffn_out,
                   g_out=g_final.reshape(1, -1) if l == depth - 1 else None)
    return h.reshape(B, S, D)
```

```python
import functools

import jax
import jax.numpy as jnp
from jax import lax
from jax.experimental import pallas as pl
from jax.experimental.pallas import tpu as pltpu

F32 = jnp.float32
BF16 = jnp.bfloat16

N_RNN_BLOCKS = 8
CONV_WIDTH = 4
LRU_C = 8.0
N_HEADS = 8
HEAD_DIM = 128
POOL_WINDOWS = (2, 4, 8, 16)
POOL_HALO = 16
CONV_HALO = 8
NORM_EPS = 1e-6
LANES = 128

MIB = 1024 * 1024
VMEM_LIMIT = 56 * MIB
VMEM_BLOCK_BUDGET = 45 * MIB


def _params(*sem):
    return pltpu.CompilerParams(dimension_semantics=sem, vmem_limit_bytes=VMEM_LIMIT)


def _rows_that_fit(m, fixed_bytes, bytes_per_row, cap=1024):
    t = _tile(m, cap)
    while t > 8 and fixed_bytes + t * bytes_per_row > VMEM_BLOCK_BUDGET:
        t //= 2
    return t


def _tile(n, want):
    t = min(n, want)
    while n % t:
        t //= 2
    return t


def _sigmoid(x):
    return 0.5 * jnp.tanh(0.5 * x) + 0.5


def _log_sigmoid(x):
    return jnp.minimum(x, 0.0) - jnp.log1p(jnp.exp(-jnp.abs(x)))


def _normalize_rows(x, g):
    ms = jnp.mean(x * x, axis=-1, keepdims=True)
    return x * lax.rsqrt(ms + NORM_EPS) * g


def _norm_matmul_kernel(x_ref, g_ref, w_ref, c_ref, o_ref, h_ref):
    @pl.when(pl.program_id(1) == 0)
    def _():
        h_ref[...] = _normalize_rows(x_ref[...], g_ref[...]).astype(BF16)

    acc = jnp.dot(h_ref[...], w_ref[...], preferred_element_type=F32)
    o_ref[...] = (acc * c_ref[...]).astype(o_ref.dtype)


def _norm_matmul(x, g, w, layer, col0, N, colscale, out_dtype, tm=1024, tn=1024):
    M, D = x.shape
    tm, tn = _tile(M, tm), _tile(N, tn)
    assert col0 % tn == 0
    j0 = col0 // tn
    return pl.pallas_call(
        _norm_matmul_kernel,
        grid=(M // tm, N // tn),
        in_specs=[pl.BlockSpec((tm, D), lambda i, j: (i, 0)),
                  pl.BlockSpec((1, D), lambda i, j: (0, 0)),
                  pl.BlockSpec((None, D, tn), lambda i, j: (layer, 0, j0 + j)),
                  pl.BlockSpec((1, tn), lambda i, j: (0, j))],
        out_specs=[pl.BlockSpec((tm, tn), lambda i, j: (i, j)),
                   pl.BlockSpec((tm, D), lambda i, j: (i, 0))],
        out_shape=[jax.ShapeDtypeStruct((M, N), out_dtype), jax.ShapeDtypeStruct((M, D), BF16)],
        compiler_params=_params("arbitrary", "arbitrary"),
        name="norm_matmul",
    )(x, g, w, colscale)


def _in_proj_kernel(h_ref, w_ref, wf_ref, *rest, bounds):
    outs, f_ref = rest[:-1], rest[-1]
    j = pl.program_id(1)

    @pl.when(j == 0)
    def _():
        f_ref[...] = jnp.dot(h_ref[...], wf_ref[...], preferred_element_type=F32)

    acc = jnp.dot(h_ref[...], w_ref[...], preferred_element_type=F32)
    tm, tn = acc.shape
    rows = min(tm, tn)
    for k, o_ref in enumerate(outs):
        owns = jnp.broadcast_to((j >= bounds[k]) & (j < bounds[k + 1]), (rows, tn))
        for r in range(0, tm, rows):
            pltpu.store(o_ref.at[r:r + rows, :], acc[r:r + rows, :], mask=owns)


def _in_proj(h, w, w_forget, widths, tm=1024, tn=1024):
    M, D = h.shape
    N = w.shape[1]
    tm = _tile(M, tm)
    assert all(wd % tn == 0 for wd in widths) and sum(widths) == N
    bounds = [0]
    for wd in widths:
        bounds.append(bounds[-1] + wd // tn)

    def out_map(k):
        lo, n = bounds[k], bounds[k + 1] - bounds[k]
        return lambda i, j: (i, jnp.clip(j - lo, 0, n - 1))

    out_specs = [pl.BlockSpec((tm, tn), out_map(k)) for k in range(len(widths))]
    out_shape = [jax.ShapeDtypeStruct((M, wd), F32) for wd in widths]
    out_specs.append(pl.BlockSpec((tm, LANES), lambda i, j: (i, 0)))
    out_shape.append(jax.ShapeDtypeStruct((M, LANES), F32))
    return pl.pallas_call(
        functools.partial(_in_proj_kernel, bounds=tuple(bounds)),
        grid=(M // tm, N // tn),
        in_specs=[pl.BlockSpec((tm, D), lambda i, j: (i, 0)),
                  pl.BlockSpec((D, tn), lambda i, j: (0, j)),
                  pl.BlockSpec((D, LANES), lambda i, j: (0, 0))],
        out_specs=out_specs,
        out_shape=out_shape,
        compiler_params=_params("arbitrary", "arbitrary"),
        name="in_proj",
    )(h, w, w_forget)


def _swiglu_kernel(h_ref, wg_ref, wu_ref, o_ref):
    h = h_ref[...]
    gate = jnp.dot(h, wg_ref[...], preferred_element_type=F32)
    up = jnp.dot(h, wu_ref[...], preferred_element_type=F32)
    o_ref[...] = (gate * _sigmoid(gate) * up).astype(o_ref.dtype)


def _swiglu(h, w, layer, tm=1024, tn=512):
    M, D = h.shape
    F = w.shape[2] // 2
    tm, tn = _tile(M, tm), _tile(F, tn)
    nf = F // tn
    return pl.pallas_call(
        _swiglu_kernel,
        grid=(M // tm, nf),
        in_specs=[pl.BlockSpec((tm, D), lambda i, j: (i, 0)),
                  pl.BlockSpec((None, D, tn), lambda i, j: (layer, 0, j)),
                  pl.BlockSpec((None, D, tn), lambda i, j: (layer, 0, j + nf))],
        out_specs=pl.BlockSpec((tm, tn), lambda i, j: (i, j)),
        out_shape=jax.ShapeDtypeStruct((M, F), BF16),
        compiler_params=_params("parallel", "arbitrary"),
        name="swiglu",
    )(h, w, w)


def _matmul_res_kernel(a_ref, w_ref, r_ref, *rest, norm):
    out = r_ref[...] + jnp.dot(a_ref[...], w_ref[...], preferred_element_type=F32)
    if norm == "replace":
        g_ref, o_ref = rest
        o_ref[...] = _normalize_rows(out, g_ref[...])
    elif norm == "extra":
        g_ref, o_ref, h_ref = rest
        o_ref[...] = out
        h_ref[...] = _normalize_rows(out, g_ref[...]).astype(h_ref.dtype)
    else:
        rest[0][...] = out


def _matmul_res(a, w, layer, res, norm_gain=None, norm=None):
    M, K = a.shape
    N = w.shape[2]
    assert (norm_gain is None) == (norm is None)
    bytes_per_row = 2 * (2 * K + 4 * N + 4 * N + (2 * N if norm == "extra" else 0))
    tm = _rows_that_fit(M, 2 * K * N, bytes_per_row)
    row_tile = pl.BlockSpec((tm, N), lambda i: (i, 0))
    in_specs = [pl.BlockSpec((tm, K), lambda i: (i, 0)),
                pl.BlockSpec((None, K, N), lambda i: (layer, 0, 0), pipeline_mode=pl.Buffered(1)),
                row_tile]
    args = [a, w, res]
    out_specs, out_shape = [row_tile], [jax.ShapeDtypeStruct((M, N), F32)]
    if norm_gain is not None:
        in_specs.append(pl.BlockSpec((1, N), lambda i: (0, 0)))
        args.append(norm_gain)
    if norm == "extra":
        out_specs.append(row_tile)
        out_shape.append(jax.ShapeDtypeStruct((M, N), BF16))
    outs = pl.pallas_call(
        functools.partial(_matmul_res_kernel, norm=norm),
        grid=(M // tm,),
        in_specs=in_specs,
        out_specs=out_specs,
        out_shape=out_shape,
        compiler_params=_params("arbitrary"),
        name="matmul_res",
    )(*args)
    return outs if norm == "extra" else outs[0]


N_BIAS_COLS = 3
LOG2E = 1.4426950408889634


def _split3(x):
    hi = x.astype(BF16)
    rest = x - hi.astype(F32)
    mid = rest.astype(BF16)
    lo = (rest - mid.astype(F32)).astype(BF16)
    return jnp.concatenate([hi, mid, lo], axis=1)


def _forget_bias_kernel(f_ref, b_ref, tri_ref, sel_ref, o_ref, carry_ref):
    @pl.when(pl.program_id(1) == 0)
    def _():
        carry_ref[...] = jnp.zeros_like(carry_ref)

    x = _log_sigmoid(f_ref[...] + b_ref[...])
    cum3 = jnp.dot(tri_ref[...], _split3(x), preferred_element_type=F32)
    cum = (cum3[:, 0:LANES] + cum3[:, LANES:2 * LANES] + cum3[:, 2 * LANES:3 * LANES]
           + carry_ref[0:1, :])
    T = x.shape[0]
    carry_ref[0:1, :] = cum[T - 1:T, :]
    o_ref[...] = jnp.dot(_split3(cum * (-LOG2E)), sel_ref[...],
                         preferred_element_type=F32).astype(o_ref.dtype)


def _forget_bias(f, b, B, S, T=512):
    H = N_HEADS
    T = _tile(S, T)
    nt = S // T
    tri = jnp.tril(jnp.ones((T, T), BF16))
    r = jnp.arange(N_BIAS_COLS * LANES)[:, None]
    c = jnp.arange(H * LANES)[None, :]
    sel = ((r // LANES == c % LANES) & (r % LANES == c // LANES)).astype(BF16)
    return pl.pallas_call(
        _forget_bias_kernel,
        grid=(B, nt),
        in_specs=[pl.BlockSpec((T, LANES), lambda b_, t: (b_ * nt + t, 0)),
                  pl.BlockSpec((1, LANES), lambda b_, t: (0, 0)),
                  pl.BlockSpec((T, T), lambda b_, t: (0, 0)),
                  pl.BlockSpec(sel.shape, lambda b_, t: (0, 0))],
        out_specs=pl.BlockSpec((T, H * LANES), lambda b_, t: (b_ * nt + t, 0)),
        out_shape=jax.ShapeDtypeStruct((B * S, H * LANES), BF16),
        scratch_shapes=[pltpu.VMEM((8, LANES), F32)],
        compiler_params=_params("arbitrary", "arbitrary"),
        name="forget_bias",
    )(f, b, tri, sel)


def _attn_kernel(q_ref, k_ref, v_ref, kb_ref, o_ref, vt_ref, qt_ref, acc_ref, ml_ref, sbuf_ref, *,
                 tq, tk, unroll, ahead, tiles):
    nq = pl.num_programs(2) * tiles
    S = k_ref.shape[0]
    Dh = HEAD_DIM
    per_q = tq // tk
    assert 2 <= ahead <= per_q

    def one_tile(sub, carry):
        i = pl.program_id(2) * tiles + sub
        slot = i % 2

        def load_queries(tile, dst):
            rows = pl.multiple_of(tile * tq, tq)
            qt_ref[dst, 0:Dh, :] = q_ref[pl.ds(rows, tq), :].T

        def scores(j, qslot):
            start = pl.multiple_of(j * tk, tk)
            k_aug = jnp.concatenate([k_ref[pl.ds(start, tk), :], kb_ref[pl.ds(start, tk), :]],
                                    axis=1)
            return jnp.dot(k_aug, qt_ref[qslot], preferred_element_type=F32)

        @pl.when(i == 0)
        def _():
            for c in range(S // tq):
                vt_ref[:, c * tq:(c + 1) * tq] = v_ref[c * tq:(c + 1) * tq, :].T
            row = lax.broadcasted_iota(jnp.int32, (Dh, tq), 0)
            for dst in range(2):
                qt_ref[dst, Dh:2 * Dh, :] = jnp.where(row < N_BIAS_COLS, 1.0, 0.0).astype(BF16)
            load_queries(0, 0)
            for a in range(ahead):
                sbuf_ref[a] = scores(a, 0)

        load_queries(jnp.minimum(i + 1, nq - 1), 1 - slot)

        ml_ref[0:1, :] = jnp.full((1, tq), -jnp.inf, F32)
        ml_ref[1:2, :] = jnp.zeros((1, tq), F32)
        acc_ref[...] = jnp.zeros_like(acc_ref)

        def run(base, count, last=False):
            first_diag = count - per_q if last else count
            pending = {a: sbuf_ref[a] for a in range(ahead)}

            def max_pass(c, m_in):
                s = pending.pop(c)
                if c >= first_diag:
                    key = lax.broadcasted_iota(jnp.int32, (tk, tq), 0) + (c - first_diag) * tk
                    qry = lax.broadcasted_iota(jnp.int32, (tk, tq), 1)
                    s = jnp.where(key <= qry, s, -jnp.inf)
                return s, m_in, jnp.maximum(m_in, jnp.max(s, axis=0, keepdims=True))

            cur = max_pass(0, ml_ref[0:1, :])
            l = ml_ref[1:2, :]
            for c in range(count):
                if last and c + ahead >= count:
                    pending[c + ahead] = scores(c + ahead - count, 1 - slot)
                else:
                    pending[c + ahead] = scores(base + c + ahead, slot)
                nxt = max_pass(c + 1, cur[2]) if c + 1 < count else None
                s, m_old, m_new = cur
                p = jnp.exp2(s - m_new)
                alpha = jnp.exp2(m_old - m_new)
                l = alpha * l + jnp.sum(p, axis=0, keepdims=True)
                start = pl.multiple_of((base + c) * tk, tk)
                pv = jnp.dot(vt_ref[:, pl.ds(start, tk)], p.astype(BF16),
                             preferred_element_type=F32)
                acc_ref[...] = alpha * acc_ref[...] + pv
                cur = nxt
            ml_ref[0:1, :] = m_new
            ml_ref[1:2, :] = l
            for t in sorted(pending):
                sbuf_ref[t - count] = pending.pop(t)

        total = (i + 1) * per_q
        n_groups = (total - 1) // unroll

        def group(g, carry_):
            run(g * unroll, unroll)
            return carry_

        lax.fori_loop(0, n_groups, group, 0)

        tail = total - n_groups * unroll
        for size in range(per_q, unroll + 1, per_q):
            @pl.when(tail == size)
            def _(size=size):
                run(n_groups * unroll, size, last=True)

        rows = pl.multiple_of(sub * tq, tq)
        o_ref[pl.ds(rows, tq), :] = (acc_ref[...] / ml_ref[1:2, :]).T.astype(o_ref.dtype)
        return carry

    lax.fori_loop(0, tiles, one_tile, 0)


def _attention(qkv, kbias, B, S, tq=512, tk=256, unroll=16, ahead=2, tiles=8):
    H, Dh = N_HEADS, HEAD_DIM
    tq = _tile(S, tq)
    tk = _tile(tq, tk)
    unroll = max(unroll, tq // tk)
    tiles = _tile(S // tq, tiles)
    ns = S // (tq * tiles)
    return pl.pallas_call(
        functools.partial(_attn_kernel, tq=tq, tk=tk, unroll=unroll, ahead=ahead, tiles=tiles),
        grid=(B, H, ns),
        in_specs=[pl.BlockSpec((S, Dh), lambda b, h, i: (b, h)),
                  pl.BlockSpec((S, Dh), lambda b, h, i: (b, H + h)),
                  pl.BlockSpec((S, Dh), lambda b, h, i: (b, 2 * H + h)),
                  pl.BlockSpec((S, LANES), lambda b, h, i: (b, h))],
        out_specs=pl.BlockSpec((tq * tiles, Dh), lambda b, h, i: (b * ns + i, h)),
        out_shape=jax.ShapeDtypeStruct((B * S, H * Dh), BF16),
        scratch_shapes=[pltpu.VMEM((Dh, S), BF16), pltpu.VMEM((2, 2 * Dh, tq), BF16),
                        pltpu.VMEM((Dh, tq), F32), pltpu.VMEM((8, tq), F32),
                        pltpu.VMEM((ahead, tk, tq), F32)],
        compiler_params=_params("arbitrary", "arbitrary", "arbitrary"),
        name="fox_attention",
    )(qkv, qkv, qkv, kbias)


def _gelu_tanh(x):
    c = 0.7978845608028654
    return 0.5 * x * (1.0 + jnp.tanh(c * (x + 0.044715 * (x * x * x))))


def _rglru_kernel(x_ref, y_ref, cw_ref, cb_ref, wr_ref, br_ref, wi_ref, bi_ref, lam_ref,
                  o_ref, ext_ref, a_ref, b_ref, h_ref, state_ref, *, T):
    nb, _, C = x_ref.shape
    cb = C // N_RNN_BLOCKS

    @pl.when(pl.program_id(0) == 0)
    def _():
        ext_ref[:, 0:CONV_HALO, :] = jnp.zeros((nb, CONV_HALO, C), F32)
        state_ref[...] = jnp.zeros_like(state_ref)

    log_sig_lam = _log_sigmoid(lam_ref[...])
    for n_b in range(nb):
        x = x_ref[n_b]
        ext_ref[n_b, CONV_HALO:CONV_HALO + T, :] = x
        u = cb_ref[...] + x * cw_ref[CONV_WIDTH - 1:CONV_WIDTH, :]
        for k in range(CONV_WIDTH - 1):
            off = CONV_HALO + k - (CONV_WIDTH - 1)
            u = u + ext_ref[n_b, off:off + T, :] * cw_ref[k:k + 1, :]
        ext_ref[n_b, 0:CONV_HALO, :] = x[T - CONV_HALO:T, :]

        ub = u.astype(BF16)
        r_parts, i_parts = [], []
        for n in range(N_RNN_BLOCKS):
            un = ub[:, n * cb:(n + 1) * cb]
            r_parts.append(jnp.dot(un, wr_ref[n], preferred_element_type=F32))
            i_parts.append(jnp.dot(un, wi_ref[n], preferred_element_type=F32))
        r = _sigmoid(jnp.concatenate(r_parts, axis=1) + br_ref[...])
        ig = _sigmoid(jnp.concatenate(i_parts, axis=1) + bi_ref[...])
        log_a = (LRU_C * r) * log_sig_lam
        a = jnp.exp(log_a)
        a_ref[n_b] = a
        b_ref[n_b] = jnp.sqrt(-jnp.tanh(log_a) * (a * a + 1.0)) * (ig * u)

    def group(g, hs):
        base = pl.multiple_of(g * 8, 8)
        a8 = [a_ref[n_b, pl.ds(base, 8), :] for n_b in range(nb)]
        b8 = [b_ref[n_b, pl.ds(base, 8), :] for n_b in range(nb)]
        hs = list(hs)
        rows = [[] for _ in range(nb)]
        for s in range(8):
            for n_b in range(nb):
                hs[n_b] = a8[n_b][s:s + 1, :] * hs[n_b] + b8[n_b][s:s + 1, :]
                rows[n_b].append(hs[n_b])
        for n_b in range(nb):
            h_ref[n_b, pl.ds(base, 8), :] = jnp.concatenate(rows[n_b], axis=0)
        return tuple(hs)

    h_last = lax.fori_loop(0, T // 8, group, tuple(state_ref[n_b, 0:1, :] for n_b in range(nb)))
    for n_b in range(nb):
        state_ref[n_b, 0:1, :] = h_last[n_b]
        o_ref[n_b] = (_gelu_tanh(y_ref[n_b]) * h_ref[n_b]).astype(o_ref.dtype)


def _rglru(xy, conv_w, conv_b, w_rg, b_rg, w_ig, b_ig, lam, B, S, T=512):
    C = xy.shape[1] // 2
    T = _tile(S, T)
    xy = xy.reshape(B, S, 2 * C)
    vec = pl.BlockSpec((1, C), lambda t: (0, 0))
    blockdiag = pl.BlockSpec(w_rg.shape, lambda t: (0, 0, 0))
    out = pl.pallas_call(
        functools.partial(_rglru_kernel, T=T),
        grid=(S // T,),
        in_specs=[pl.BlockSpec((B, T, C), lambda t: (0, t, 0)),
                  pl.BlockSpec((B, T, C), lambda t: (0, t, 1)),
                  pl.BlockSpec((CONV_WIDTH, C), lambda t: (0, 0)),
                  vec, blockdiag, vec, blockdiag, vec, vec],
        out_specs=pl.BlockSpec((B, T, C), lambda t: (0, t, 0)),
        out_shape=jax.ShapeDtypeStruct((B, S, C), BF16),
        scratch_shapes=[pltpu.VMEM((B, T + CONV_HALO, C), F32),
                        pltpu.VMEM((B, T, C), F32), pltpu.VMEM((B, T, C), F32),
                        pltpu.VMEM((B, T, C), F32), pltpu.VMEM((B, 8, C), F32)],
        compiler_params=_params("arbitrary"),
        name="rglru",
    )(xy, xy, conv_w, conv_b, w_rg, b_rg, w_ig, b_ig, lam)
    return out.reshape(B * S, C)


def _pool_kernel(u_ref, w_ref, s_ref, o_ref, ext_ref, *, T):
    C = u_ref.shape[1]
    G = len(POOL_WINDOWS)
    cg = C // G
    t = pl.program_id(1)

    @pl.when(t == 0)
    def _():
        ext_ref[0:POOL_HALO, :] = jnp.zeros((POOL_HALO, C), F32)

    x = u_ref[...]
    ext_ref[POOL_HALO:POOL_HALO + T, :] = x
    pos = (t * T + 1 + lax.broadcasted_iota(jnp.int32, (T, 1), 0)).astype(F32)
    for g, w in enumerate(POOL_WINDOWS):
        lo = g * cg
        xs = x[:, lo:lo + cg]
        acc = xs
        for k in range(1, w):
            acc = acc + ext_ref[POOL_HALO - k:POOL_HALO - k + T, lo:lo + cg]
        pooled = acc / jnp.minimum(pos, float(w)) - xs
        y = jnp.dot(pooled.astype(BF16), w_ref[g], preferred_element_type=F32)
        o_ref[:, lo:lo + cg] = (y * s_ref[:, lo:lo + cg]).astype(o_ref.dtype)
    ext_ref[0:POOL_HALO, :] = x[T - POOL_HALO:T, :]


def _pool(u, w_pool, scale, B, S, T=512):
    C = u.shape[1]
    T = _tile(S, T)
    nt = S // T
    return pl.pallas_call(
        functools.partial(_pool_kernel, T=T),
        grid=(B, nt),
        in_specs=[pl.BlockSpec((T, C), lambda b, t: (b * nt + t, 0)),
                  pl.BlockSpec(w_pool.shape, lambda b, t: (0, 0, 0)),
                  pl.BlockSpec((1, C), lambda b, t: (0, 0))],
        out_specs=pl.BlockSpec((T, C), lambda b, t: (b * nt + t, 0)),
        out_shape=jax.ShapeDtypeStruct((B * S, C), BF16),
        scratch_shapes=[pltpu.VMEM((T + POOL_HALO, C), F32)],
        compiler_params=_params("arbitrary", "arbitrary"),
        name="multiscale_pool",
    )(u, w_pool, scale)


def _merge_kernel(ya_ref, yb_ref, yc_ref, wa_ref, wb_ref, wc_ref, ga_ref, gb_ref, gc_ref, o_ref):
    out = _sigmoid(ga_ref[...]) * jnp.dot(ya_ref[...], wa_ref[...], preferred_element_type=F32)
    out = out + _sigmoid(gb_ref[...]) * jnp.dot(yb_ref[...], wb_ref[...], preferred_element_type=F32)
    out = out + _sigmoid(gc_ref[...]) * jnp.dot(yc_ref[...], wc_ref[...], preferred_element_type=F32)
    o_ref[...] = out.astype(o_ref.dtype)


def _merge(ya, yb, yc, wa, wb, wc, layer, gates):
    M, C = ya.shape
    D = wa.shape[2]
    tm = _rows_that_fit(M, 2 * 3 * 2 * C * D, 2 * (3 * 2 * C + 3 * 4 * D + 2 * D))
    y_spec = pl.BlockSpec((tm, C), lambda i: (i, 0))
    w_spec = pl.BlockSpec((None, C, D), lambda i: (layer, 0, 0))
    return pl.pallas_call(
        _merge_kernel,
        grid=(M // tm,),
        in_specs=[y_spec, y_spec, y_spec, w_spec, w_spec, w_spec,
                  pl.BlockSpec((tm, D), lambda i: (i, 0)),
                  pl.BlockSpec((tm, D), lambda i: (i, 1)),
                  pl.BlockSpec((tm, D), lambda i: (i, 2))],
        out_specs=pl.BlockSpec((tm, D), lambda i: (i, 0)),
        out_shape=jax.ShapeDtypeStruct((M, D), BF16),
        compiler_params=_params("arbitrary"),
        name="gated_merge",
    )(ya, yb, yc, wa, wb, wc, gates, gates, gates)


def _layer(l, x, B, S, g_mix, w_in, b_forget, conv_w, conv_b, w_rg, b_rg, w_ig, b_ig, lam,
           w_pool, pool_scale, w_a, w_b, w_c, w_out, g_ffn, w_ffn_in, w_ffn_out, g_out=None):
    D = x.shape[1]
    H = N_HEADS
    d_rnn = conv_w.shape[1]
    d_attn = H * HEAD_DIM
    d_pool = pool_scale.shape[0]
    row = lambda v: v.reshape(1, -1)

    o_rnn = 2 * d_rnn
    o_qkv = o_rnn + 3 * d_attn
    o_f = o_qkv + H
    o_pool = o_f + d_pool
    w_f32 = jnp.concatenate([w_in[l, :, :o_rnn], w_in[l, :, o_f:]], axis=1)
    w_forget = jnp.pad(w_in[l, :, o_qkv:o_f], ((0, 0), (0, LANES - H)))
    qkv_scale = jnp.ones((1, 3 * d_attn), F32).at[:, :d_attn].set(HEAD_DIM ** -0.5 * LOG2E)
    w_qkv = w_in[l:l + 1, :, o_rnn:o_qkv]
    qkv, h = _norm_matmul(x, row(g_mix), w_qkv, 0, 0, 3 * d_attn, qkv_scale, BF16)
    xy, pool_in, gates, f_logit = _in_proj(h, w_f32, w_forget, (o_rnn, d_pool, 3 * D))

    y_a = _rglru(xy, conv_w, row(conv_b), w_rg.astype(BF16), row(b_rg), w_ig.astype(BF16),
                 row(b_ig), row(lam), B, S)

    kbias = _forget_bias(f_logit, jnp.pad(b_forget, (0, LANES - H)).reshape(1, LANES), B, S)
    y_b = _attention(qkv, kbias, B, S)

    y_c = _pool(pool_in, w_pool.astype(BF16), row(pool_scale), B, S)

    merged = _merge(y_a, y_b, y_c, w_a, w_b, w_c, l, gates)
    x, h2 = _matmul_res(merged, w_out, l, x, norm_gain=row(g_ffn), norm="extra")
    act = _swiglu(h2, w_ffn_in, l)
    return _matmul_res(act, w_ffn_out, l, x, norm_gain=g_out,
                       norm=None if g_out is None else "replace")


def kernel(x, g_mix, w_in, b_forget, conv_w, conv_b, w_rg, b_rg, w_ig, b_ig, lru_lambda, w_pool,
           pool_scale, w_branch_rnn, w_branch_attn, w_branch_pool, w_out, g_ffn, w_ffn_in,
           w_ffn_out, g_final):
    B, S, D = x.shape
    h = x.reshape(B * S, D)
    depth = g_mix.shape[0]
    big = [w.astype(BF16) for w in (w_in, w_branch_rnn, w_branch_attn, w_branch_pool, w_out,
                                    w_ffn_in, w_ffn_out)]
    w_in, w_branch_rnn, w_branch_attn, w_branch_pool, w_out, w_ffn_in, w_ffn_out = big
    for l in range(depth):
        h = _layer(l, h, B, S, g_mix[l], w_in, b_forget[l], conv_w[l], conv_b[l], w_rg[l], b_rg[l],
                   w_ig[l], b_ig[l], lru_lambda[l], w_pool[l], pool_scale[l], w_branch_rnn,
                   w_branch_attn, w_branch_pool, w_out, g_ffn[l], w_ffn_in, w_ffn_out,
                   g_out=g_final.reshape(1, -1) if l == depth - 1 else None)
    return h.reshape(B, S, D)
```

```python
import functools

import jax
import jax.numpy as jnp
from jax import lax
from jax.experimental import pallas as pl
from jax.experimental.pallas import tpu as pltpu

F32 = jnp.float32
BF16 = jnp.bfloat16

N_RNN_BLOCKS = 8
CONV_WIDTH = 4
LRU_C = 8.0
N_HEADS = 8
HEAD_DIM = 128
POOL_WINDOWS = (2, 4, 8, 16)
POOL_HALO = 16
CONV_HALO = 8
NORM_EPS = 1e-6
LANES = 128

MIB = 1024 * 1024
VMEM_LIMIT = 56 * MIB
VMEM_BLOCK_BUDGET = 45 * MIB


def _params(*sem):
    return pltpu.CompilerParams(dimension_semantics=sem, vmem_limit_bytes=VMEM_LIMIT)


def _rows_that_fit(m, fixed_bytes, bytes_per_row, cap=1024):
    t = _tile(m, cap)
    while t > 8 and fixed_bytes + t * bytes_per_row > VMEM_BLOCK_BUDGET:
        t //= 2
    return t


def _tile(n, want):
    t = min(n, want)
    while n % t:
        t //= 2
    return t


def _sigmoid(x):
    return 0.5 * jnp.tanh(0.5 * x) + 0.5


def _log_sigmoid(x):
    return jnp.minimum(x, 0.0) - jnp.log1p(jnp.exp(-jnp.abs(x)))


def _normalize_rows(x, g):
    ms = jnp.mean(x * x, axis=-1, keepdims=True)
    return x * lax.rsqrt(ms + NORM_EPS) * g


def _norm_matmul_kernel(x_ref, g_ref, w_ref, c_ref, o_ref, h_ref):
    @pl.when(pl.program_id(1) == 0)
    def _():
        h_ref[...] = _normalize_rows(x_ref[...], g_ref[...]).astype(BF16)

    acc = jnp.dot(h_ref[...], w_ref[...], preferred_element_type=F32)
    o_ref[...] = (acc * c_ref[...]).astype(o_ref.dtype)


def _norm_matmul(x, g, w, layer, col0, N, colscale, out_dtype, tm=1024, tn=1024):
    M, D = x.shape
    tm, tn = _tile(M, tm), _tile(N, tn)
    assert col0 % tn == 0
    j0 = col0 // tn
    return pl.pallas_call(
        _norm_matmul_kernel,
        grid=(M // tm, N // tn),
        in_specs=[pl.BlockSpec((tm, D), lambda i, j: (i, 0)),
                  pl.BlockSpec((1, D), lambda i, j: (0, 0)),
                  pl.BlockSpec((None, D, tn), lambda i, j: (layer, 0, j0 + j)),
                  pl.BlockSpec((1, tn), lambda i, j: (0, j))],
        out_specs=[pl.BlockSpec((tm, tn), lambda i, j: (i, j)),
                   pl.BlockSpec((tm, D), lambda i, j: (i, 0))],
        out_shape=[jax.ShapeDtypeStruct((M, N), out_dtype), jax.ShapeDtypeStruct((M, D), BF16)],
        compiler_params=_params("arbitrary", "arbitrary"),
        name="norm_matmul",
    )(x, g, w, colscale)


def _in_proj_kernel(h_ref, w_ref, wf_ref, *rest, bounds):
    outs, f_ref = rest[:-1], rest[-1]
    j = pl.program_id(1)

    @pl.when(j == 0)
    def _():
        f_ref[...] = jnp.dot(h_ref[...], wf_ref[...], preferred_element_type=F32)

    acc = jnp.dot(h_ref[...], w_ref[...], preferred_element_type=F32)
    tm, tn = acc.shape
    rows = min(tm, tn)
    for k, o_ref in enumerate(outs):
        owns = jnp.broadcast_to((j >= bounds[k]) & (j < bounds[k + 1]), (rows, tn))
        for r in range(0, tm, rows):
            pltpu.store(o_ref.at[r:r + rows, :], acc[r:r + rows, :], mask=owns)


def _in_proj(h, w, w_forget, widths, tm=1024, tn=1024):
    M, D = h.shape
    N = w.shape[1]
    tm = _tile(M, tm)
    assert all(wd % tn == 0 for wd in widths) and sum(widths) == N
    bounds = [0]
    for wd in widths:
        bounds.append(bounds[-1] + wd // tn)

    def out_map(k):
        lo, n = bounds[k], bounds[k + 1] - bounds[k]
        return lambda i, j: (i, jnp.clip(j - lo, 0, n - 1))

    out_specs = [pl.BlockSpec((tm, tn), out_map(k)) for k in range(len(widths))]
    out_shape = [jax.ShapeDtypeStruct((M, wd), F32) for wd in widths]
    out_specs.append(pl.BlockSpec((tm, LANES), lambda i, j: (i, 0)))
    out_shape.append(jax.ShapeDtypeStruct((M, LANES), F32))
    return pl.pallas_call(
        functools.partial(_in_proj_kernel, bounds=tuple(bounds)),
        grid=(M // tm, N // tn),
        in_specs=[pl.BlockSpec((tm, D), lambda i, j: (i, 0)),
                  pl.BlockSpec((D, tn), lambda i, j: (0, j)),
                  pl.BlockSpec((D, LANES), lambda i, j: (0, 0))],
        out_specs=out_specs,
        out_shape=out_shape,
        compiler_params=_params("arbitrary", "arbitrary"),
        name="in_proj",
    )(h, w, w_forget)


def _swiglu_kernel(h_ref, wg_ref, wu_ref, o_ref):
    h = h_ref[...]
    gate = jnp.dot(h, wg_ref[...], preferred_element_type=F32)
    up = jnp.dot(h, wu_ref[...], preferred_element_type=F32)
    o_ref[...] = (gate * _sigmoid(gate) * up).astype(o_ref.dtype)


def _swiglu(h, w, layer, tm=1024, tn=512):
    M, D = h.shape
    F = w.shape[2] // 2
    tm, tn = _tile(M, tm), _tile(F, tn)
    nf = F // tn
    return pl.pallas_call(
        _swiglu_kernel,
        grid=(M // tm, nf),
        in_specs=[pl.BlockSpec((tm, D), lambda i, j: (i, 0)),
                  pl.BlockSpec((None, D, tn), lambda i, j: (layer, 0, j)),
                  pl.BlockSpec((None, D, tn), lambda i, j: (layer, 0, j + nf))],
        out_specs=pl.BlockSpec((tm, tn), lambda i, j: (i, j)),
        out_shape=jax.ShapeDtypeStruct((M, F), BF16),
        compiler_params=_params("parallel", "arbitrary"),
        name="swiglu",
    )(h, w, w)


def _matmul_res_kernel(a_ref, w_ref, r_ref, *rest, norm):
    out = r_ref[...] + jnp.dot(a_ref[...], w_ref[...], preferred_element_type=F32)
    if norm == "replace":
        g_ref, o_ref = rest
        o_ref[...] = _normalize_rows(out, g_ref[...])
    elif norm == "extra":
        g_ref, o_ref, h_ref = rest
        o_ref[...] = out
        h_ref[...] = _normalize_rows(out, g_ref[...]).astype(h_ref.dtype)
    else:
        rest[0][...] = out


def _matmul_res(a, w, layer, res, norm_gain=None, norm=None):
    M, K = a.shape
    N = w.shape[2]
    assert (norm_gain is None) == (norm is None)
    bytes_per_row = 2 * (2 * K + 4 * N + 4 * N + (2 * N if norm == "extra" else 0))
    tm = _rows_that_fit(M, 2 * K * N, bytes_per_row)
    row_tile = pl.BlockSpec((tm, N), lambda i: (i, 0))
    in_specs = [pl.BlockSpec((tm, K), lambda i: (i, 0)),
                pl.BlockSpec((None, K, N), lambda i: (layer, 0, 0), pipeline_mode=pl.Buffered(1)),
                row_tile]
    args = [a, w, res]
    out_specs, out_shape = [row_tile], [jax.ShapeDtypeStruct((M, N), F32)]
    if norm_gain is not None:
        in_specs.append(pl.BlockSpec((1, N), lambda i: (0, 0)))
        args.append(norm_gain)
    if norm == "extra":
        out_specs.append(row_tile)
        out_shape.append(jax.ShapeDtypeStruct((M, N), BF16))
    outs = pl.pallas_call(
        functools.partial(_matmul_res_kernel, norm=norm),
        grid=(M // tm,),
        in_specs=in_specs,
        out_specs=out_specs,
        out_shape=out_shape,
        compiler_params=_params("arbitrary"),
        name="matmul_res",
    )(*args)
    return outs if norm == "extra" else outs[0]


N_BIAS_COLS = 3
LOG2E = 1.4426950408889634


def _split3(x):
    hi = x.astype(BF16)
    rest = x - hi.astype(F32)
    mid = rest.astype(BF16)
    lo = (rest - mid.astype(F32)).astype(BF16)
    return jnp.concatenate([hi, mid, lo], axis=1)


def _forget_bias_kernel(f_ref, b_ref, tri_ref, sel_ref, o_ref, carry_ref):
    @pl.when(pl.program_id(1) == 0)
    def _():
        carry_ref[...] = jnp.zeros_like(carry_ref)

    x = _log_sigmoid(f_ref[...] + b_ref[...])
    cum3 = jnp.dot(tri_ref[...], _split3(x), preferred_element_type=F32)
    cum = (cum3[:, 0:LANES] + cum3[:, LANES:2 * LANES] + cum3[:, 2 * LANES:3 * LANES]
           + carry_ref[0:1, :])
    T = x.shape[0]
    carry_ref[0:1, :] = cum[T - 1:T, :]
    o_ref[...] = jnp.dot(_split3(cum * (-LOG2E)), sel_ref[...],
                         preferred_element_type=F32).astype(o_ref.dtype)


def _forget_bias(f, b, B, S, T=512):
    H = N_HEADS
    T = _tile(S, T)
    nt = S // T
    tri = jnp.tril(jnp.ones((T, T), BF16))
    r = jnp.arange(N_BIAS_COLS * LANES)[:, None]
    c = jnp.arange(H * LANES)[None, :]
    sel = ((r // LANES == c % LANES) & (r % LANES == c // LANES)).astype(BF16)
    return pl.pallas_call(
        _forget_bias_kernel,
        grid=(B, nt),
        in_specs=[pl.BlockSpec((T, LANES), lambda b_, t: (b_ * nt + t, 0)),
                  pl.BlockSpec((1, LANES), lambda b_, t: (0, 0)),
                  pl.BlockSpec((T, T), lambda b_, t: (0, 0)),
                  pl.BlockSpec(sel.shape, lambda b_, t: (0, 0))],
        out_specs=pl.BlockSpec((T, H * LANES), lambda b_, t: (b_ * nt + t, 0)),
        out_shape=jax.ShapeDtypeStruct((B * S, H * LANES), BF16),
        scratch_shapes=[pltpu.VMEM((8, LANES), F32)],
        compiler_params=_params("arbitrary", "arbitrary"),
        name="forget_bias",
    )(f, b, tri, sel)


def _attn_kernel(q_ref, k_ref, v_ref, kb_ref, o_ref, vt_ref, qt_ref, acc_ref, ml_ref, sbuf_ref, *,
                 tq, tk, unroll, ahead, tiles):
    nq = pl.num_programs(2) * tiles
    S = k_ref.shape[0]
    Dh = HEAD_DIM
    per_q = tq // tk
    assert 2 <= ahead <= per_q

    def one_tile(sub, carry):
        i = pl.program_id(2) * tiles + sub
        slot = i % 2

        def load_queries(tile, dst):
            rows = pl.multiple_of(tile * tq, tq)
            qt_ref[dst, 0:Dh, :] = q_ref[pl.ds(rows, tq), :].T

        def scores(j, qslot):
            start = pl.multiple_of(j * tk, tk)
            k_aug = jnp.concatenate([k_ref[pl.ds(start, tk), :], kb_ref[pl.ds(start, tk), :]],
                                    axis=1)
            return jnp.dot(k_aug, qt_ref[qslot], preferred_element_type=F32)

        @pl.when(i == 0)
        def _():
            for c in range(S // tq):
                vt_ref[:, c * tq:(c + 1) * tq] = v_ref[c * tq:(c + 1) * tq, :].T
            row = lax.broadcasted_iota(jnp.int32, (Dh, tq), 0)
            for dst in range(2):
                qt_ref[dst, Dh:2 * Dh, :] = jnp.where(row < N_BIAS_COLS, 1.0, 0.0).astype(BF16)
            load_queries(0, 0)
            for a in range(ahead):
                sbuf_ref[a] = scores(a, 0)

        load_queries(jnp.minimum(i + 1, nq - 1), 1 - slot)

        ml_ref[0:1, :] = jnp.full((1, tq), -jnp.inf, F32)
        ml_ref[1:2, :] = jnp.zeros((1, tq), F32)
        acc_ref[...] = jnp.zeros_like(acc_ref)

        def run(base, count, last=False):
            first_diag = count - per_q if last else count
            pending = {a: sbuf_ref[a] for a in range(ahead)}

            def max_pass(c, m_in):
                s = pending.pop(c)
                if c >= first_diag:
                    key = lax.broadcasted_iota(jnp.int32, (tk, tq), 0) + (c - first_diag) * tk
                    qry = lax.broadcasted_iota(jnp.int32, (tk, tq), 1)
                    s = jnp.where(key <= qry, s, -jnp.inf)
                return s, m_in, jnp.maximum(m_in, jnp.max(s, axis=0, keepdims=True))

            cur = max_pass(0, ml_ref[0:1, :])
            l = ml_ref[1:2, :]
            for c in range(count):
                if last and c + ahead >= count:
                    pending[c + ahead] = scores(c + ahead - count, 1 - slot)
                else:
                    pending[c + ahead] = scores(base + c + ahead, slot)
                nxt = max_pass(c + 1, cur[2]) if c + 1 < count else None
                s, m_old, m_new = cur
                p = jnp.exp2(s - m_new)
                alpha = jnp.exp2(m_old - m_new)
                l = alpha * l + jnp.sum(p, axis=0, keepdims=True)
                start = pl.multiple_of((base + c) * tk, tk)
                pv = jnp.dot(vt_ref[:, pl.ds(start, tk)], p.astype(BF16),
                             preferred_element_type=F32)
                acc_ref[...] = alpha * acc_ref[...] + pv
                cur = nxt
            ml_ref[0:1, :] = m_new
            ml_ref[1:2, :] = l
            for t in sorted(pending):
                sbuf_ref[t - count] = pending.pop(t)

        total = (i + 1) * per_q
        n_big = (total - 1) // (2 * unroll)

        def group(g, carry_):
            run(g * 2 * unroll, 2 * unroll)
            return carry_

        lax.fori_loop(0, n_big, group, 0)

        done = n_big * 2 * unroll
        has_mid = (total - done - 1) // unroll

        @pl.when(has_mid == 1)
        def _():
            run(done, unroll)

        done = done + has_mid * unroll
        tail = total - done
        for size in range(per_q, unroll + 1, per_q):
            @pl.when(tail == size)
            def _(size=size):
                run(done, size, last=True)

        rows = pl.multiple_of(sub * tq, tq)
        o_ref[pl.ds(rows, tq), :] = (acc_ref[...] / ml_ref[1:2, :]).T.astype(o_ref.dtype)
        return carry

    lax.fori_loop(0, tiles, one_tile, 0)


def _attention(qkv, kbias, B, S, tq=512, tk=256, unroll=16, ahead=2, tiles=8):
    H, Dh = N_HEADS, HEAD_DIM
    tq = _tile(S, tq)
    tk = _tile(tq, tk)
    unroll = max(unroll, tq // tk)
    tiles = _tile(S // tq, tiles)
    ns = S // (tq * tiles)
    return pl.pallas_call(
        functools.partial(_attn_kernel, tq=tq, tk=tk, unroll=unroll, ahead=ahead, tiles=tiles),
        grid=(B, H, ns),
        in_specs=[pl.BlockSpec((S, Dh), lambda b, h, i: (b, h)),
                  pl.BlockSpec((S, Dh), lambda b, h, i: (b, H + h)),
                  pl.BlockSpec((S, Dh), lambda b, h, i: (b, 2 * H + h)),
                  pl.BlockSpec((S, LANES), lambda b, h, i: (b, h))],
        out_specs=pl.BlockSpec((tq * tiles, Dh), lambda b, h, i: (b * ns + i, h)),
        out_shape=jax.ShapeDtypeStruct((B * S, H * Dh), BF16),
        scratch_shapes=[pltpu.VMEM((Dh, S), BF16), pltpu.VMEM((2, 2 * Dh, tq), BF16),
                        pltpu.VMEM((Dh, tq), F32), pltpu.VMEM((8, tq), F32),
                        pltpu.VMEM((ahead, tk, tq), F32)],
        compiler_params=_params("arbitrary", "arbitrary", "arbitrary"),
        name="fox_attention",
    )(qkv, qkv, qkv, kbias)


def _gelu_tanh(x):
    c = 0.7978845608028654
    return 0.5 * x * (1.0 + jnp.tanh(c * (x + 0.044715 * (x * x * x))))


def _rglru_kernel(x_ref, y_ref, cw_ref, cb_ref, wr_ref, br_ref, wi_ref, bi_ref, lam_ref,
                  o_ref, ext_ref, a_ref, b_ref, h_ref, state_ref, *, T):
    nb, _, C = x_ref.shape
    cb = C // N_RNN_BLOCKS

    @pl.when(pl.program_id(0) == 0)
    def _():
        ext_ref[:, 0:CONV_HALO, :] = jnp.zeros((nb, CONV_HALO, C), F32)
        state_ref[...] = jnp.zeros_like(state_ref)

    log_sig_lam = _log_sigmoid(lam_ref[...])
    for n_b in range(nb):
        x = x_ref[n_b]
        ext_ref[n_b, CONV_HALO:CONV_HALO + T, :] = x
        u = cb_ref[...] + x * cw_ref[CONV_WIDTH - 1:CONV_WIDTH, :]
        for k in range(CONV_WIDTH - 1):
            off = CONV_HALO + k - (CONV_WIDTH - 1)
            u = u + ext_ref[n_b, off:off + T, :] * cw_ref[k:k + 1, :]
        ext_ref[n_b, 0:CONV_HALO, :] = x[T - CONV_HALO:T, :]

        ub = u.astype(BF16)
        r_parts, i_parts = [], []
        for n in range(N_RNN_BLOCKS):
            un = ub[:, n * cb:(n + 1) * cb]
            r_parts.append(jnp.dot(un, wr_ref[n], preferred_element_type=F32))
            i_parts.append(jnp.dot(un, wi_ref[n], preferred_element_type=F32))
        r = _sigmoid(jnp.concatenate(r_parts, axis=1) + br_ref[...])
        ig = _sigmoid(jnp.concatenate(i_parts, axis=1) + bi_ref[...])
        log_a = (LRU_C * r) * log_sig_lam
        a = jnp.exp(log_a)
        a_ref[n_b] = a
        b_ref[n_b] = jnp.sqrt(-jnp.tanh(log_a) * (a * a + 1.0)) * (ig * u)

    def group(g, hs):
        base = pl.multiple_of(g * 8, 8)
        a8 = [a_ref[n_b, pl.ds(base, 8), :] for n_b in range(nb)]
        b8 = [b_ref[n_b, pl.ds(base, 8), :] for n_b in range(nb)]
        hs = list(hs)
        rows = [[] for _ in range(nb)]
        for s in range(8):
            for n_b in range(nb):
                hs[n_b] = a8[n_b][s:s + 1, :] * hs[n_b] + b8[n_b][s:s + 1, :]
                rows[n_b].append(hs[n_b])
        for n_b in range(nb):
            h_ref[n_b, pl.ds(base, 8), :] = jnp.concatenate(rows[n_b], axis=0)
        return tuple(hs)

    h_last = lax.fori_loop(0, T // 8, group, tuple(state_ref[n_b, 0:1, :] for n_b in range(nb)))
    for n_b in range(nb):
        state_ref[n_b, 0:1, :] = h_last[n_b]
        o_ref[n_b] = (_gelu_tanh(y_ref[n_b]) * h_ref[n_b]).astype(o_ref.dtype)


def _rglru(xy, conv_w, conv_b, w_rg, b_rg, w_ig, b_ig, lam, B, S, T=512):
    C = xy.shape[1] // 2
    T = _tile(S, T)
    xy = xy.reshape(B, S, 2 * C)
    vec = pl.BlockSpec((1, C), lambda t: (0, 0))
    blockdiag = pl.BlockSpec(w_rg.shape, lambda t: (0, 0, 0))
    out = pl.pallas_call(
        functools.partial(_rglru_kernel, T=T),
        grid=(S // T,),
        in_specs=[pl.BlockSpec((B, T, C), lambda t: (0, t, 0)),
                  pl.BlockSpec((B, T, C), lambda t: (0, t, 1)),
                  pl.BlockSpec((CONV_WIDTH, C), lambda t: (0, 0)),
                  vec, blockdiag, vec, blockdiag, vec, vec],
        out_specs=pl.BlockSpec((B, T, C), lambda t: (0, t, 0)),
        out_shape=jax.ShapeDtypeStruct((B, S, C), BF16),
        scratch_shapes=[pltpu.VMEM((B, T + CONV_HALO, C), F32),
                        pltpu.VMEM((B, T, C), F32), pltpu.VMEM((B, T, C), F32),
                        pltpu.VMEM((B, T, C), F32), pltpu.VMEM((B, 8, C), F32)],
        compiler_params=_params("arbitrary"),
        name="rglru",
    )(xy, xy, conv_w, conv_b, w_rg, b_rg, w_ig, b_ig, lam)
    return out.reshape(B * S, C)


def _pool_kernel(u_ref, w_ref, s_ref, o_ref, ext_ref, *, T):
    C = u_ref.shape[1]
    G = len(POOL_WINDOWS)
    cg = C // G
    t = pl.program_id(1)

    @pl.when(t == 0)
    def _():
        ext_ref[0:POOL_HALO, :] = jnp.zeros((POOL_HALO, C), F32)

    x = u_ref[...]
    ext_ref[POOL_HALO:POOL_HALO + T, :] = x
    pos = (t * T + 1 + lax.broadcasted_iota(jnp.int32, (T, 1), 0)).astype(F32)
    for g, w in enumerate(POOL_WINDOWS):
        lo = g * cg
        xs = x[:, lo:lo + cg]
        acc = xs
        for k in range(1, w):
            acc = acc + ext_ref[POOL_HALO - k:POOL_HALO - k + T, lo:lo + cg]
        pooled = acc / jnp.minimum(pos, float(w)) - xs
        y = jnp.dot(pooled.astype(BF16), w_ref[g], preferred_element_type=F32)
        o_ref[:, lo:lo + cg] = (y * s_ref[:, lo:lo + cg]).astype(o_ref.dtype)
    ext_ref[0:POOL_HALO, :] = x[T - POOL_HALO:T, :]


def _pool(u, w_pool, scale, B, S, T=512):
    C = u.shape[1]
    T = _tile(S, T)
    nt = S // T
    return pl.pallas_call(
        functools.partial(_pool_kernel, T=T),
        grid=(B, nt),
        in_specs=[pl.BlockSpec((T, C), lambda b, t: (b * nt + t, 0)),
                  pl.BlockSpec(w_pool.shape, lambda b, t: (0, 0, 0)),
                  pl.BlockSpec((1, C), lambda b, t: (0, 0))],
        out_specs=pl.BlockSpec((T, C), lambda b, t: (b * nt + t, 0)),
        out_shape=jax.ShapeDtypeStruct((B * S, C), BF16),
        scratch_shapes=[pltpu.VMEM((T + POOL_HALO, C), F32)],
        compiler_params=_params("arbitrary", "arbitrary"),
        name="multiscale_pool",
    )(u, w_pool, scale)


def _merge_kernel(ya_ref, yb_ref, yc_ref, wa_ref, wb_ref, wc_ref, ga_ref, gb_ref, gc_ref, o_ref):
    out = _sigmoid(ga_ref[...]) * jnp.dot(ya_ref[...], wa_ref[...], preferred_element_type=F32)
    out = out + _sigmoid(gb_ref[...]) * jnp.dot(yb_ref[...], wb_ref[...], preferred_element_type=F32)
    out = out + _sigmoid(gc_ref[...]) * jnp.dot(yc_ref[...], wc_ref[...], preferred_element_type=F32)
    o_ref[...] = out.astype(o_ref.dtype)


def _merge(ya, yb, yc, wa, wb, wc, layer, gates):
    M, C = ya.shape
    D = wa.shape[2]
    tm = _rows_that_fit(M, 2 * 3 * 2 * C * D, 2 * (3 * 2 * C + 3 * 4 * D + 2 * D))
    y_spec = pl.BlockSpec((tm, C), lambda i: (i, 0))
    w_spec = pl.BlockSpec((None, C, D), lambda i: (layer, 0, 0))
    return pl.pallas_call(
        _merge_kernel,
        grid=(M // tm,),
        in_specs=[y_spec, y_spec, y_spec, w_spec, w_spec, w_spec,
                  pl.BlockSpec((tm, D), lambda i: (i, 0)),
                  pl.BlockSpec((tm, D), lambda i: (i, 1)),
                  pl.BlockSpec((tm, D), lambda i: (i, 2))],
        out_specs=pl.BlockSpec((tm, D), lambda i: (i, 0)),
        out_shape=jax.ShapeDtypeStruct((M, D), BF16),
        compiler_params=_params("arbitrary"),
        name="gated_merge",
    )(ya, yb, yc, wa, wb, wc, gates, gates, gates)


def _layer(l, x, B, S, g_mix, w_in, b_forget, conv_w, conv_b, w_rg, b_rg, w_ig, b_ig, lam,
           w_pool, pool_scale, w_a, w_b, w_c, w_out, g_ffn, w_ffn_in, w_ffn_out, g_out=None):
    D = x.shape[1]
    H = N_HEADS
    d_rnn = conv_w.shape[1]
    d_attn = H * HEAD_DIM
    d_pool = pool_scale.shape[0]
    row = lambda v: v.reshape(1, -1)

    o_rnn = 2 * d_rnn
    o_qkv = o_rnn + 3 * d_attn
    o_f = o_qkv + H
    o_pool = o_f + d_pool
    w_f32 = jnp.concatenate([w_in[l, :, :o_rnn], w_in[l, :, o_f:]], axis=1)
    w_forget = jnp.pad(w_in[l, :, o_qkv:o_f], ((0, 0), (0, LANES - H)))
    qkv_scale = jnp.ones((1, 3 * d_attn), F32).at[:, :d_attn].set(HEAD_DIM ** -0.5 * LOG2E)
    w_qkv = w_in[l:l + 1, :, o_rnn:o_qkv]
    qkv, h = _norm_matmul(x, row(g_mix), w_qkv, 0, 0, 3 * d_attn, qkv_scale, BF16)
    xy, pool_in, gates, f_logit = _in_proj(h, w_f32, w_forget, (o_rnn, d_pool, 3 * D))

    y_a = _rglru(xy, conv_w, row(conv_b), w_rg.astype(BF16), row(b_rg), w_ig.astype(BF16),
                 row(b_ig), row(lam), B, S)

    kbias = _forget_bias(f_logit, jnp.pad(b_forget, (0, LANES - H)).reshape(1, LANES), B, S)
    y_b = _attention(qkv, kbias, B, S)

    y_c = _pool(pool_in, w_pool.astype(BF16), row(pool_scale), B, S)

    merged = _merge(y_a, y_b, y_c, w_a, w_b, w_c, l, gates)
    x, h2 = _matmul_res(merged, w_out, l, x, norm_gain=row(g_ffn), norm="extra")
    act = _swiglu(h2, w_ffn_in, l)
    return _matmul_res(act, w_ffn_out, l, x, norm_gain=g_out,
                       norm=None if g_out is None else "replace")


def kernel(x, g_mix, w_in, b_forget, conv_w, conv_b, w_rg, b_rg, w_ig, b_ig, lru_lambda, w_pool,
           pool_scale, w_branch_rnn, w_branch_attn, w_branch_pool, w_out, g_ffn, w_ffn_in,
           w_ffn_out, g_final):
    B, S, D = x.shape
    h = x.reshape(B * S, D)
    depth = g_mix.shape[0]
    big = [w.astype(BF16) for w in (w_in, w_branch_rnn, w_branch_attn, w_branch_pool, w_out,
                                    w_ffn_in, w_ffn_out)]
    w_in, w_branch_rnn, w_branch_attn, w_branch_pool, w_out, w_ffn_in, w_ffn_out = big
    for l in range(depth):
        h = _layer(l, h, B, S, g_mix[l], w_in, b_forget[l], conv_w[l], conv_b[l], w_rg[l], b_rg[l],
                   w_ig[l], b_ig[l], lru_lambda[l], w_pool[l], pool_scale[l], w_branch_rnn,
                   w_branch_attn, w_branch_pool, w_out, g_ffn[l], w_ffn_in, w_ffn_out,
                   g_out=g_final.reshape(1, -1) if l == depth - 1 else None)
    return h.reshape(B, S, D)
```

```python
import functools

import jax
import jax.numpy as jnp
from jax import lax
from jax.experimental import pallas as pl
from jax.experimental.pallas import tpu as pltpu

F32 = jnp.float32
BF16 = jnp.bfloat16

N_RNN_BLOCKS = 8
CONV_WIDTH = 4
LRU_C = 8.0
N_HEADS = 8
HEAD_DIM = 128
POOL_WINDOWS = (2, 4, 8, 16)
POOL_HALO = 16
CONV_HALO = 8
NORM_EPS = 1e-6
LANES = 128

MIB = 1024 * 1024
VMEM_LIMIT = 56 * MIB
VMEM_BLOCK_BUDGET = 45 * MIB


def _params(*sem):
    return pltpu.CompilerParams(dimension_semantics=sem, vmem_limit_bytes=VMEM_LIMIT)


def _rows_that_fit(m, fixed_bytes, bytes_per_row, cap=1024):
    t = _tile(m, cap)
    while t > 8 and fixed_bytes + t * bytes_per_row > VMEM_BLOCK_BUDGET:
        t //= 2
    return t


def _tile(n, want):
    t = min(n, want)
    while n % t:
        t //= 2
    return t


def _sigmoid(x):
    return 0.5 * jnp.tanh(0.5 * x) + 0.5


def _log_sigmoid(x):
    return jnp.minimum(x, 0.0) - jnp.log1p(jnp.exp(-jnp.abs(x)))


def _normalize_rows(x, g):
    ms = jnp.mean(x * x, axis=-1, keepdims=True)
    return x * lax.rsqrt(ms + NORM_EPS) * g


def _norm_matmul_kernel(x_ref, g_ref, w_ref, c_ref, o_ref, h_ref):
    @pl.when(pl.program_id(1) == 0)
    def _():
        h_ref[...] = _normalize_rows(x_ref[...], g_ref[...]).astype(BF16)

    acc = jnp.dot(h_ref[...], w_ref[...], preferred_element_type=F32)
    o_ref[...] = (acc * c_ref[...]).astype(o_ref.dtype)


def _norm_matmul(x, g, w, layer, col0, N, colscale, out_dtype, tm=1024, tn=1024):
    M, D = x.shape
    tm, tn = _tile(M, tm), _tile(N, tn)
    assert col0 % tn == 0
    j0 = col0 // tn
    return pl.pallas_call(
        _norm_matmul_kernel,
        grid=(M // tm, N // tn),
        in_specs=[pl.BlockSpec((tm, D), lambda i, j: (i, 0)),
                  pl.BlockSpec((1, D), lambda i, j: (0, 0)),
                  pl.BlockSpec((None, D, tn), lambda i, j: (layer, 0, j0 + j)),
                  pl.BlockSpec((1, tn), lambda i, j: (0, j))],
        out_specs=[pl.BlockSpec((tm, tn), lambda i, j: (i, j)),
                   pl.BlockSpec((tm, D), lambda i, j: (i, 0))],
        out_shape=[jax.ShapeDtypeStruct((M, N), out_dtype), jax.ShapeDtypeStruct((M, D), BF16)],
        compiler_params=_params("arbitrary", "arbitrary"),
        name="norm_matmul",
    )(x, g, w, colscale)


def _in_proj_kernel(h_ref, w_ref, wf_ref, *rest, bounds):
    outs, f_ref = rest[:-1], rest[-1]
    j = pl.program_id(1)

    @pl.when(j == 0)
    def _():
        f_ref[...] = jnp.dot(h_ref[...], wf_ref[...], preferred_element_type=F32)

    acc = jnp.dot(h_ref[...], w_ref[...], preferred_element_type=F32)
    tm, tn = acc.shape
    rows = min(tm, tn)
    for k, o_ref in enumerate(outs):
        owns = jnp.broadcast_to((j >= bounds[k]) & (j < bounds[k + 1]), (rows, tn))
        for r in range(0, tm, rows):
            pltpu.store(o_ref.at[r:r + rows, :], acc[r:r + rows, :], mask=owns)


def _in_proj(h, w, w_forget, widths, tm=1024, tn=1024):
    M, D = h.shape
    N = w.shape[1]
    tm = _tile(M, tm)
    assert all(wd % tn == 0 for wd in widths) and sum(widths) == N
    bounds = [0]
    for wd in widths:
        bounds.append(bounds[-1] + wd // tn)

    def out_map(k):
        lo, n = bounds[k], bounds[k + 1] - bounds[k]
        return lambda i, j: (i, jnp.clip(j - lo, 0, n - 1))

    out_specs = [pl.BlockSpec((tm, tn), out_map(k)) for k in range(len(widths))]
    out_shape = [jax.ShapeDtypeStruct((M, wd), F32) for wd in widths]
    out_specs.append(pl.BlockSpec((tm, LANES), lambda i, j: (i, 0)))
    out_shape.append(jax.ShapeDtypeStruct((M, LANES), F32))
    return pl.pallas_call(
        functools.partial(_in_proj_kernel, bounds=tuple(bounds)),
        grid=(M // tm, N // tn),
        in_specs=[pl.BlockSpec((tm, D), lambda i, j: (i, 0)),
                  pl.BlockSpec((D, tn), lambda i, j: (0, j)),
                  pl.BlockSpec((D, LANES), lambda i, j: (0, 0))],
        out_specs=out_specs,
        out_shape=out_shape,
        compiler_params=_params("arbitrary", "arbitrary"),
        name="in_proj",
    )(h, w, w_forget)


def _swiglu_kernel(h_ref, wg_ref, wu_ref, o_ref):
    h = h_ref[...]
    gate = jnp.dot(h, wg_ref[...], preferred_element_type=F32)
    up = jnp.dot(h, wu_ref[...], preferred_element_type=F32)
    o_ref[...] = (gate * _sigmoid(gate) * up).astype(o_ref.dtype)


def _swiglu(h, w, layer, tm=1024, tn=512):
    M, D = h.shape
    F = w.shape[2] // 2
    tm, tn = _tile(M, tm), _tile(F, tn)
    nf = F // tn
    return pl.pallas_call(
        _swiglu_kernel,
        grid=(M // tm, nf),
        in_specs=[pl.BlockSpec((tm, D), lambda i, j: (i, 0)),
                  pl.BlockSpec((None, D, tn), lambda i, j: (layer, 0, j)),
                  pl.BlockSpec((None, D, tn), lambda i, j: (layer, 0, j + nf))],
        out_specs=pl.BlockSpec((tm, tn), lambda i, j: (i, j)),
        out_shape=jax.ShapeDtypeStruct((M, F), BF16),
        compiler_params=_params("parallel", "arbitrary"),
        name="swiglu",
    )(h, w, w)


def _matmul_res_kernel(a_ref, w_ref, r_ref, *rest, norm):
    out = r_ref[...] + jnp.dot(a_ref[...], w_ref[...], preferred_element_type=F32)
    if norm == "replace":
        g_ref, o_ref = rest
        o_ref[...] = _normalize_rows(out, g_ref[...])
    elif norm == "extra":
        g_ref, o_ref, h_ref = rest
        o_ref[...] = out
        h_ref[...] = _normalize_rows(out, g_ref[...]).astype(h_ref.dtype)
    else:
        rest[0][...] = out


def _matmul_res(a, w, layer, res, norm_gain=None, norm=None):
    M, K = a.shape
    N = w.shape[2]
    assert (norm_gain is None) == (norm is None)
    bytes_per_row = 2 * (2 * K + 4 * N + 4 * N + (2 * N if norm == "extra" else 0))
    tm = _rows_that_fit(M, 2 * K * N, bytes_per_row)
    row_tile = pl.BlockSpec((tm, N), lambda i: (i, 0))
    in_specs = [pl.BlockSpec((tm, K), lambda i: (i, 0)),
                pl.BlockSpec((None, K, N), lambda i: (layer, 0, 0), pipeline_mode=pl.Buffered(1)),
                row_tile]
    args = [a, w, res]
    out_specs, out_shape = [row_tile], [jax.ShapeDtypeStruct((M, N), F32)]
    if norm_gain is not None:
        in_specs.append(pl.BlockSpec((1, N), lambda i: (0, 0)))
        args.append(norm_gain)
    if norm == "extra":
        out_specs.append(row_tile)
        out_shape.append(jax.ShapeDtypeStruct((M, N), BF16))
    outs = pl.pallas_call(
        functools.partial(_matmul_res_kernel, norm=norm),
        grid=(M // tm,),
        in_specs=in_specs,
        out_specs=out_specs,
        out_shape=out_shape,
        compiler_params=_params("arbitrary"),
        name="matmul_res",
    )(*args)
    return outs if norm == "extra" else outs[0]


N_BIAS_COLS = 3
LOG2E = 1.4426950408889634


def _split3(x):
    hi = x.astype(BF16)
    rest = x - hi.astype(F32)
    mid = rest.astype(BF16)
    lo = (rest - mid.astype(F32)).astype(BF16)
    return jnp.concatenate([hi, mid, lo], axis=1)


def _forget_bias_kernel(f_ref, b_ref, tri_ref, sel_ref, o_ref, carry_ref):
    @pl.when(pl.program_id(1) == 0)
    def _():
        carry_ref[...] = jnp.zeros_like(carry_ref)

    x = _log_sigmoid(f_ref[...] + b_ref[...])
    cum3 = jnp.dot(tri_ref[...], _split3(x), preferred_element_type=F32)
    cum = (cum3[:, 0:LANES] + cum3[:, LANES:2 * LANES] + cum3[:, 2 * LANES:3 * LANES]
           + carry_ref[0:1, :])
    T = x.shape[0]
    carry_ref[0:1, :] = cum[T - 1:T, :]
    o_ref[...] = jnp.dot(_split3(cum * (-LOG2E)), sel_ref[...],
                         preferred_element_type=F32).astype(o_ref.dtype)


def _forget_bias(f, b, B, S, T=512):
    H = N_HEADS
    T = _tile(S, T)
    nt = S // T
    tri = jnp.tril(jnp.ones((T, T), BF16))
    r = jnp.arange(N_BIAS_COLS * LANES)[:, None]
    c = jnp.arange(H * LANES)[None, :]
    sel = ((r // LANES == c % LANES) & (r % LANES == c // LANES)).astype(BF16)
    return pl.pallas_call(
        _forget_bias_kernel,
        grid=(B, nt),
        in_specs=[pl.BlockSpec((T, LANES), lambda b_, t: (b_ * nt + t, 0)),
                  pl.BlockSpec((1, LANES), lambda b_, t: (0, 0)),
                  pl.BlockSpec((T, T), lambda b_, t: (0, 0)),
                  pl.BlockSpec(sel.shape, lambda b_, t: (0, 0))],
        out_specs=pl.BlockSpec((T, H * LANES), lambda b_, t: (b_ * nt + t, 0)),
        out_shape=jax.ShapeDtypeStruct((B * S, H * LANES), BF16),
        scratch_shapes=[pltpu.VMEM((8, LANES), F32)],
        compiler_params=_params("arbitrary", "arbitrary"),
        name="forget_bias",
    )(f, b, tri, sel)


def _attn_kernel(q_ref, k_ref, v_ref, kb_ref, o_ref, vt_ref, qt_ref, acc_ref, ml_ref, sbuf_ref, *,
                 tq, tk, unroll, ahead, tiles):
    nq = pl.num_programs(2) * tiles
    S = k_ref.shape[0]
    Dh = HEAD_DIM
    per_q = tq // tk
    assert 2 <= ahead <= per_q

    def one_tile(sub, carry):
        i = pl.program_id(2) * tiles + sub
        slot = i % 2

        def load_queries(tile, dst):
            rows = pl.multiple_of(tile * tq, tq)
            qt_ref[dst, 0:Dh, :] = q_ref[pl.ds(rows, tq), :].T

        def scores(j, qslot):
            start = pl.multiple_of(j * tk, tk)
            k_aug = jnp.concatenate([k_ref[pl.ds(start, tk), :], kb_ref[pl.ds(start, tk), :]],
                                    axis=1)
            return jnp.dot(k_aug, qt_ref[qslot], preferred_element_type=F32)

        @pl.when(i == 0)
        def _():
            for c in range(S // tq):
                vt_ref[:, c * tq:(c + 1) * tq] = v_ref[c * tq:(c + 1) * tq, :].T
            row = lax.broadcasted_iota(jnp.int32, (Dh, tq), 0)
            for dst in range(2):
                qt_ref[dst, Dh:2 * Dh, :] = jnp.where(row < N_BIAS_COLS, 1.0, 0.0).astype(BF16)
            load_queries(0, 0)
            for a in range(ahead):
                sbuf_ref[a] = scores(a, 0)

        load_queries(jnp.minimum(i + 1, nq - 1), 1 - slot)

        ml_ref[0:1, :] = jnp.full((1, tq), -jnp.inf, F32)
        ml_ref[1:2, :] = jnp.zeros((1, tq), F32)
        acc_ref[...] = jnp.zeros_like(acc_ref)

        def run(base, count, last=False):
            first_diag = count - per_q if last else count
            pending = {a: sbuf_ref[a] for a in range(ahead)}

            def max_pass(c, m_in):
                s = pending.pop(c)
                if c >= first_diag:
                    key = lax.broadcasted_iota(jnp.int32, (tk, tq), 0) + (c - first_diag) * tk
                    qry = lax.broadcasted_iota(jnp.int32, (tk, tq), 1)
                    s = jnp.where(key <= qry, s, -jnp.inf)
                return s, m_in, jnp.maximum(m_in, jnp.max(s, axis=0, keepdims=True))

            cur = max_pass(0, ml_ref[0:1, :])
            l = ml_ref[1:2, :]
            for c in range(count):
                if last and c + ahead >= count:
                    pending[c + ahead] = scores(c + ahead - count, 1 - slot)
                else:
                    pending[c + ahead] = scores(base + c + ahead, slot)
                nxt = max_pass(c + 1, cur[2]) if c + 1 < count else None
                s, m_old, m_new = cur
                p = jnp.exp2(s - m_new)
                alpha = jnp.exp2(m_old - m_new)
                l = alpha * l + jnp.sum(p, axis=0, keepdims=True)
                start = pl.multiple_of((base + c) * tk, tk)
                pv = jnp.dot(vt_ref[:, pl.ds(start, tk)], p.astype(BF16),
                             preferred_element_type=F32)
                acc_ref[...] = alpha * acc_ref[...] + pv
                cur = nxt
            ml_ref[0:1, :] = m_new
            ml_ref[1:2, :] = l
            for t in sorted(pending):
                sbuf_ref[t - count] = pending.pop(t)

        total = (i + 1) * per_q
        n_big = (total - 1) // (2 * unroll)

        def group(g, carry_):
            run(g * 2 * unroll, 2 * unroll)
            return carry_

        lax.fori_loop(0, n_big, group, 0)

        done = n_big * 2 * unroll
        has_mid = (total - done - 1) // unroll

        @pl.when(has_mid == 1)
        def _():
            run(done, unroll)

        done = done + has_mid * unroll
        tail = total - done
        for size in range(per_q, unroll + 1, per_q):
            @pl.when(tail == size)
            def _(size=size):
                run(done, size, last=True)

        rows = pl.multiple_of(sub * tq, tq)
        o_ref[pl.ds(rows, tq), :] = (acc_ref[...] / ml_ref[1:2, :]).T.astype(o_ref.dtype)
        return carry

    lax.fori_loop(0, tiles, one_tile, 0)


def _attention(qkv, kbias, B, S, tq=512, tk=256, unroll=16, ahead=2, tiles=8):
    H, Dh = N_HEADS, HEAD_DIM
    tq = _tile(S, tq)
    tk = _tile(tq, tk)
    unroll = max(unroll, tq // tk)
    tiles = _tile(S // tq, tiles)
    ns = S // (tq * tiles)
    return pl.pallas_call(
        functools.partial(_attn_kernel, tq=tq, tk=tk, unroll=unroll, ahead=ahead, tiles=tiles),
        grid=(B, H, ns),
        in_specs=[pl.BlockSpec((S, Dh), lambda b, h, i: (b, h)),
                  pl.BlockSpec((S, Dh), lambda b, h, i: (b, H + h)),
                  pl.BlockSpec((S, Dh), lambda b, h, i: (b, 2 * H + h)),
                  pl.BlockSpec((S, LANES), lambda b, h, i: (b, h))],
        out_specs=pl.BlockSpec((tq * tiles, Dh), lambda b, h, i: (b * ns + i, h)),
        out_shape=jax.ShapeDtypeStruct((B * S, H * Dh), BF16),
        scratch_shapes=[pltpu.VMEM((Dh, S), BF16), pltpu.VMEM((2, 2 * Dh, tq), BF16),
                        pltpu.VMEM((Dh, tq), F32), pltpu.VMEM((8, tq), F32),
                        pltpu.VMEM((ahead, tk, tq), F32)],
        compiler_params=_params("arbitrary", "arbitrary", "arbitrary"),
        name="fox_attention",
    )(qkv, qkv, qkv, kbias)


def _gelu_tanh(x):
    c = 0.7978845608028654
    return 0.5 * x * (1.0 + jnp.tanh(c * (x + 0.044715 * (x * x * x))))


def _rglru_kernel(x_ref, y_ref, cw_ref, cb_ref, wr_ref, br_ref, wi_ref, bi_ref, lam_ref,
                  o_ref, ext_ref, a_ref, b_ref, h_ref, state_ref, *, T):
    nb, _, C = x_ref.shape
    cb = C // N_RNN_BLOCKS

    @pl.when(pl.program_id(0) == 0)
    def _():
        ext_ref[:, 0:CONV_HALO, :] = jnp.zeros((nb, CONV_HALO, C), F32)
        state_ref[...] = jnp.zeros_like(state_ref)

    log_sig_lam = _log_sigmoid(lam_ref[...])
    for n_b in range(nb):
        x = x_ref[n_b]
        ext_ref[n_b, CONV_HALO:CONV_HALO + T, :] = x
        u = cb_ref[...] + x * cw_ref[CONV_WIDTH - 1:CONV_WIDTH, :]
        for k in range(CONV_WIDTH - 1):
            off = CONV_HALO + k - (CONV_WIDTH - 1)
            u = u + ext_ref[n_b, off:off + T, :] * cw_ref[k:k + 1, :]
        ext_ref[n_b, 0:CONV_HALO, :] = x[T - CONV_HALO:T, :]

        ub = u.astype(BF16)
        r_parts, i_parts = [], []
        for n in range(N_RNN_BLOCKS):
            un = ub[:, n * cb:(n + 1) * cb]
            r_parts.append(jnp.dot(un, wr_ref[n], preferred_element_type=F32))
            i_parts.append(jnp.dot(un, wi_ref[n], preferred_element_type=F32))
        r = _sigmoid(jnp.concatenate(r_parts, axis=1) + br_ref[...])
        ig = _sigmoid(jnp.concatenate(i_parts, axis=1) + bi_ref[...])
        log_a = (LRU_C * r) * log_sig_lam
        a = jnp.exp(log_a)
        a_ref[n_b] = a
        b_ref[n_b] = jnp.sqrt(-jnp.tanh(log_a) * (a * a + 1.0)) * (ig * u)

    def group(g, hs):
        base = pl.multiple_of(g * 8, 8)
        a8 = [a_ref[n_b, pl.ds(base, 8), :] for n_b in range(nb)]
        b8 = [b_ref[n_b, pl.ds(base, 8), :] for n_b in range(nb)]
        hs = list(hs)
        rows = [[] for _ in range(nb)]
        for s in range(8):
            for n_b in range(nb):
                hs[n_b] = a8[n_b][s:s + 1, :] * hs[n_b] + b8[n_b][s:s + 1, :]
                rows[n_b].append(hs[n_b])
        for n_b in range(nb):
            h_ref[n_b, pl.ds(base, 8), :] = jnp.concatenate(rows[n_b], axis=0)
        return tuple(hs)

    h_last = lax.fori_loop(0, T // 8, group, tuple(state_ref[n_b, 0:1, :] for n_b in range(nb)))
    for n_b in range(nb):
        state_ref[n_b, 0:1, :] = h_last[n_b]
        o_ref[n_b] = (_gelu_tanh(y_ref[n_b]) * h_ref[n_b]).astype(o_ref.dtype)


def _rglru(xy, conv_w, conv_b, w_rg, b_rg, w_ig, b_ig, lam, B, S, T=512):
    C = conv_w.shape[1]
    T = _tile(S, T)
    xy = xy.reshape(B, S, xy.shape[1])
    vec = pl.BlockSpec((1, C), lambda t: (0, 0))
    blockdiag = pl.BlockSpec(w_rg.shape, lambda t: (0, 0, 0))
    out = pl.pallas_call(
        functools.partial(_rglru_kernel, T=T),
        grid=(S // T,),
        in_specs=[pl.BlockSpec((B, T, C), lambda t: (0, t, 0)),
                  pl.BlockSpec((B, T, C), lambda t: (0, t, 1)),
                  pl.BlockSpec((CONV_WIDTH, C), lambda t: (0, 0)),
                  vec, blockdiag, vec, blockdiag, vec, vec],
        out_specs=pl.BlockSpec((B, T, C), lambda t: (0, t, 0)),
        out_shape=jax.ShapeDtypeStruct((B, S, C), BF16),
        scratch_shapes=[pltpu.VMEM((B, T + CONV_HALO, C), F32),
                        pltpu.VMEM((B, T, C), F32), pltpu.VMEM((B, T, C), F32),
                        pltpu.VMEM((B, T, C), F32), pltpu.VMEM((B, 8, C), F32)],
        compiler_params=_params("arbitrary"),
        name="rglru",
    )(xy, xy, conv_w, conv_b, w_rg, b_rg, w_ig, b_ig, lam)
    return out.reshape(B * S, C)


def _pool_kernel(u_ref, w_ref, s_ref, o_ref, ext_ref, *, T):
    C = u_ref.shape[1]
    G = len(POOL_WINDOWS)
    cg = C // G
    t = pl.program_id(1)

    @pl.when(t == 0)
    def _():
        ext_ref[0:POOL_HALO, :] = jnp.zeros((POOL_HALO, C), F32)

    x = u_ref[...]
    ext_ref[POOL_HALO:POOL_HALO + T, :] = x
    pos = (t * T + 1 + lax.broadcasted_iota(jnp.int32, (T, 1), 0)).astype(F32)
    for g, w in enumerate(POOL_WINDOWS):
        lo = g * cg
        xs = x[:, lo:lo + cg]
        acc = xs
        for k in range(1, w):
            acc = acc + ext_ref[POOL_HALO - k:POOL_HALO - k + T, lo:lo + cg]
        pooled = acc / jnp.minimum(pos, float(w)) - xs
        y = jnp.dot(pooled.astype(BF16), w_ref[g], preferred_element_type=F32)
        o_ref[:, lo:lo + cg] = (y * s_ref[:, lo:lo + cg]).astype(o_ref.dtype)
    ext_ref[0:POOL_HALO, :] = x[T - POOL_HALO:T, :]


def _pool(u, w_pool, scale, B, S, col_block=0, T=512):
    C = scale.shape[1]
    T = _tile(S, T)
    nt = S // T
    return pl.pallas_call(
        functools.partial(_pool_kernel, T=T),
        grid=(B, nt),
        in_specs=[pl.BlockSpec((T, C), lambda b, t: (b * nt + t, col_block)),
                  pl.BlockSpec(w_pool.shape, lambda b, t: (0, 0, 0)),
                  pl.BlockSpec((1, C), lambda b, t: (0, 0))],
        out_specs=pl.BlockSpec((T, C), lambda b, t: (b * nt + t, 0)),
        out_shape=jax.ShapeDtypeStruct((B * S, C), BF16),
        scratch_shapes=[pltpu.VMEM((T + POOL_HALO, C), F32)],
        compiler_params=_params("arbitrary", "arbitrary"),
        name="multiscale_pool",
    )(u, w_pool, scale)


def _merge_kernel(ya_ref, yb_ref, yc_ref, wa_ref, wb_ref, wc_ref, ga_ref, gb_ref, gc_ref, o_ref):
    out = _sigmoid(ga_ref[...]) * jnp.dot(ya_ref[...], wa_ref[...], preferred_element_type=F32)
    out = out + _sigmoid(gb_ref[...]) * jnp.dot(yb_ref[...], wb_ref[...], preferred_element_type=F32)
    out = out + _sigmoid(gc_ref[...]) * jnp.dot(yc_ref[...], wc_ref[...], preferred_element_type=F32)
    o_ref[...] = out.astype(o_ref.dtype)


def _merge(ya, yb, yc, wa, wb, wc, layer, gates):
    M, C = ya.shape
    D = wa.shape[2]
    tm = _rows_that_fit(M, 2 * 3 * 2 * C * D, 2 * (3 * 2 * C + 3 * 4 * D + 2 * D))
    y_spec = pl.BlockSpec((tm, C), lambda i: (i, 0))
    w_spec = pl.BlockSpec((None, C, D), lambda i: (layer, 0, 0))
    return pl.pallas_call(
        _merge_kernel,
        grid=(M // tm,),
        in_specs=[y_spec, y_spec, y_spec, w_spec, w_spec, w_spec,
                  pl.BlockSpec((tm, D), lambda i: (i, 0)),
                  pl.BlockSpec((tm, D), lambda i: (i, 1)),
                  pl.BlockSpec((tm, D), lambda i: (i, 2))],
        out_specs=pl.BlockSpec((tm, D), lambda i: (i, 0)),
        out_shape=jax.ShapeDtypeStruct((M, D), BF16),
        compiler_params=_params("arbitrary"),
        name="gated_merge",
    )(ya, yb, yc, wa, wb, wc, gates, gates, gates)


def _layer(l, x, B, S, g_mix, w_in, b_forget, conv_w, conv_b, w_rg, b_rg, w_ig, b_ig, lam,
           w_pool, pool_scale, w_a, w_b, w_c, w_out, g_ffn, w_ffn_in, w_ffn_out, g_out=None):
    D = x.shape[1]
    H = N_HEADS
    d_rnn = conv_w.shape[1]
    d_attn = H * HEAD_DIM
    d_pool = pool_scale.shape[0]
    row = lambda v: v.reshape(1, -1)

    o_rnn = 2 * d_rnn
    o_qkv = o_rnn + 3 * d_attn
    o_f = o_qkv + H
    o_pool = o_f + d_pool
    w_f32 = jnp.concatenate([w_in[l, :, :o_rnn], w_in[l, :, o_f:]], axis=1)
    w_forget = jnp.pad(w_in[l, :, o_qkv:o_f], ((0, 0), (0, LANES - H)))
    qkv_scale = jnp.ones((1, 3 * d_attn), F32).at[:, :d_attn].set(HEAD_DIM ** -0.5 * LOG2E)
    w_qkv = w_in[l:l + 1, :, o_rnn:o_qkv]
    qkv, h = _norm_matmul(x, row(g_mix), w_qkv, 0, 0, 3 * d_attn, qkv_scale, BF16)
    assert o_rnn % d_pool == 0
    xyp, gates, f_logit = _in_proj(h, w_f32, w_forget, (o_rnn + d_pool, 3 * D))

    y_a = _rglru(xyp, conv_w, row(conv_b), w_rg.astype(BF16), row(b_rg), w_ig.astype(BF16),
                 row(b_ig), row(lam), B, S)

    kbias = _forget_bias(f_logit, jnp.pad(b_forget, (0, LANES - H)).reshape(1, LANES), B, S)
    y_b = _attention(qkv, kbias, B, S)

    y_c = _pool(xyp, w_pool.astype(BF16), row(pool_scale), B, S, col_block=o_rnn // d_pool)

    merged = _merge(y_a, y_b, y_c, w_a, w_b, w_c, l, gates)
    x, h2 = _matmul_res(merged, w_out, l, x, norm_gain=row(g_ffn), norm="extra")
    act = _swiglu(h2, w_ffn_in, l)
    return _matmul_res(act, w_ffn_out, l, x, norm_gain=g_out,
                       norm=None if g_out is None else "replace")


def kernel(x, g_mix, w_in, b_forget, conv_w, conv_b, w_rg, b_rg, w_ig, b_ig, lru_lambda, w_pool,
           pool_scale, w_branch_rnn, w_branch_attn, w_branch_pool, w_out, g_ffn, w_ffn_in,
           w_ffn_out, g_final):
    B, S, D = x.shape
    h = x.reshape(B * S, D)
    depth = g_mix.shape[0]
    big = [w.astype(BF16) for w in (w_in, w_branch_rnn, w_branch_attn, w_branch_pool, w_out,
                                    w_ffn_in, w_ffn_out)]
    w_in, w_branch_rnn, w_branch_attn, w_branch_pool, w_out, w_ffn_in, w_ffn_out = big
    for l in range(depth):
        h = _layer(l, h, B, S, g_mix[l], w_in, b_forget[l], conv_w[l], conv_b[l], w_rg[l], b_rg[l],
                   w_ig[l], b_ig[l], lru_lambda[l], w_pool[l], pool_scale[l], w_branch_rnn,
                   w_branch_attn, w_branch_pool, w_out, g_ffn[l], w_ffn_in, w_ffn_out,
                   g_out=g_final.reshape(1, -1) if l == depth - 1 else None)
    return h.reshape(B, S, D)
```
